```python
import jax, jax.numpy as jnp
from jax import lax
import numpy as np

D_MODEL = 1024
BATCH = 8
SEQ = 2048
DEPTH = 1
DEC_BATCH = 8
DEC_SEQ = 4096
PAST_LEN = 128

ATTN_GROUPS = ((128, 1), (512, 4), (2048, 16))
N_GROUPS = 3
ATTN_HEADS_PER_GROUP = 8
ATTN_HEAD_DIM = 64
ATTN_HEADS = N_GROUPS * ATTN_HEADS_PER_GROUP
ATTN_WIDTH = ATTN_HEADS * ATTN_HEAD_DIM
ATTN_OUT = ATTN_HEADS_PER_GROUP * ATTN_HEAD_DIM
ROPE_THETA = 10000.0
NEG = -1e30
DN_HEADS = 8
DN_HEAD_DIM = 128
DN_WIDTH = DN_HEADS * DN_HEAD_DIM
DN_CONV = 3
CHUNK = 64
D_FF = 2816
FFN_CONV = 3
EPS = 1e-6
SPLIT_SIZES = (ATTN_WIDTH, ATTN_WIDTH, ATTN_WIDTH, DN_WIDTH, DN_WIDTH, DN_WIDTH, DN_WIDTH,
               2 * DN_HEADS, 2 * DN_HEADS, D_MODEL, D_MODEL)
IN_COLS = 3 * ATTN_WIDTH + 4 * DN_WIDTH + 4 * DN_HEADS + 2 * D_MODEL
SPLIT_IDX = (ATTN_WIDTH, 2 * ATTN_WIDTH, 3 * ATTN_WIDTH,
             3 * ATTN_WIDTH + DN_WIDTH, 3 * ATTN_WIDTH + 2 * DN_WIDTH, 3 * ATTN_WIDTH + 3 * DN_WIDTH,
             3 * ATTN_WIDTH + 4 * DN_WIDTH, 3 * ATTN_WIDTH + 4 * DN_WIDTH + 2 * DN_HEADS,
             3 * ATTN_WIDTH + 4 * DN_WIDTH + 4 * DN_HEADS,
             3 * ATTN_WIDTH + 4 * DN_WIDTH + 4 * DN_HEADS + D_MODEL)

kernel_name = "hybrid_dilated_attn_gated_deltanet_encoder"

F32 = jnp.float32


def rmsnorm(x, g):
    xf = x.astype(F32)
    y = xf * lax.rsqrt(jnp.mean(xf * xf, axis=-1, keepdims=True) + EPS)
    return (y * g.astype(F32)).astype(x.dtype)


def l2norm(t):
    return t * lax.rsqrt(jnp.sum(t * t, axis=-1, keepdims=True) + EPS)


def rope(t, S):
    dh = t.shape[-1]
    half = dh // 2
    inv = ROPE_THETA ** (-jnp.arange(half, dtype=F32) / half)
    ang = jnp.arange(S, dtype=F32)[:, None] * inv[None, :]
    cos = jnp.cos(ang)[None, :, None, :]
    sin = jnp.sin(ang)[None, :, None, :]
    t1 = t[..., :half].astype(F32)
    t2 = t[..., half:].astype(F32)
    return jnp.concatenate([t1 * cos - t2 * sin, t2 * cos + t1 * sin], axis=-1).astype(t.dtype)


def dwconv_centered(x, w):
    K = w.shape[0]
    p = K // 2
    S = x.shape[1]
    xp = jnp.pad(x, ((0, 0), (p, p), (0, 0)))
    y = xp[:, 0:S] * w[0]
    for j in range(1, K):
        y = y + xp[:, j:j + S] * w[j]
    return y


def banded_window_attention(q, k, v, half):
    Bz, L, H, dh = q.shape
    nb = -(-L // half)
    pad = nb * half - L
    qb = jnp.pad(q, ((0, 0), (0, pad), (0, 0), (0, 0))).reshape(Bz, nb, half, H, dh)

    def windows(t):
        tp = jnp.pad(t, ((0, 0), (half, pad + half), (0, 0), (0, 0))).reshape(Bz, nb + 2, half, H, dh)
        return jnp.concatenate([tp[:, :-2], tp[:, 1:-1], tp[:, 2:]], axis=2)

    kw, vw = windows(k), windows(v)
    s = jnp.einsum('bnqhd,bnkhd->bnhqk', qb, kw).astype(F32) * (dh ** -0.5)
    blk = jnp.arange(nb)[:, None, None] * half
    qpos = blk + jnp.arange(half)[None, :, None]
    kpos = blk - half + jnp.arange(3 * half)[None, None, :]
    valid = (jnp.abs(qpos - kpos) <= half) & (kpos >= 0) & (kpos < L)
    s = jnp.where(valid[None, :, None], s, NEG)
    m = jnp.max(s, axis=-1, keepdims=True)
    p = jnp.exp(s - m)
    den = jnp.sum(p, axis=-1)
    o = jnp.einsum('bnhqk,bnkhd->bnqhd', p, vw.astype(F32)) / jnp.swapaxes(den, 2, 3)[..., None]
    lse = jnp.swapaxes(m[..., 0] + jnp.log(den), 2, 3)
    o = o.reshape(Bz, nb * half, H, dh)[:, :L]
    lse = lse.reshape(Bz, nb * half, H)[:, :L]
    return o, lse


def dilated_window_attention(q, k, v, window, dilation):
    B, S, H, dh = q.shape
    L = S // dilation

    def to_res(t):
        return t.reshape(B, L, dilation, H, dh).transpose(0, 2, 1, 3, 4).reshape(B * dilation, L, H, dh)

    o, lse = banded_window_attention(to_res(q), to_res(k), to_res(v), window // (2 * dilation))
    o = o.reshape(B, dilation, L, H, dh).transpose(0, 2, 1, 3, 4).reshape(B, S, H, dh)
    lse = lse.reshape(B, dilation, L, H).transpose(0, 2, 1, 3).reshape(B, S, H)
    return o, lse


def gated_delta_chunked(q, k, v, g, beta):
    q, k, v, g, beta = (t.astype(F32) for t in (q, k, v, g, beta))
    Z, S, H, dk = q.shape
    dv = v.shape[-1]
    C = CHUNK
    N = S // C
    q = q.reshape(Z, N, C, H, dk)
    k = k.reshape(Z, N, C, H, dk)
    v = v.reshape(Z, N, C, H, dv)
    beta = beta.reshape(Z, N, C, H)
    G = jnp.cumsum(g.reshape(Z, N, C, H), axis=2)
    Gh = jnp.swapaxes(G, 2, 3)
    tril = jnp.tril(jnp.ones((C, C), dtype=bool))
    strict = jnp.tril(jnp.ones((C, C), dtype=bool), k=-1)
    decay = jnp.exp(jnp.where(tril, Gh[..., :, None] - Gh[..., None, :], -jnp.inf))
    kb = k * beta[..., None]
    A = jnp.where(strict, jnp.einsum('znchk,zndhk->znhcd', kb, k) * decay, 0.0)
    eye = jnp.eye(C, dtype=F32)
    T = lax.linalg.triangular_solve(A + eye, jnp.broadcast_to(eye, A.shape), left_side=True, lower=True)
    U = jnp.einsum('znhcd,zndhv->znchv', T, v * beta[..., None])
    W = jnp.einsum('znhcd,zndhk->znchk', T, kb * jnp.exp(G)[..., None])
    qG = q * jnp.exp(G)[..., None]
    P = jnp.einsum('znchk,zndhk->znhcd', q, k) * decay
    kD = k * jnp.exp(G[:, :, -1:] - G)[..., None]
    gl = jnp.exp(G[:, :, -1])
    xs = tuple(jnp.moveaxis(t, 1, 0) for t in (qG, P, U, W, kD, gl))

    def step(state, xs_n):
        qG_n, P_n, U_n, W_n, kD_n, gl_n = xs_n
        v_new = U_n - jnp.einsum('zchk,zhkv->zchv', W_n, state)
        o = jnp.einsum('zchk,zhkv->zchv', qG_n, state) + jnp.einsum('zhcd,zdhv->zchv', P_n, v_new)
        state = state * gl_n[..., None, None] + jnp.einsum('zchk,zchv->zhkv', kD_n, v_new)
        return state, o

    _, o = lax.scan(step, jnp.zeros((Z, H, dk, dv), F32), xs)
    return jnp.moveaxis(o, 0, 1).reshape(Z, S, H, dv)


def token_mixer(h, w_in, conv_qkv_w, a_log, dt_bias, dn_norm_g, w_br_attn, w_br_dn, w_out):
    B, S, _ = h.shape
    proj = h @ w_in
    qa, ka, va, qd, kd, vd, zd, ad, bd, ga, gd = jnp.split(proj, SPLIT_IDX, axis=-1)

    qa = rope(qa.reshape(B, S, ATTN_HEADS, ATTN_HEAD_DIM), S)
    ka = rope(ka.reshape(B, S, ATTN_HEADS, ATTN_HEAD_DIM), S)
    va = va.reshape(B, S, ATTN_HEADS, ATTN_HEAD_DIM)
    outs, lses = [], []
    for gi, (window, dil) in enumerate(ATTN_GROUPS):
        sl = slice(gi * ATTN_HEADS_PER_GROUP, (gi + 1) * ATTN_HEADS_PER_GROUP)
        o, lse = dilated_window_attention(qa[:, :, sl], ka[:, :, sl], va[:, :, sl], window, dil)
        outs.append(o)
        lses.append(lse)
    wts = jax.nn.softmax(jnp.stack(lses, axis=0), axis=0)
    ya = jnp.einsum('gbsh,gbshd->bshd', wts, jnp.stack(outs, axis=0))
    ya = ya.reshape(B, S, ATTN_OUT).astype(h.dtype)

    qkv = jax.nn.silu(dwconv_centered(jnp.concatenate([qd, kd, vd], axis=-1), conv_qkv_w))
    qd, kd, vd = jnp.split(qkv, 3, axis=-1)
    qd = l2norm(qd.reshape(B, S, DN_HEADS, DN_HEAD_DIM).astype(F32)) * (DN_HEAD_DIM ** -0.5)
    kd = l2norm(kd.reshape(B, S, DN_HEADS, DN_HEAD_DIM).astype(F32))
    vd = vd.reshape(B, S, DN_HEADS, DN_HEAD_DIM).astype(F32)
    g = -jnp.exp(a_log.astype(F32)) * jax.nn.softplus(
        ad.reshape(B, S, 2, DN_HEADS).astype(F32) + dt_bias.astype(F32))
    beta = jax.nn.sigmoid(bd.reshape(B, S, 2, DN_HEADS).astype(F32))
    rev = lambda t: jnp.flip(t, axis=1)
    q2 = jnp.concatenate([qd, rev(qd)], axis=0)
    k2 = jnp.concatenate([kd, rev(kd)], axis=0)
    v2 = jnp.concatenate([vd, rev(vd)], axis=0)
    g2 = jnp.concatenate([g[:, :, 0], rev(g[:, :, 1])], axis=0)
    b2 = jnp.concatenate([beta[:, :, 0], rev(beta[:, :, 1])], axis=0)
    o2 = gated_delta_chunked(q2, k2, v2, g2, b2)
    od = o2[:B] + rev(o2[B:])
    z = zd.reshape(B, S, DN_HEADS, DN_HEAD_DIM).astype(F32)
    od = od * lax.rsqrt(jnp.mean(od * od, axis=-1, keepdims=True) + EPS) * dn_norm_g.astype(F32) * jax.nn.silu(z)
    yd = od.reshape(B, S, DN_WIDTH).astype(h.dtype)

    merged = jax.nn.sigmoid(ga) * (ya @ w_br_attn) + jax.nn.sigmoid(gd) * (yd @ w_br_dn)
    return merged @ w_out


def conv_ffn(h, w_up, ffn_conv_w, ffn_conv_b, w_down):
    up = dwconv_centered(h @ w_up, ffn_conv_w) + ffn_conv_b
    val, gate = jnp.split(up, 2, axis=-1)
    return (jax.nn.silu(gate) * val) @ w_down


def block(x, c, w_ada, b_ada, norm1_g, w_in, conv_qkv_w, a_log, dt_bias, dn_norm_g,
          w_br_attn, w_br_dn, w_out, norm2_g, w_up, ffn_conv_w, ffn_conv_b, w_down):
    mod = jax.nn.silu(c) @ w_ada + b_ada
    sh1, sc1, gt1, sh2, sc2, gt2 = (m[:, None, :] for m in jnp.split(mod, 6, axis=-1))
    h = rmsnorm(x, norm1_g) * (1 + sc1) + sh1
    x = x + gt1 * token_mixer(h, w_in, conv_qkv_w, a_log, dt_bias, dn_norm_g, w_br_attn, w_br_dn, w_out)
    h = rmsnorm(x, norm2_g) * (1 + sc2) + sh2
    x = x + gt2 * conv_ffn(h, w_up, ffn_conv_w, ffn_conv_b, w_down)
    return x


def trunk(x, c, w_ada, b_ada, norm1_g, w_in, conv_qkv_w, a_log, dt_bias, dn_norm_g,
          w_br_attn, w_br_dn, w_out, norm2_g, w_up, ffn_conv_w, ffn_conv_b, w_down, norm_f_g):
    for l in range(DEPTH):
        x = block(x, c, w_ada[l], b_ada[l], norm1_g[l], w_in[l], conv_qkv_w[l], a_log[l], dt_bias[l],
                  dn_norm_g[l], w_br_attn[l], w_br_dn[l], w_out[l], norm2_g[l], w_up[l],
                  ffn_conv_w[l], ffn_conv_b[l], w_down[l])
    return rmsnorm(x, norm_f_g)


def setup_inputs(seed: int = 0) -> dict:
    key = jax.random.key(seed)
    ks = jax.random.split(key, 24)
    nrm = lambda k, shape, scale: jax.random.normal(k, shape, F32) * scale
    D = D_MODEL
    dt = jax.random.uniform(ks[10], (DEPTH, 2, DN_HEADS), F32, minval=0.001, maxval=0.1)
    return {
        "x_prompt": nrm(ks[0], (BATCH, SEQ, D), 1.0),
        "x_sample": nrm(ks[1], (DEC_BATCH, DEC_SEQ, D), 1.0),
        "c_prompt": nrm(ks[2], (BATCH, D), 1.0),
        "c_sample": nrm(ks[3], (DEC_BATCH, D), 1.0),
        "w_ada": nrm(ks[4], (DEPTH, D, 6 * D), D ** -0.5),
        "b_ada": nrm(ks[5], (DEPTH, 6 * D), 0.01),
        "norm1_g": 1.0 + nrm(ks[6], (DEPTH, D), 0.01),
        "w_in": nrm(ks[7], (DEPTH, D, IN_COLS), D ** -0.5),
        "conv_qkv_w": nrm(ks[8], (DEPTH, DN_CONV, 3 * DN_WIDTH), DN_CONV ** -0.5),
        "a_log": jnp.log(jax.random.uniform(ks[9], (DEPTH, 2, DN_HEADS), F32, minval=1.0, maxval=16.0)),
        "dt_bias": jnp.log(jnp.expm1(dt)),
        "dn_norm_g": 1.0 + nrm(ks[11], (DEPTH, DN_HEAD_DIM), 0.01),
        "w_br_attn": nrm(ks[12], (DEPTH, ATTN_OUT, D), ATTN_OUT ** -0.5),
        "w_br_dn": nrm(ks[13], (DEPTH, DN_WIDTH, D), DN_WIDTH ** -0.5),
        "w_out": nrm(ks[14], (DEPTH, D, D), D ** -0.5),
        "norm2_g": 1.0 + nrm(ks[15], (DEPTH, D), 0.01),
        "w_up": nrm(ks[16], (DEPTH, D, 2 * D_FF), D ** -0.5),
        "ffn_conv_w": nrm(ks[17], (DEPTH, FFN_CONV, 2 * D_FF), FFN_CONV ** -0.5),
        "ffn_conv_b": nrm(ks[18], (DEPTH, 2 * D_FF), 0.01),
        "w_down": nrm(ks[19], (DEPTH, D_FF, D), D_FF ** -0.5),
        "norm_f_g": 1.0 + nrm(ks[20], (D,), 0.01),
    }


def reference(x_prompt, x_sample, c_prompt, c_sample, w_ada, b_ada, norm1_g, w_in, conv_qkv_w,
              a_log, dt_bias, dn_norm_g, w_br_attn, w_br_dn, w_out, norm2_g, w_up, ffn_conv_w,
              ffn_conv_b, w_down, norm_f_g):
    y_prompt = trunk(x_prompt, c_prompt, w_ada, b_ada, norm1_g, w_in, conv_qkv_w, a_log, dt_bias,
                     dn_norm_g, w_br_attn, w_br_dn, w_out, norm2_g, w_up, ffn_conv_w, ffn_conv_b,
                     w_down, norm_f_g)
    y_sample = trunk(x_sample, c_sample, w_ada, b_ada, norm1_g, w_in, conv_qkv_w, a_log, dt_bias,
                     dn_norm_g, w_br_attn, w_br_dn, w_out, norm2_g, w_up, ffn_conv_w, ffn_conv_b,
                     w_down, norm_f_g)
    return (y_prompt, y_sample)
```

```python
import functools

import jax
import jax.numpy as jnp
from jax import lax
from jax.experimental import pallas as pl
from jax.experimental.pallas import tpu as pltpu

F32 = jnp.float32
BF16 = jnp.bfloat16
HIGHEST = lax.Precision.HIGHEST

D_MODEL = 1024
ATTN_GROUPS = ((128, 1), (512, 4), (2048, 16))
N_GROUPS = 3
HEAD_DIM = 64
GROUP_W = 512
ATTN_W = N_GROUPS * GROUP_W
ATTN_HALF = 64
ATTN_BQ = 128
ROPE_THETA = 10000.0
NEG = -1e30
DN_HEADS = 8
DN_HEAD_DIM = 128
DN_W = DN_HEADS * DN_HEAD_DIM
DN_CHUNK = 256
DN_BASE = 16
D_FF = 2816
EPS = 1e-6
LANES = 128
VMEM_LIMIT = 56 * 1024 * 1024


def _cparams():
    return pltpu.CompilerParams(vmem_limit_bytes=VMEM_LIMIT)


def _sigmoid(x):
    return 1.0 / (1.0 + jnp.exp(-x))


def _dot(a, b):
    return jnp.dot(a.astype(BF16), b.astype(BF16), preferred_element_type=F32)


def _dot_nt(a, b):
    return lax.dot_general(a.astype(BF16), b.astype(BF16), (((1,), (1,)), ((), ())),
                           preferred_element_type=F32)


def _dot_tn(a, b):
    return lax.dot_general(a.astype(BF16), b.astype(BF16), (((0,), (0,)), ((), ())),
                           preferred_element_type=F32)


def _rms(x, g):
    return x * lax.rsqrt(jnp.mean(x * x, axis=-1, keepdims=True) + EPS) * g


def _ada_kernel(c_ref, w_ref, b_ref, o_ref):
    c = c_ref[...]
    s = c * _sigmoid(c)
    o_ref[...] = jnp.dot(s, w_ref[...], preferred_element_type=F32, precision=HIGHEST) + b_ref[...]


def _ada(c, w, b):
    bt, d = c.shape
    n = w.shape[1]
    tn = 1024
    return pl.pallas_call(
        _ada_kernel,
        grid=(n // tn,),
        in_specs=[pl.BlockSpec((bt, d), lambda j: (0, 0)),
                  pl.BlockSpec((d, tn), lambda j: (0, j)),
                  pl.BlockSpec((1, tn), lambda j: (0, j))],
        out_specs=pl.BlockSpec((bt, tn), lambda j: (0, j)),
        out_shape=jax.ShapeDtypeStruct((bt, n), F32),
        name="ada",
    )(c, w, b)


def _rope_table_kernel(inv_ref, cos_ref, sin_ref):
    tm = cos_ref.shape[0]
    pos = (pl.program_id(0) * tm + lax.broadcasted_iota(jnp.int32, (tm, LANES), 0)).astype(F32)
    ang = pos * inv_ref[...]
    lane = lax.broadcasted_iota(jnp.int32, (tm, LANES), 1)
    sign = jnp.where((lane & (HEAD_DIM - 1)) < HEAD_DIM // 2, -1.0, 1.0)
    cos_ref[...] = jnp.cos(ang)
    sin_ref[...] = jnp.sin(ang) * sign


def _rope_tables(s):
    half = HEAD_DIM // 2
    inv = ROPE_THETA ** (-(jnp.arange(LANES) % half).astype(F32) / half)
    tm = 512
    return pl.pallas_call(
        _rope_table_kernel,
        grid=(s // tm,),
        in_specs=[pl.BlockSpec((1, LANES), lambda i: (0, 0))],
        out_specs=[pl.BlockSpec((tm, LANES), lambda i: (i, 0))] * 2,
        out_shape=[jax.ShapeDtypeStruct((s, LANES), F32)] * 2,
        name="rope_tables",
    )(inv.reshape(1, LANES))


def _norm_mod_kernel(x_ref, mod_ref, g_ref, o_ref):
    h = _rms(x_ref[0], g_ref[...]) * (1.0 + mod_ref[0, 1:2, :]) + mod_ref[0, 0:1, :]
    o_ref[0] = h.astype(o_ref.dtype)


def _norm_mod(x, mod, g):
    b, s, d = x.shape
    tm = 512
    return pl.pallas_call(
        _norm_mod_kernel,
        grid=(b, s // tm),
        in_specs=[pl.BlockSpec((1, tm, d), lambda bi, i: (bi, i, 0)),
                  pl.BlockSpec((1, 6, d), lambda bi, i: (bi, 0, 0)),
                  pl.BlockSpec((1, d), lambda bi, i: (0, 0))],
        out_specs=pl.BlockSpec((1, tm, d), lambda bi, i: (bi, i, 0)),
        out_shape=jax.ShapeDtypeStruct((b, s, d), BF16),
        name="norm_mod",
    )(x, mod, g)


def _mm_kernel(h_ref, w_ref, o_ref):
    o_ref[0] = jnp.dot(h_ref[0], w_ref[...], preferred_element_type=F32).astype(o_ref.dtype)


def _mm_rope_kernel(h_ref, w_ref, cos_ref, sin_ref, o_ref):
    acc = jnp.dot(h_ref[0], w_ref[...], preferred_element_type=F32)
    cos = cos_ref[...]
    sin = sin_ref[...]
    lane = lax.broadcasted_iota(jnp.int32, cos.shape, 1)
    first_half = (lane & (HEAD_DIM - 1)) < HEAD_DIM // 2
    for c in range(acc.shape[1] // LANES):
        t = acc[:, c * LANES:(c + 1) * LANES]
        partner = jnp.where(first_half, pltpu.roll(t, LANES - HEAD_DIM // 2, 1),
                            pltpu.roll(t, HEAD_DIM // 2, 1))
        o_ref[0, :, c * LANES:(c + 1) * LANES] = (t * cos + partner * sin).astype(o_ref.dtype)


def _mm(h, w, out_dtype, rope=None):
    b, s, k = h.shape
    n = w.shape[1]
    tm, tn = 1024, 512
    in_specs = [pl.BlockSpec((1, tm, k), lambda bi, i, j: (bi, i, 0)),
                pl.BlockSpec((k, tn), lambda bi, i, j: (0, j))]
    args = [h, w]
    kern = _mm_kernel
    if rope is not None:
        in_specs += [pl.BlockSpec((tm, LANES), lambda bi, i, j: (i, 0))] * 2
        args += list(rope)
        kern = _mm_rope_kernel
    return pl.pallas_call(
        kern,
        grid=(b, s // tm, n // tn),
        in_specs=in_specs,
        out_specs=pl.BlockSpec((1, tm, tn), lambda bi, i, j: (bi, i, j)),
        out_shape=jax.ShapeDtypeStruct((b, s, n), out_dtype),
        compiler_params=_cparams(),
        name="proj_rope" if rope is not None else "proj",
    )(*args)


def _gates_kernel(h_ref, w_ref, a_ref, dt_ref, o_ref):
    acc = jnp.dot(h_ref[0], w_ref[...], preferred_element_type=F32)
    c = acc.shape[0]
    x = acc + dt_ref[...]
    softplus = jnp.maximum(x, 0.0) + jnp.log1p(jnp.exp(-jnp.abs(x)))
    g = -jnp.exp(a_ref[...]) * softplus
    beta = _sigmoid(acc)
    r = lax.broadcasted_iota(jnp.int32, (c, c), 0)
    cc = lax.broadcasted_iota(jnp.int32, (c, c), 1)
    pre = jnp.dot(jnp.where(cc <= r, 1.0, 0.0).astype(F32), g, preferred_element_type=F32,
                  precision=HIGHEST)
    suf = jnp.dot(jnp.where(cc >= r, 1.0, 0.0).astype(F32), g, preferred_element_type=F32,
                  precision=HIGHEST)
    lane = lax.broadcasted_iota(jnp.int32, acc.shape, 1)
    o_ref[0] = jnp.where(lane < DN_HEADS, pre,
                         jnp.where(lane < 2 * DN_HEADS, suf,
                                   jnp.where(lane < 4 * DN_HEADS, beta, 0.0)))


def _gates(h, w, a_row, dt_row):
    b, s, k = h.shape
    tm = DN_CHUNK
    return pl.pallas_call(
        _gates_kernel,
        grid=(b, s // tm),
        in_specs=[pl.BlockSpec((1, tm, k), lambda bi, i: (bi, i, 0)),
                  pl.BlockSpec((k, LANES), lambda bi, i: (0, 0)),
                  pl.BlockSpec((1, LANES), lambda bi, i: (0, 0)),
                  pl.BlockSpec((1, LANES), lambda bi, i: (0, 0))],
        out_specs=pl.BlockSpec((1, tm, LANES), lambda bi, i: (bi, i, 0)),
        out_shape=jax.ShapeDtypeStruct((b, s, LANES), F32),
        name="dn_gates",
    )(h, w, a_row, dt_row)


def _attn_kernel(q_ref, k_ref, v_ref, o_ref, l_ref, *, seq, kb):
    bq = q_ref.shape[1]
    q0 = pl.program_id(2) * bq
    ks = pl.multiple_of(jnp.clip(q0 - ATTN_HALF, 0, seq - kb), ATTN_HALF)
    q = q_ref[0].astype(F32) * (HEAD_DIM ** -0.5)
    k = k_ref[0, pl.ds(ks, kb), :]
    v = v_ref[0, pl.ds(ks, kb), :]
    qpos = q0 + lax.broadcasted_iota(jnp.int32, (bq, kb), 0)
    kpos = ks + lax.broadcasted_iota(jnp.int32, (bq, kb), 1)
    valid = jnp.abs(qpos - kpos) <= ATTN_HALF
    lane = lax.broadcasted_iota(jnp.int32, (bq, LANES), 1)
    for hp in range(GROUP_W // LANES):
        sl = slice(hp * LANES, (hp + 1) * LANES)
        qp, kp, vp = q[:, sl], k[:, sl], v[:, sl]
        o_pair = l_pair = None
        for sub in range(2):
            in_head = (lane >= HEAD_DIM) if sub else (lane < HEAD_DIM)
            s = _dot_nt(jnp.where(in_head, qp, 0.0), kp)
            s = jnp.where(valid, s, NEG)
            mx = jnp.max(s, axis=-1, keepdims=True)
            p = jnp.exp(s - mx)
            den = jnp.sum(p, axis=-1, keepdims=True)
            o = _dot(p, vp) / den
            lse = jnp.broadcast_to(mx + jnp.log(den), (bq, LANES))
            if sub == 0:
                o_pair, l_pair = o, lse
            else:
                o_pair = jnp.where(in_head, o, o_pair)
                l_pair = jnp.where(in_head, lse, l_pair)
        o_ref[0, :, sl] = o_pair
        l_ref[0, :, sl] = l_pair


def _attn(qk, va, gi, dil):
    b, s, _ = qk.shape
    seq = s // dil
    kb = min(2 * ATTN_BQ, seq)
    qk_v = qk.reshape(b, seq, dil * 2 * ATTN_W)
    va_v = va.reshape(b, seq, dil * ATTN_W)
    nq, nv = 2 * N_GROUPS, N_GROUPS
    out = jax.ShapeDtypeStruct((b, seq, dil * GROUP_W), F32)
    o, l = pl.pallas_call(
        functools.partial(_attn_kernel, seq=seq, kb=kb),
        grid=(b, dil, seq // ATTN_BQ),
        in_specs=[pl.BlockSpec((1, ATTN_BQ, GROUP_W), lambda bi, r, m: (bi, m, r * nq + gi)),
                  pl.BlockSpec((1, seq, GROUP_W), lambda bi, r, m: (bi, 0, r * nq + N_GROUPS + gi)),
                  pl.BlockSpec((1, seq, GROUP_W), lambda bi, r, m: (bi, 0, r * nv + gi))],
        out_specs=[pl.BlockSpec((1, ATTN_BQ, GROUP_W), lambda bi, r, m: (bi, m, r))] * 2,
        out_shape=[out, out],
        compiler_params=_cparams(),
        name=f"attn_g{gi}",
    )(qk_v, qk_v, va_v)
    return o.reshape(b, s, GROUP_W), l.reshape(b, s, GROUP_W)


def _tri_inverse(a, masks_ref, base_idx, eye):
    x = -a * masks_ref[base_idx]
    t = eye + x
    p = x
    n = 2
    while n < DN_BASE:
        p = _dot(p, p)
        t = t + _dot(t, p)
        n *= 2
    n = DN_BASE
    lvl = 1
    while n < DN_CHUNK:
        t = t - _dot(_dot(t, a * masks_ref[base_idx + lvl]), t)
        n *= 2
        lvl += 1
    return t


def _dn_kernel(q_ref, k_ref, v_ref, z_ref, cwq_ref, cwk_ref, cwv_ref, gates_ref, gt_ref, ng_ref,
               o_ref, pad_ref, qn_ref, kn_ref, vn_ref, acc_ref, masks_ref, *, seq):
    h = pl.program_id(1)
    c = DN_CHUNK
    nc = seq // c
    n_lvl = 1
    while DN_BASE << (n_lvl - 1) < c:
        n_lvl += 1

    r = lax.broadcasted_iota(jnp.int32, (c, c), 0)
    cc = lax.broadcasted_iota(jnp.int32, (c, c), 1)
    as_f32 = lambda m: jnp.where(m, 1.0, 0.0).astype(F32)
    eye = as_f32(r == cc)
    per_dir = 2 + n_lvl
    base_shift = DN_BASE.bit_length() - 1
    for d, (lo, hi) in enumerate(((cc, r), (r, cc))):
        masks_ref[d * per_dir + 0] = as_f32(lo <= hi)
        masks_ref[d * per_dir + 1] = as_f32(lo < hi)
        masks_ref[d * per_dir + 2] = as_f32((lo < hi) & ((lo >> base_shift) == (hi >> base_shift)))
        for lvl in range(1, n_lvl):
            sh = base_shift + lvl - 1
            lo_blk, hi_blk = lo >> sh, hi >> sh
            masks_ref[d * per_dir + 2 + lvl] = as_f32((hi_blk == lo_blk + 1) & ((lo_blk & 1) == 0))

    def prep(x_ref, w_ref, out_ref, kind):
        pad_ref[0:8, :] = jnp.zeros((8, LANES), F32)
        pad_ref[seq + 8:seq + 16, :] = jnp.zeros((8, LANES), F32)
        pad_ref[8:seq + 8, :] = x_ref[0]
        w = w_ref[...]
        for ci in range(nc):
            r0 = ci * c
            y = (pad_ref[r0 + 7:r0 + 7 + c, :] * w[0:1, :] + pad_ref[r0 + 8:r0 + 8 + c, :] * w[1:2, :]
                 + pad_ref[r0 + 9:r0 + 9 + c, :] * w[2:3, :])
            y = y * _sigmoid(y)
            if kind != "v":
                y = y * lax.rsqrt(jnp.sum(y * y, axis=-1, keepdims=True) + EPS)
            if kind == "q":
                y = y * (DN_HEAD_DIM ** -0.5)
            out_ref[r0:r0 + c, :] = y

    prep(q_ref, cwq_ref, qn_ref, "q")
    prep(k_ref, cwk_ref, kn_ref, "k")
    prep(v_ref, cwv_ref, vn_ref, "v")
    acc_ref[...] = jnp.zeros_like(acc_ref)

    lane = lax.broadcasted_iota(jnp.int32, (c, LANES), 1)

    def chunk(ci, state, d):
        r0 = pl.multiple_of(ci * c, c)
        q = qn_ref[pl.ds(r0, c), :]
        k = kn_ref[pl.ds(r0, c), :]
        v = vn_ref[pl.ds(r0, c), :]
        gch = gates_ref[0, pl.ds(r0, c), :]

        def col(j):
            return jnp.sum(jnp.where(lane == j, gch, 0.0), axis=1, keepdims=True)

        gc = col(d * DN_HEADS + h)
        beta = col((2 + d) * DN_HEADS + h)
        grow = gt_ref[0, d * DN_HEADS + h, pl.ds(ci, 1), :]
        g_end = gc[0:1, :] if d else gc[c - 1:c, :]
        base = d * per_dir
        dec = jnp.exp(jnp.minimum(gc - grow, 0.0))
        kb = k * beta
        a = _dot_nt(kb, k) * dec * masks_ref[base + 1]
        p = _dot_nt(q, k) * dec * masks_ref[base + 0]
        t = _tri_inverse(a, masks_ref, base + 2, eye)
        e_g = jnp.exp(gc)
        u = _dot(t, v * beta)
        w = _dot(t, kb * e_g)
        v_new = u - _dot(w, state)
        o = _dot(q * e_g, state) + _dot(p, v_new)
        new_state = state * jnp.exp(g_end) + _dot_tn(k * jnp.exp(g_end - gc), v_new)
        acc_ref[pl.ds(r0, c), :] += o
        return new_state

    def body(i, carry):
        sf, sb = carry
        return chunk(i, sf, 0), chunk(nc - 1 - i, sb, 1)

    zero = jnp.zeros((DN_HEAD_DIM, DN_HEAD_DIM), F32)
    lax.fori_loop(0, nc, body, (zero, zero))

    def finish(ci, _):
        r0 = pl.multiple_of(ci * c, c)
        od = acc_ref[pl.ds(r0, c), :]
        z = z_ref[0, pl.ds(r0, c), :]
        y = _rms(od, ng_ref[...]) * (z * _sigmoid(z))
        o_ref[0, pl.ds(r0, c), :] = y.astype(o_ref.dtype)
        return 0

    lax.fori_loop(0, nc, finish, 0)


def _deltanet(qkvd, zd, gates, gt, conv_w, norm_g):
    b, s, _ = qkvd.shape
    hd = DN_HEAD_DIM
    nc = s // DN_CHUNK
    n_lvl = 1
    while DN_BASE << (n_lvl - 1) < DN_CHUNK:
        n_lvl += 1
    n_masks = 2 * (2 + n_lvl)
    col = lambda off: pl.BlockSpec((1, s, hd), lambda bi, h: (bi, 0, off + h))
    cw = lambda off: pl.BlockSpec((3, hd), lambda bi, h: (0, off + h))
    return pl.pallas_call(
        functools.partial(_dn_kernel, seq=s),
        grid=(b, DN_HEADS),
        in_specs=[col(0), col(DN_HEADS), col(2 * DN_HEADS), col(0),
                  cw(0), cw(DN_HEADS), cw(2 * DN_HEADS),
                  pl.BlockSpec((1, s, LANES), lambda bi, h: (bi, 0, 0)),
                  pl.BlockSpec((1, 2 * DN_HEADS, nc, DN_CHUNK), lambda bi, h: (bi, 0, 0, 0)),
                  pl.BlockSpec((1, hd), lambda bi, h: (0, 0))],
        out_specs=pl.BlockSpec((1, s, hd), lambda bi, h: (bi, 0, h)),
        out_shape=jax.ShapeDtypeStruct((b, s, DN_W), BF16),
        scratch_shapes=[pltpu.VMEM((s + 16, hd), F32),
                        pltpu.VMEM((s, hd), F32), pltpu.VMEM((s, hd), F32), pltpu.VMEM((s, hd), F32),
                        pltpu.VMEM((s, hd), F32),
                        pltpu.VMEM((n_masks, DN_CHUNK, DN_CHUNK), F32)],
        compiler_params=_cparams(),
        name="deltanet",
    )(qkvd, qkvd, qkvd, zd, conv_w, conv_w, conv_w, gates, gt, norm_g)


def _merge_kernel(o1_ref, o2_ref, o3_ref, l1_ref, l2_ref, l3_ref, yd_ref, h1_ref, x_ref, mod_ref,
                  wg_ref, wba_ref, wbd_ref, wo_ref, g2_ref, x1_ref, h2_ref):
    l1, l2, l3 = l1_ref[0], l2_ref[0], l3_ref[0]
    mx = jnp.maximum(l1, jnp.maximum(l2, l3))
    e1, e2, e3 = jnp.exp(l1 - mx), jnp.exp(l2 - mx), jnp.exp(l3 - mx)
    ya = (e1 * o1_ref[0] + e2 * o2_ref[0] + e3 * o3_ref[0]) / (e1 + e2 + e3)
    d = D_MODEL
    h1 = h1_ref[0]
    gate_a = _sigmoid(jnp.dot(h1, wg_ref[:, 0:d], preferred_element_type=F32))
    gate_d = _sigmoid(jnp.dot(h1, wg_ref[:, d:2 * d], preferred_element_type=F32))
    merged = gate_a * _dot(ya, wba_ref[...]) + gate_d * _dot(yd_ref[0], wbd_ref[...])
    x1 = x_ref[0] + mod_ref[0, 2:3, :] * _dot(merged, wo_ref[...])
    x1_ref[0] = x1
    h2 = _rms(x1, g2_ref[...]) * (1.0 + mod_ref[0, 4:5, :]) + mod_ref[0, 3:4, :]
    h2_ref[0] = h2.astype(h2_ref.dtype)


def _merge_out(os_, ls_, yd, h1, x, mod, wg, wba, wbd, wo, g2):
    b, s, d = x.shape
    tm = 256
    row = lambda w: pl.BlockSpec((1, tm, w), lambda bi, i: (bi, i, 0))
    full = lambda a: pl.BlockSpec(a.shape, lambda bi, i: (0,) * a.ndim)
    return pl.pallas_call(
        _merge_kernel,
        grid=(b, s // tm),
        in_specs=[row(GROUP_W)] * 6 + [row(DN_W), row(d), row(d),
                                       pl.BlockSpec((1, 6, d), lambda bi, i: (bi, 0, 0)),
                                       full(wg), full(wba), full(wbd), full(wo), full(g2)],
        out_specs=[row(d), row(d)],
        out_shape=[jax.ShapeDtypeStruct((b, s, d), F32), jax.ShapeDtypeStruct((b, s, d), BF16)],
        compiler_params=_cparams(),
        name="merge_out",
    )(*os_, *ls_, yd, h1, x, mod, wg, wba, wbd, wo, g2)


FFN_HALO = 16


def _ffn_up_kernel(hp_ref, hm_ref, hn_ref, wv_ref, wg_ref, cwv_ref, cwg_ref, bv_ref, bg_ref,
                   o_ref, upv_ref, upg_ref, *, nt):
    i = pl.program_id(1)
    tm = hm_ref.shape[1]
    lhs = jnp.concatenate([hp_ref[0], hm_ref[0], hn_ref[0]], axis=0)
    rows = lax.broadcasted_iota(jnp.int32, (tm + 2 * FFN_HALO, 1), 0)
    first_kept = jnp.where(i > 0, 0, FFN_HALO)
    end_kept = jnp.where(i < nt - 1, tm + 2 * FFN_HALO, tm + FFN_HALO)
    keep = (rows >= first_kept) & (rows < end_kept)

    def branch(w_ref, cw_ref, b_ref, up_ref):
        up_ref[...] = jnp.where(keep, jnp.dot(lhs, w_ref[...], preferred_element_type=F32), 0.0)
        cw = cw_ref[...]
        lo = FFN_HALO - 1
        return (up_ref[lo:lo + tm, :] * cw[0:1, :] + up_ref[lo + 1:lo + 1 + tm, :] * cw[1:2, :]
                + up_ref[lo + 2:lo + 2 + tm, :] * cw[2:3, :] + b_ref[...])

    val = branch(wv_ref, cwv_ref, bv_ref, upv_ref)
    gate = branch(wg_ref, cwg_ref, bg_ref, upg_ref)
    o_ref[0] = (gate * _sigmoid(gate) * val).astype(o_ref.dtype)


def _ffn_up(h2, w_up, conv_w, conv_b):
    b, s, d = h2.shape
    tm, tn = 512, 256
    nt = s // tm
    nj = D_FF // tn
    hb = tm // FFN_HALO
    return pl.pallas_call(
        functools.partial(_ffn_up_kernel, nt=nt),
        grid=(b, nt, nj),
        in_specs=[pl.BlockSpec((1, FFN_HALO, d), lambda bi, i, j: (bi, jnp.maximum(i * hb - 1, 0), 0)),
                  pl.BlockSpec((1, tm, d), lambda bi, i, j: (bi, i, 0)),
                  pl.BlockSpec((1, FFN_HALO, d),
                               lambda bi, i, j: (bi, jnp.minimum((i + 1) * hb, s // FFN_HALO - 1), 0)),
                  pl.BlockSpec((d, tn), lambda bi, i, j: (0, j)),
                  pl.BlockSpec((d, tn), lambda bi, i, j: (0, j + nj)),
                  pl.BlockSpec((3, tn), lambda bi, i, j: (0, j)),
                  pl.BlockSpec((3, tn), lambda bi, i, j: (0, j + nj)),
                  pl.BlockSpec((1, tn), lambda bi, i, j: (0, j)),
                  pl.BlockSpec((1, tn), lambda bi, i, j: (0, j + nj))],
        out_specs=pl.BlockSpec((1, tm, tn), lambda bi, i, j: (bi, i, j)),
        out_shape=jax.ShapeDtypeStruct((b, s, D_FF), BF16),
        scratch_shapes=[pltpu.VMEM((tm + 2 * FFN_HALO, tn), F32)] * 2,
        compiler_params=_cparams(),
        name="ffn_up",
    )(h2, h2, h2, w_up, w_up, conv_w, conv_w, conv_b, conv_b)


def _ffn_down_kernel(a_ref, w_ref, x_ref, mod_ref, g_ref, o_ref):
    x2 = x_ref[0] + mod_ref[0, 5:6, :] * jnp.dot(a_ref[0], w_ref[...], preferred_element_type=F32)
    o_ref[0] = _rms(x2, g_ref[...])


def _ffn_down(act, w_down, x1, mod, g):
    b, s, d = x1.shape
    tm = 512
    return pl.pallas_call(
        _ffn_down_kernel,
        grid=(b, s // tm),
        in_specs=[pl.BlockSpec((1, tm, D_FF), lambda bi, i: (bi, i, 0)),
                  pl.BlockSpec((D_FF, d), lambda bi, i: (0, 0)),
                  pl.BlockSpec((1, tm, d), lambda bi, i: (bi, i, 0)),
                  pl.BlockSpec((1, 6, d), lambda bi, i: (bi, 0, 0)),
                  pl.BlockSpec((1, d), lambda bi, i: (0, 0))],
        out_specs=pl.BlockSpec((1, tm, d), lambda bi, i: (bi, i, 0)),
        out_shape=jax.ShapeDtypeStruct((b, s, d), F32),
        compiler_params=_cparams(),
        name="ffn_down",
    )(act, w_down, x1, mod, g)


def _trunk(x, mod, p, rope):
    b, s, _ = x.shape
    h1 = _norm_mod(x, mod, p["norm1_g"])
    qk = _mm(h1, p["w_qk"], BF16, rope=rope)
    va = _mm(h1, p["w_va"], BF16)
    qkvd = _mm(h1, p["w_qkvd"], F32)
    zd = _mm(h1, p["w_zd"], F32)
    gates = _gates(h1, p["w_ab"], p["a_row"], p["dt_row"])
    gt = jnp.transpose(gates[:, :, :2 * DN_HEADS], (0, 2, 1)).reshape(
        b, 2 * DN_HEADS, s // DN_CHUNK, DN_CHUNK)
    os_, ls_ = [], []
    for gi, (_, dil) in enumerate(ATTN_GROUPS):
        o, l = _attn(qk, va, gi, dil)
        os_.append(o)
        ls_.append(l)
    yd = _deltanet(qkvd, zd, gates, gt, p["conv_qkv_w"], p["dn_norm_g"])
    x1, h2 = _merge_out(os_, ls_, yd, h1, x, mod, p["w_gate"], p["w_br_attn"], p["w_br_dn"],
                        p["w_out"], p["norm2_g"])
    act = _ffn_up(h2, p["w_up"], p["ffn_conv_w"], p["ffn_conv_b"])
    return _ffn_down(act, p["w_down"], x1, mod, p["norm_f_g"])


def kernel(x_prompt, x_sample, c_prompt, c_sample, w_ada, b_ada, norm1_g, w_in, conv_qkv_w, a_log, dt_bias, dn_norm_g, w_br_attn, w_br_dn, w_out, norm2_g, w_up, ffn_conv_w, ffn_conv_b, w_down, norm_f_g):
    d = D_MODEL
    assert w_ada.shape[0] == 1, "single layer"
    w = w_in[0]
    o_qd = 3 * ATTN_W
    o_zd = o_qd + 3 * DN_W
    o_ab = o_zd + DN_W
    o_gate = o_ab + 4 * DN_HEADS
    pad16 = lambda v: jnp.pad(v.reshape(1, 2 * DN_HEADS).astype(F32), ((0, 0), (0, LANES - 2 * DN_HEADS)))
    p = {
        "norm1_g": norm1_g[0].reshape(1, d),
        "w_qk": w[:, :2 * ATTN_W].astype(BF16),
        "w_va": w[:, 2 * ATTN_W:o_qd].astype(BF16),
        "w_qkvd": w[:, o_qd:o_zd].astype(BF16),
        "w_zd": w[:, o_zd:o_ab].astype(BF16),
        "w_ab": jnp.pad(w[:, o_ab:o_gate], ((0, 0), (0, LANES - 4 * DN_HEADS))).astype(BF16),
        "w_gate": w[:, o_gate:].astype(BF16),
        "a_row": pad16(a_log[0]),
        "dt_row": pad16(dt_bias[0]),
        "conv_qkv_w": conv_qkv_w[0],
        "dn_norm_g": dn_norm_g[0].reshape(1, DN_HEAD_DIM),
        "w_br_attn": w_br_attn[0].astype(BF16),
        "w_br_dn": w_br_dn[0].astype(BF16),
        "w_out": w_out[0].astype(BF16),
        "norm2_g": norm2_g[0].reshape(1, d),
        "w_up": w_up[0].astype(BF16),
        "ffn_conv_w": ffn_conv_w[0],
        "ffn_conv_b": ffn_conv_b[0].reshape(1, 2 * D_FF),
        "w_down": w_down[0].astype(BF16),
        "norm_f_g": norm_f_g.reshape(1, d),
    }
    nb = x_prompt.shape[0]
    mod = _ada(jnp.concatenate([c_prompt, c_sample], axis=0), w_ada[0], b_ada[0].reshape(1, 6 * d))
    mod = mod.reshape(-1, 6, d)
    rope = _rope_tables(max(x_prompt.shape[1], x_sample.shape[1]))
    y_prompt = _trunk(x_prompt, mod[:nb], p, rope)
    y_sample = _trunk(x_sample, mod[nb:], p, rope)
    return (y_prompt, y_sample)
```

```python
import functools

import jax
import jax.numpy as jnp
from jax import lax
from jax.experimental import pallas as pl
from jax.experimental.pallas import tpu as pltpu

F32 = jnp.float32
BF16 = jnp.bfloat16
HIGHEST = lax.Precision.HIGHEST

D_MODEL = 1024
ATTN_GROUPS = ((128, 1), (512, 4), (2048, 16))
N_GROUPS = 3
HEAD_DIM = 64
GROUP_W = 512
ATTN_W = N_GROUPS * GROUP_W
ATTN_HALF = 64
ATTN_BQ = 128
ROPE_THETA = 10000.0
NEG = -1e30
DN_HEADS = 8
DN_HEAD_DIM = 128
DN_W = DN_HEADS * DN_HEAD_DIM
DN_CHUNK = 256
DN_BASE = 16
DN_GROUP = 2
D_FF = 2816
EPS = 1e-6
LANES = 128
VMEM_LIMIT = 56 * 1024 * 1024


def _cparams():
    return pltpu.CompilerParams(vmem_limit_bytes=VMEM_LIMIT)


def _sigmoid(x):
    return 1.0 / (1.0 + jnp.exp(-x))


def _dot(a, b):
    return jnp.dot(a.astype(BF16), b.astype(BF16), preferred_element_type=F32)


def _dot_nt(a, b):
    return lax.dot_general(a.astype(BF16), b.astype(BF16), (((1,), (1,)), ((), ())),
                           preferred_element_type=F32)


def _dot_tn(a, b):
    return lax.dot_general(a.astype(BF16), b.astype(BF16), (((0,), (0,)), ((), ())),
                           preferred_element_type=F32)


def _rms(x, g):
    return x * lax.rsqrt(jnp.mean(x * x, axis=-1, keepdims=True) + EPS) * g


def _ada_kernel(c_ref, w_ref, b_ref, o_ref):
    c = c_ref[...]
    s = c * _sigmoid(c)
    o_ref[...] = jnp.dot(s, w_ref[...], preferred_element_type=F32, precision=HIGHEST) + b_ref[...]


def _ada(c, w, b):
    bt, d = c.shape
    n = w.shape[1]
    tn = 1024
    return pl.pallas_call(
        _ada_kernel,
        grid=(n // tn,),
        in_specs=[pl.BlockSpec((bt, d), lambda j: (0, 0)),
                  pl.BlockSpec((d, tn), lambda j: (0, j)),
                  pl.BlockSpec((1, tn), lambda j: (0, j))],
        out_specs=pl.BlockSpec((bt, tn), lambda j: (0, j)),
        out_shape=jax.ShapeDtypeStruct((bt, n), F32),
        name="ada",
    )(c, w, b)


def _rope_table_kernel(inv_ref, cos_ref, sin_ref):
    tm = cos_ref.shape[0]
    pos = (pl.program_id(0) * tm + lax.broadcasted_iota(jnp.int32, (tm, LANES), 0)).astype(F32)
    ang = pos * inv_ref[...]
    lane = lax.broadcasted_iota(jnp.int32, (tm, LANES), 1)
    sign = jnp.where((lane & (HEAD_DIM - 1)) < HEAD_DIM // 2, -1.0, 1.0)
    cos_ref[...] = jnp.cos(ang)
    sin_ref[...] = jnp.sin(ang) * sign


def _rope_tables(s):
    half = HEAD_DIM // 2
    inv = ROPE_THETA ** (-(jnp.arange(LANES) % half).astype(F32) / half)
    tm = 512
    return pl.pallas_call(
        _rope_table_kernel,
        grid=(s // tm,),
        in_specs=[pl.BlockSpec((1, LANES), lambda i: (0, 0))],
        out_specs=[pl.BlockSpec((tm, LANES), lambda i: (i, 0))] * 2,
        out_shape=[jax.ShapeDtypeStruct((s, LANES), F32)] * 2,
        name="rope_tables",
    )(inv.reshape(1, LANES))


def _norm_mod_kernel(x_ref, mod_ref, g_ref, o_ref):
    h = _rms(x_ref[0], g_ref[...]) * (1.0 + mod_ref[0, 1:2, :]) + mod_ref[0, 0:1, :]
    o_ref[0] = h.astype(o_ref.dtype)


def _norm_mod(x, mod, g):
    b, s, d = x.shape
    tm = 512
    return pl.pallas_call(
        _norm_mod_kernel,
        grid=(b, s // tm),
        in_specs=[pl.BlockSpec((1, tm, d), lambda bi, i: (bi, i, 0)),
                  pl.BlockSpec((1, 6, d), lambda bi, i: (bi, 0, 0)),
                  pl.BlockSpec((1, d), lambda bi, i: (0, 0))],
        out_specs=pl.BlockSpec((1, tm, d), lambda bi, i: (bi, i, 0)),
        out_shape=jax.ShapeDtypeStruct((b, s, d), BF16),
        name="norm_mod",
    )(x, mod, g)


def _mm_kernel(h_ref, w_ref, o_ref):
    o_ref[0] = jnp.dot(h_ref[0], w_ref[...], preferred_element_type=F32).astype(o_ref.dtype)


def _mm(h, w, out_dtype):
    b, s, k = h.shape
    n = w.shape[1]
    tm, tn = 1024, 512
    return pl.pallas_call(
        _mm_kernel,
        grid=(b, s // tm, n // tn),
        in_specs=[pl.BlockSpec((1, tm, k), lambda bi, i, j: (bi, i, 0)),
                  pl.BlockSpec((k, tn), lambda bi, i, j: (0, j))],
        out_specs=pl.BlockSpec((1, tm, tn), lambda bi, i, j: (bi, i, j)),
        out_shape=jax.ShapeDtypeStruct((b, s, n), out_dtype),
        compiler_params=_cparams(),
        name="proj",
    )(h, w)


def _proj_attn_kernel(h_ref, w_ref, cos_ref, sin_ref, o_ref, *scratch, dil):
    kind = pl.program_id(2)
    acc = jnp.dot(h_ref[0], w_ref[...], preferred_element_type=F32)
    tm = acc.shape[0]
    n_slab = acc.shape[1] // LANES

    def emit(c, val):
        if dil == 1:
            o_ref[0, 0, 0, :, c * LANES:(c + 1) * LANES] = val.astype(o_ref.dtype)
        else:
            scratch[0][c] = val

    @pl.when(kind < 2)
    def _():
        cos = cos_ref[...]
        sin = sin_ref[...]
        lane = lax.broadcasted_iota(jnp.int32, cos.shape, 1)
        first_half = (lane & (HEAD_DIM - 1)) < HEAD_DIM // 2
        for c in range(n_slab):
            t = acc[:, c * LANES:(c + 1) * LANES]
            partner = jnp.where(first_half, pltpu.roll(t, LANES - HEAD_DIM // 2, 1),
                                pltpu.roll(t, HEAD_DIM // 2, 1))
            emit(c, t * cos + partner * sin)

    @pl.when(kind == 2)
    def _():
        for c in range(n_slab):
            emit(c, acc[:, c * LANES:(c + 1) * LANES])

    if dil > 1:
        n = tm // dil
        for r in range(dil):
            for c in range(n_slab):
                o_ref[0, 0, r, :, c * LANES:(c + 1) * LANES] = (
                    scratch[0][c, pl.ds(r, n, stride=dil), :].astype(o_ref.dtype))


def _proj_attn(h, w, rope, dil, gi):
    b, s, k = h.shape
    tm, tn = 1024, GROUP_W
    scratch = [pltpu.VMEM((tn // LANES, tm, LANES), F32)] if dil > 1 else []
    return pl.pallas_call(
        functools.partial(_proj_attn_kernel, dil=dil),
        grid=(b, s // tm, 3),
        in_specs=[pl.BlockSpec((1, tm, k), lambda bi, i, j: (bi, i, 0)),
                  pl.BlockSpec((k, tn), lambda bi, i, j: (0, j)),
                  pl.BlockSpec((tm, LANES), lambda bi, i, j: (i, 0)),
                  pl.BlockSpec((tm, LANES), lambda bi, i, j: (i, 0))],
        out_specs=pl.BlockSpec((1, 1, dil, tm // dil, tn), lambda bi, i, j: (bi, j, 0, i, 0)),
        out_shape=jax.ShapeDtypeStruct((b, 3, dil, s // dil, tn), BF16),
        scratch_shapes=scratch,
        compiler_params=_cparams(),
        name=f"proj_attn_g{gi}",
    )(h, w, *rope)


def _gates_kernel(h_ref, w_ref, a_ref, dt_ref, o_ref, gt_ref):
    acc = jnp.dot(h_ref[0], w_ref[...], preferred_element_type=F32)
    c = acc.shape[0]
    x = acc + dt_ref[...]
    softplus = jnp.maximum(x, 0.0) + jnp.log1p(jnp.exp(-jnp.abs(x)))
    g = -jnp.exp(a_ref[...]) * softplus
    beta = _sigmoid(acc)
    r = lax.broadcasted_iota(jnp.int32, (c, c), 0)
    cc = lax.broadcasted_iota(jnp.int32, (c, c), 1)
    pre = jnp.dot(jnp.where(cc <= r, 1.0, 0.0).astype(F32), g, preferred_element_type=F32,
                  precision=HIGHEST)
    suf = jnp.dot(jnp.where(cc >= r, 1.0, 0.0).astype(F32), g, preferred_element_type=F32,
                  precision=HIGHEST)
    lane = lax.broadcasted_iota(jnp.int32, acc.shape, 1)
    out = jnp.where(lane < DN_HEADS, pre,
                    jnp.where(lane < 2 * DN_HEADS, suf,
                              jnp.where(lane < 4 * DN_HEADS, beta, 0.0)))
    o_ref[0] = out
    gt_ref[0, 0] = out.T[0:2 * DN_HEADS, :]


def _gates(h, w, a_row, dt_row):
    b, s, k = h.shape
    tm = DN_CHUNK
    return pl.pallas_call(
        _gates_kernel,
        grid=(b, s // tm),
        in_specs=[pl.BlockSpec((1, tm, k), lambda bi, i: (bi, i, 0)),
                  pl.BlockSpec((k, LANES), lambda bi, i: (0, 0)),
                  pl.BlockSpec((1, LANES), lambda bi, i: (0, 0)),
                  pl.BlockSpec((1, LANES), lambda bi, i: (0, 0))],
        out_specs=[pl.BlockSpec((1, tm, LANES), lambda bi, i: (bi, i, 0)),
                   pl.BlockSpec((1, 1, 2 * DN_HEADS, tm), lambda bi, i: (bi, i, 0, 0))],
        out_shape=[jax.ShapeDtypeStruct((b, s, LANES), F32),
                   jax.ShapeDtypeStruct((b, s // tm, 2 * DN_HEADS, tm), F32)],
        name="dn_gates",
    )(h, w, a_row, dt_row)


def _attn_kernel(q_ref, k_ref, v_ref, o_ref, l_ref, *scratch, seq, kb, dil):
    bq = ATTN_BQ
    n_slab = GROUP_W // LANES
    q0 = pl.program_id(1) * bq
    ks = pl.multiple_of(jnp.clip(q0 - ATTN_HALF, 0, seq - kb), ATTN_HALF)
    qpos = q0 + lax.broadcasted_iota(jnp.int32, (bq, kb), 0)
    kpos = ks + lax.broadcasted_iota(jnp.int32, (bq, kb), 1)
    valid = jnp.abs(qpos - kpos) <= ATTN_HALF
    lane = lax.broadcasted_iota(jnp.int32, (bq, LANES), 1)

    def residue(r):
        q = q_ref[0, 0, r].astype(F32) * (HEAD_DIM ** -0.5)
        k = k_ref[0, 0, r, pl.ds(ks, kb), :]
        v = v_ref[0, 0, r, pl.ds(ks, kb), :]
        for hp in range(n_slab):
            sl = slice(hp * LANES, (hp + 1) * LANES)
            qp, kp, vp = q[:, sl], k[:, sl], v[:, sl]
            o_pair = l_pair = None
            for sub in range(2):
                in_head = (lane >= HEAD_DIM) if sub else (lane < HEAD_DIM)
                s = _dot_nt(jnp.where(in_head, qp, 0.0), kp)
                s = jnp.where(valid, s, NEG)
                mx = jnp.max(s, axis=-1, keepdims=True)
                p = jnp.exp(s - mx)
                den = jnp.sum(p, axis=-1, keepdims=True)
                o = _dot(p, vp) / den
                lse = jnp.broadcast_to(mx + jnp.log(den), (bq, LANES))
                if sub == 0:
                    o_pair, l_pair = o, lse
                else:
                    o_pair = jnp.where(in_head, o, o_pair)
                    l_pair = jnp.where(in_head, lse, l_pair)
            if dil == 1:
                o_ref[0, :, sl] = o_pair.astype(o_ref.dtype)
                l_ref[0, :, sl] = l_pair
            else:
                scratch[0][hp, pl.ds(r, bq, stride=dil), :] = o_pair
                scratch[1][hp, pl.ds(r, bq, stride=dil), :] = l_pair

    if dil == 1:
        residue(0)
    else:
        def body(r, carry):
            residue(r)
            return carry
        lax.fori_loop(0, dil, body, 0)
        for hp in range(n_slab):
            sl = slice(hp * LANES, (hp + 1) * LANES)
            o_ref[0, :, sl] = scratch[0][hp].astype(o_ref.dtype)
            l_ref[0, :, sl] = scratch[1][hp]


def _attn(qkv, gi, dil):
    b, _, _, seq, _ = qkv.shape
    s = seq * dil
    kb = min(2 * ATTN_BQ, seq)
    rows = ATTN_BQ * dil
    scratch = [pltpu.VMEM((GROUP_W // LANES, rows, LANES), F32)] * 2 if dil > 1 else []
    kv = lambda kind: pl.BlockSpec((1, 1, dil, seq, GROUP_W), lambda bi, m: (bi, kind, 0, 0, 0))
    return pl.pallas_call(
        functools.partial(_attn_kernel, seq=seq, kb=kb, dil=dil),
        grid=(b, seq // ATTN_BQ),
        in_specs=[pl.BlockSpec((1, 1, dil, ATTN_BQ, GROUP_W), lambda bi, m: (bi, 0, 0, m, 0)),
                  kv(1), kv(2)],
        out_specs=[pl.BlockSpec((1, rows, GROUP_W), lambda bi, m: (bi, m, 0))] * 2,
        out_shape=[jax.ShapeDtypeStruct((b, s, GROUP_W), BF16),
                   jax.ShapeDtypeStruct((b, s, GROUP_W), F32)],
        scratch_shapes=scratch,
        compiler_params=_cparams(),
        name=f"attn_g{gi}",
    )(qkv, qkv, qkv)


def _tri_inverse(a_list, masks_ref, bases, eye):
    ps = [-a * masks_ref[b] for a, b in zip(a_list, bases)]
    ts = [eye + x for x in ps]
    n = 2
    while n < DN_BASE:
        ps = [_dot(p, p) for p in ps]
        ts = [t + _dot(t, p) for t, p in zip(ts, ps)]
        n *= 2
    n = DN_BASE
    lvl = 1
    while n < DN_CHUNK:
        left = [_dot(t, a * masks_ref[b + lvl]) for t, a, b in zip(ts, a_list, bases)]
        ts = [t - _dot(l, t) for t, l in zip(ts, left)]
        n *= 2
        lvl += 1
    return ts


def _dn_kernel(q_ref, k_ref, v_ref, z_ref, cwq_ref, cwk_ref, cwv_ref, gates_ref, gt_ref, ng_ref,
               o_ref, pad_ref, qn_ref, kn_ref, vn_ref, masks_ref, u_ref, wq_ref, kd_ref, p_ref,
               gl_ref, *, seq):
    h = pl.program_id(1)
    c = DN_CHUNK
    nc = seq // c
    n_lvl = 1
    while DN_BASE << (n_lvl - 1) < c:
        n_lvl += 1

    r = lax.broadcasted_iota(jnp.int32, (c, c), 0)
    cc = lax.broadcasted_iota(jnp.int32, (c, c), 1)
    as_f32 = lambda m: jnp.where(m, 1.0, 0.0).astype(F32)
    eye = as_f32(r == cc)
    per_dir = 2 + n_lvl
    base_shift = DN_BASE.bit_length() - 1
    for d, (lo, hi) in enumerate(((cc, r), (r, cc))):
        masks_ref[d * per_dir + 0] = as_f32(lo <= hi)
        masks_ref[d * per_dir + 1] = as_f32(lo < hi)
        masks_ref[d * per_dir + 2] = as_f32((lo < hi) & ((lo >> base_shift) == (hi >> base_shift)))
        for lvl in range(1, n_lvl):
            sh = base_shift + lvl - 1
            lo_blk, hi_blk = lo >> sh, hi >> sh
            masks_ref[d * per_dir + 2 + lvl] = as_f32((hi_blk == lo_blk + 1) & ((lo_blk & 1) == 0))

    def prep(x_ref, w_ref, out_ref, kind):
        pad_ref[0:8, :] = jnp.zeros((8, LANES), F32)
        pad_ref[seq + 8:seq + 16, :] = jnp.zeros((8, LANES), F32)
        pad_ref[8:seq + 8, :] = x_ref[0]
        w = w_ref[...]
        for ci in range(nc):
            r0 = ci * c
            y = (pad_ref[r0 + 7:r0 + 7 + c, :] * w[0:1, :] + pad_ref[r0 + 8:r0 + 8 + c, :] * w[1:2, :]
                 + pad_ref[r0 + 9:r0 + 9 + c, :] * w[2:3, :])
            y = y * _sigmoid(y)
            if kind != "v":
                y = y * lax.rsqrt(jnp.sum(y * y, axis=-1, keepdims=True) + EPS)
            if kind == "q":
                y = y * (DN_HEAD_DIM ** -0.5)
            out_ref[r0:r0 + c, :] = y

    prep(q_ref, cwq_ref, qn_ref, "q")
    prep(k_ref, cwk_ref, kn_ref, "k")
    prep(v_ref, cwv_ref, vn_ref, "v")
    acc_ref = pad_ref
    acc_ref[0:seq, :] = jnp.zeros((seq, LANES), F32)

    lane = lax.broadcasted_iota(jnp.int32, (c, LANES), 1)

    def phase_a(i, carry):
        sys_, a_list, bases = [], [], []
        for ci in [DN_GROUP * i + j for j in range(DN_GROUP)]:
            r0 = pl.multiple_of(ci * c, c)
            q = qn_ref[pl.ds(r0, c), :]
            k = kn_ref[pl.ds(r0, c), :]
            v = vn_ref[pl.ds(r0, c), :]
            gch = gates_ref[0, pl.ds(r0, c), :]
            kk = _dot_nt(k, k)
            qk = _dot_nt(q, k)
            for d in range(2):
                col = lambda j: jnp.sum(jnp.where(lane == j, gch, 0.0), axis=1, keepdims=True)
                gc = col(d * DN_HEADS + h)
                beta = col((2 + d) * DN_HEADS + h)
                grow = gt_ref[0, ci, pl.ds(d * DN_HEADS + h, 1), :]
                g_end = gc[0:1, :] if d else gc[c - 1:c, :]
                dec = jnp.exp(jnp.minimum(gc - grow, 0.0))
                e_g = jnp.exp(gc)
                a_list.append(kk * (beta * dec) * masks_ref[d * per_dir + 1])
                bases.append(d * per_dir + 2)
                sys_.append(dict(
                    ci=ci, r0=r0, d=d,
                    rhs=jnp.concatenate([v * beta, k * (beta * e_g)], axis=1),
                    qg=(q * e_g).astype(BF16),
                    kd=(k * jnp.exp(g_end - gc)).astype(BF16),
                    p=(qk * dec * masks_ref[d * per_dir]).astype(BF16),
                    gl=jnp.broadcast_to(jnp.exp(g_end), (1, LANES))))
        t_list = _tri_inverse(a_list, masks_ref, bases, eye)
        uw_list = [_dot(t, s["rhs"]) for t, s in zip(t_list, sys_)]
        for s, uw in zip(sys_, uw_list):
            d, ci, r0 = s["d"], s["ci"], s["r0"]
            u_ref[d, pl.ds(r0, c), :] = uw[:, 0:LANES]
            wq_ref[d, ci, 0:c, :] = uw[:, LANES:2 * LANES].astype(BF16)
            wq_ref[d, ci, c:2 * c, :] = s["qg"]
            kd_ref[d, pl.ds(r0, c), :] = s["kd"]
            p_ref[d, ci] = s["p"]
            gl_ref[d, pl.ds(ci, 1), :] = s["gl"]
        return carry

    lax.fori_loop(0, nc // DN_GROUP, phase_a, 0)

    def phase_b(i, carry):
        cis = (i, nc - 1 - i)
        r0s = [pl.multiple_of(ci * c, c) for ci in cis]
        ws_qs = [jnp.dot(wq_ref[d, cis[d]], carry[d].astype(BF16), preferred_element_type=F32)
                 for d in range(2)]
        v_new = [(u_ref[d, pl.ds(r0s[d], c), :] - ws_qs[d][0:c]).astype(BF16) for d in range(2)]
        outs = [ws_qs[d][c:2 * c] + jnp.dot(p_ref[d, cis[d]], v_new[d], preferred_element_type=F32)
                for d in range(2)]
        new = [carry[d] * gl_ref[d, pl.ds(cis[d], 1), :]
               + _dot_tn(kd_ref[d, pl.ds(r0s[d], c), :], v_new[d]) for d in range(2)]
        for d in range(2):
            acc_ref[pl.ds(r0s[d], c), :] += outs[d]
        return tuple(new)

    zero = jnp.zeros((DN_HEAD_DIM, DN_HEAD_DIM), F32)
    lax.fori_loop(0, nc, phase_b, (zero, zero))

    def finish(ci, _):
        r0 = pl.multiple_of(ci * c, c)
        od = acc_ref[pl.ds(r0, c), :]
        z = z_ref[0, pl.ds(r0, c), :]
        y = _rms(od, ng_ref[...]) * (z * _sigmoid(z))
        o_ref[0, pl.ds(r0, c), :] = y.astype(o_ref.dtype)
        return 0

    lax.fori_loop(0, nc, finish, 0)


def _deltanet(qkvd, zd, gates, gt, conv_w, norm_g):
    b, s, _ = qkvd.shape
    hd = DN_HEAD_DIM
    nc = s // DN_CHUNK
    n_lvl = 1
    while DN_BASE << (n_lvl - 1) < DN_CHUNK:
        n_lvl += 1
    n_masks = 2 * (2 + n_lvl)
    col = lambda off: pl.BlockSpec((1, s, hd), lambda bi, h: (bi, 0, off + h))
    cw = lambda off: pl.BlockSpec((3, hd), lambda bi, h: (0, off + h))
    return pl.pallas_call(
        functools.partial(_dn_kernel, seq=s),
        grid=(b, DN_HEADS),
        in_specs=[col(0), col(DN_HEADS), col(2 * DN_HEADS), col(0),
                  cw(0), cw(DN_HEADS), cw(2 * DN_HEADS),
                  pl.BlockSpec((1, s, LANES), lambda bi, h: (bi, 0, 0)),
                  pl.BlockSpec((1, nc, 2 * DN_HEADS, DN_CHUNK), lambda bi, h: (bi, 0, 0, 0)),
                  pl.BlockSpec((1, hd), lambda bi, h: (0, 0))],
        out_specs=pl.BlockSpec((1, s, hd), lambda bi, h: (bi, 0, h)),
        out_shape=jax.ShapeDtypeStruct((b, s, DN_W), BF16),
        scratch_shapes=[pltpu.VMEM((s + 16, hd), F32),
                        pltpu.VMEM((s, hd), F32), pltpu.VMEM((s, hd), F32), pltpu.VMEM((s, hd), F32),
                        pltpu.VMEM((n_masks, DN_CHUNK, DN_CHUNK), F32),
                        pltpu.VMEM((2, s, hd), F32),
                        pltpu.VMEM((2, nc, 2 * DN_CHUNK, hd), BF16),
                        pltpu.VMEM((2, s, hd), BF16),
                        pltpu.VMEM((2, nc, DN_CHUNK, DN_CHUNK), BF16),
                        pltpu.VMEM((2, max(nc, 8), hd), F32)],
        compiler_params=_cparams(),
        name="deltanet",
    )(qkvd, qkvd, qkvd, zd, conv_w, conv_w, conv_w, gates, gt, norm_g)


def _merge_kernel(o1_ref, o2_ref, o3_ref, l1_ref, l2_ref, l3_ref, yd_ref, h1_ref, x_ref, mod_ref,
                  wg_ref, wba_ref, wbd_ref, wo_ref, g2_ref, x1_ref, h2_ref):
    l1, l2, l3 = l1_ref[0], l2_ref[0], l3_ref[0]
    mx = jnp.maximum(l1, jnp.maximum(l2, l3))
    e1, e2, e3 = jnp.exp(l1 - mx), jnp.exp(l2 - mx), jnp.exp(l3 - mx)
    ya = (e1 * o1_ref[0] + e2 * o2_ref[0] + e3 * o3_ref[0]) / (e1 + e2 + e3)
    d = D_MODEL
    h1 = h1_ref[0]
    gate_a = _sigmoid(jnp.dot(h1, wg_ref[:, 0:d], preferred_element_type=F32))
    gate_d = _sigmoid(jnp.dot(h1, wg_ref[:, d:2 * d], preferred_element_type=F32))
    merged = gate_a * _dot(ya, wba_ref[...]) + gate_d * _dot(yd_ref[0], wbd_ref[...])
    x1 = x_ref[0] + mod_ref[0, 2:3, :] * _dot(merged, wo_ref[...])
    x1_ref[0] = x1
    h2 = _rms(x1, g2_ref[...]) * (1.0 + mod_ref[0, 4:5, :]) + mod_ref[0, 3:4, :]
    h2_ref[0] = h2.astype(h2_ref.dtype)


def _merge_out(os_, ls_, yd, h1, x, mod, wg, wba, wbd, wo, g2):
    b, s, d = x.shape
    tm = 256
    row = lambda w: pl.BlockSpec((1, tm, w), lambda bi, i: (bi, i, 0))
    full = lambda a: pl.BlockSpec(a.shape, lambda bi, i: (0,) * a.ndim)
    return pl.pallas_call(
        _merge_kernel,
        grid=(b, s // tm),
        in_specs=[row(GROUP_W)] * 6 + [row(DN_W), row(d), row(d),
                                       pl.BlockSpec((1, 6, d), lambda bi, i: (bi, 0, 0)),
                                       full(wg), full(wba), full(wbd), full(wo), full(g2)],
        out_specs=[row(d), row(d)],
        out_shape=[jax.ShapeDtypeStruct((b, s, d), F32), jax.ShapeDtypeStruct((b, s, d), BF16)],
        compiler_params=_cparams(),
        name="merge_out",
    )(*os_, *ls_, yd, h1, x, mod, wg, wba, wbd, wo, g2)


FFN_HALO = 16


def _ffn_up_kernel(hp_ref, hm_ref, hn_ref, wv_ref, wg_ref, cwv_ref, cwg_ref, bv_ref, bg_ref,
                   o_ref, upv_ref, upg_ref, *, nt):
    i = pl.program_id(1)
    tm = hm_ref.shape[1]
    lhs = jnp.concatenate([hp_ref[0], hm_ref[0], hn_ref[0]], axis=0)
    rows = lax.broadcasted_iota(jnp.int32, (tm + 2 * FFN_HALO, 1), 0)
    first_kept = jnp.where(i > 0, 0, FFN_HALO)
    end_kept = jnp.where(i < nt - 1, tm + 2 * FFN_HALO, tm + FFN_HALO)
    keep = (rows >= first_kept) & (rows < end_kept)

    def branch(w_ref, cw_ref, b_ref, up_ref):
        up_ref[...] = jnp.where(keep, jnp.dot(lhs, w_ref[...], preferred_element_type=F32), 0.0)
        cw = cw_ref[...]
        lo = FFN_HALO - 1
        return (up_ref[lo:lo + tm, :] * cw[0:1, :] + up_ref[lo + 1:lo + 1 + tm, :] * cw[1:2, :]
                + up_ref[lo + 2:lo + 2 + tm, :] * cw[2:3, :] + b_ref[...])

    val = branch(wv_ref, cwv_ref, bv_ref, upv_ref)
    gate = branch(wg_ref, cwg_ref, bg_ref, upg_ref)
    o_ref[0] = (gate * _sigmoid(gate) * val).astype(o_ref.dtype)


def _ffn_up(h2, w_up, conv_w, conv_b):
    b, s, d = h2.shape
    tm, tn = 512, 256
    nt = s // tm
    nj = D_FF // tn
    hb = tm // FFN_HALO
    return pl.pallas_call(
        functools.partial(_ffn_up_kernel, nt=nt),
        grid=(b, nt, nj),
        in_specs=[pl.BlockSpec((1, FFN_HALO, d), lambda bi, i, j: (bi, jnp.maximum(i * hb - 1, 0), 0)),
                  pl.BlockSpec((1, tm, d), lambda bi, i, j: (bi, i, 0)),
                  pl.BlockSpec((1, FFN_HALO, d),
                               lambda bi, i, j: (bi, jnp.minimum((i + 1) * hb, s // FFN_HALO - 1), 0)),
                  pl.BlockSpec((d, tn), lambda bi, i, j: (0, j)),
                  pl.BlockSpec((d, tn), lambda bi, i, j: (0, j + nj)),
                  pl.BlockSpec((3, tn), lambda bi, i, j: (0, j)),
                  pl.BlockSpec((3, tn), lambda bi, i, j: (0, j + nj)),
                  pl.BlockSpec((1, tn), lambda bi, i, j: (0, j)),
                  pl.BlockSpec((1, tn), lambda bi, i, j: (0, j + nj))],
        out_specs=pl.BlockSpec((1, tm, tn), lambda bi, i, j: (bi, i, j)),
        out_shape=jax.ShapeDtypeStruct((b, s, D_FF), BF16),
        scratch_shapes=[pltpu.VMEM((tm + 2 * FFN_HALO, tn), F32)] * 2,
        compiler_params=_cparams(),
        name="ffn_up",
    )(h2, h2, h2, w_up, w_up, conv_w, conv_w, conv_b, conv_b)


def _ffn_down_kernel(a_ref, w_ref, x_ref, mod_ref, g_ref, o_ref):
    x2 = x_ref[0] + mod_ref[0, 5:6, :] * jnp.dot(a_ref[0], w_ref[...], preferred_element_type=F32)
    o_ref[0] = _rms(x2, g_ref[...])


def _ffn_down(act, w_down, x1, mod, g):
    b, s, d = x1.shape
    tm = 512
    return pl.pallas_call(
        _ffn_down_kernel,
        grid=(b, s // tm),
        in_specs=[pl.BlockSpec((1, tm, D_FF), lambda bi, i: (bi, i, 0)),
                  pl.BlockSpec((D_FF, d), lambda bi, i: (0, 0)),
                  pl.BlockSpec((1, tm, d), lambda bi, i: (bi, i, 0)),
                  pl.BlockSpec((1, 6, d), lambda bi, i: (bi, 0, 0)),
                  pl.BlockSpec((1, d), lambda bi, i: (0, 0))],
        out_specs=pl.BlockSpec((1, tm, d), lambda bi, i: (bi, i, 0)),
        out_shape=jax.ShapeDtypeStruct((b, s, d), F32),
        compiler_params=_cparams(),
        name="ffn_down",
    )(act, w_down, x1, mod, g)


def _trunk(x, mod, p, rope):
    b, s, _ = x.shape
    h1 = _norm_mod(x, mod, p["norm1_g"])
    qkvd = _mm(h1, p["w_qkvd"], F32)
    zd = _mm(h1, p["w_zd"], F32)
    gates, gt = _gates(h1, p["w_ab"], p["a_row"], p["dt_row"])
    os_, ls_ = [], []
    for gi, (_, dil) in enumerate(ATTN_GROUPS):
        o, l = _attn(_proj_attn(h1, p["w_attn"][gi], rope, dil, gi), gi, dil)
        os_.append(o)
        ls_.append(l)
    yd = _deltanet(qkvd, zd, gates, gt, p["conv_qkv_w"], p["dn_norm_g"])
    x1, h2 = _merge_out(os_, ls_, yd, h1, x, mod, p["w_gate"], p["w_br_attn"], p["w_br_dn"],
                        p["w_out"], p["norm2_g"])
    act = _ffn_up(h2, p["w_up"], p["ffn_conv_w"], p["ffn_conv_b"])
    return _ffn_down(act, p["w_down"], x1, mod, p["norm_f_g"])


def kernel(x_prompt, x_sample, c_prompt, c_sample, w_ada, b_ada, norm1_g, w_in, conv_qkv_w, a_log, dt_bias, dn_norm_g, w_br_attn, w_br_dn, w_out, norm2_g, w_up, ffn_conv_w, ffn_conv_b, w_down, norm_f_g):
    d = D_MODEL
    assert w_ada.shape[0] == 1, "single layer"
    w = w_in[0]
    o_qd = 3 * ATTN_W
    o_zd = o_qd + 3 * DN_W
    o_ab = o_zd + DN_W
    o_gate = o_ab + 4 * DN_HEADS
    pad16 = lambda v: jnp.pad(v.reshape(1, 2 * DN_HEADS).astype(F32), ((0, 0), (0, LANES - 2 * DN_HEADS)))
    p = {
        "norm1_g": norm1_g[0].reshape(1, d),
        "w_attn": [jnp.concatenate([w[:, kind * ATTN_W + gi * GROUP_W:kind * ATTN_W + (gi + 1) * GROUP_W]
                                    for kind in range(3)], axis=1).astype(BF16)
                   for gi in range(N_GROUPS)],
        "w_qkvd": w[:, o_qd:o_zd].astype(BF16),
        "w_zd": w[:, o_zd:o_ab].astype(BF16),
        "w_ab": jnp.pad(w[:, o_ab:o_gate], ((0, 0), (0, LANES - 4 * DN_HEADS))).astype(BF16),
        "w_gate": w[:, o_gate:].astype(BF16),
        "a_row": pad16(a_log[0]),
        "dt_row": pad16(dt_bias[0]),
        "conv_qkv_w": conv_qkv_w[0],
        "dn_norm_g": dn_norm_g[0].reshape(1, DN_HEAD_DIM),
        "w_br_attn": w_br_attn[0].astype(BF16),
        "w_br_dn": w_br_dn[0].astype(BF16),
        "w_out": w_out[0].astype(BF16),
        "norm2_g": norm2_g[0].reshape(1, d),
        "w_up": w_up[0].astype(BF16),
        "ffn_conv_w": ffn_conv_w[0],
        "ffn_conv_b": ffn_conv_b[0].reshape(1, 2 * D_FF),
        "w_down": w_down[0].astype(BF16),
        "norm_f_g": norm_f_g.reshape(1, d),
    }
    nb = x_prompt.shape[0]
    mod = _ada(jnp.concatenate([c_prompt, c_sample], axis=0), w_ada[0], b_ada[0].reshape(1, 6 * d))
    mod = mod.reshape(-1, 6, d)
    rope = _rope_tables(max(x_prompt.shape[1], x_sample.shape[1]))
    y_prompt = _trunk(x_prompt, mod[:nb], p, rope)
    y_sample = _trunk(x_sample, mod[nb:], p, rope)
    return (y_prompt, y_sample)
```

```python
import functools

import jax
import jax.numpy as jnp
import numpy as np
from jax import lax
from jax.experimental import pallas as pl
from jax.experimental.pallas import tpu as pltpu

F32 = jnp.float32
BF16 = jnp.bfloat16
HIGHEST = lax.Precision.HIGHEST

D_MODEL = 1024
ATTN_GROUPS = ((128, 1), (512, 4), (2048, 16))
N_GROUPS = 3
HEAD_DIM = 64
GROUP_W = 512
ATTN_W = N_GROUPS * GROUP_W
ATTN_HALF = 64
ATTN_BQ = 128
ROPE_THETA = 10000.0
NEG = -1e30
DN_HEADS = 8
DN_HEAD_DIM = 128
DN_W = DN_HEADS * DN_HEAD_DIM
DN_CHUNK = 256
DN_BASE = 16
DN_GROUP = 2
MM_SUB = 512
D_FF = 2816
EPS = 1e-6
LANES = 128
VMEM_LIMIT = 56 * 1024 * 1024


def _cparams():
    return pltpu.CompilerParams(vmem_limit_bytes=VMEM_LIMIT)


def _sigmoid(x):
    return 1.0 / (1.0 + jnp.exp(-x))


def _dot(a, b):
    return jnp.dot(a.astype(BF16), b.astype(BF16), preferred_element_type=F32)


def _dot_nt(a, b):
    return lax.dot_general(a.astype(BF16), b.astype(BF16), (((1,), (1,)), ((), ())),
                           preferred_element_type=F32)


def _dot_tn(a, b):
    return lax.dot_general(a.astype(BF16), b.astype(BF16), (((0,), (0,)), ((), ())),
                           preferred_element_type=F32)


def _rms(x, g):
    return x * lax.rsqrt(jnp.mean(x * x, axis=-1, keepdims=True) + EPS) * g


def _ada_kernel(c_ref, w_ref, b_ref, o_ref):
    c = c_ref[...]
    s = c * _sigmoid(c)
    o_ref[...] = jnp.dot(s, w_ref[...], preferred_element_type=F32, precision=HIGHEST) + b_ref[...]


def _ada(c, w, b):
    bt, d = c.shape
    n = w.shape[1]
    tn = 1024
    return pl.pallas_call(
        _ada_kernel,
        grid=(n // tn,),
        in_specs=[pl.BlockSpec((bt, d), lambda j: (0, 0)),
                  pl.BlockSpec((d, tn), lambda j: (0, j)),
                  pl.BlockSpec((1, tn), lambda j: (0, j))],
        out_specs=pl.BlockSpec((bt, tn), lambda j: (0, j)),
        out_shape=jax.ShapeDtypeStruct((bt, n), F32),
        name="ada",
    )(c, w, b)


def _rope_table_kernel(inv_ref, cos_ref, sin_ref):
    tm = cos_ref.shape[0]
    pos = (pl.program_id(0) * tm + lax.broadcasted_iota(jnp.int32, (tm, LANES), 0)).astype(F32)
    ang = pos * inv_ref[...]
    lane = lax.broadcasted_iota(jnp.int32, (tm, LANES), 1)
    cos_ref[...] = jnp.cos(ang)
    sin_ref[...] = jnp.sin(ang) * jnp.where(lane < LANES // 2, -1.0, 1.0)


def _rope_tables(s):
    half = HEAD_DIM // 2
    inv = ROPE_THETA ** (-(jnp.arange(LANES) % half).astype(F32) / half)
    tm = 512
    return pl.pallas_call(
        _rope_table_kernel,
        grid=(s // tm,),
        in_specs=[pl.BlockSpec((1, LANES), lambda i: (0, 0))],
        out_specs=[pl.BlockSpec((tm, LANES), lambda i: (i, 0))] * 2,
        out_shape=[jax.ShapeDtypeStruct((s, LANES), F32)] * 2,
        name="rope_tables",
    )(inv.reshape(1, LANES))


def _norm_mod_kernel(x_ref, mod_ref, g_ref, o_ref):
    h = _rms(x_ref[0], g_ref[...]) * (1.0 + mod_ref[0, 1:2, :]) + mod_ref[0, 0:1, :]
    o_ref[0] = h.astype(o_ref.dtype)


def _norm_mod(x, mod, g):
    b, s, d = x.shape
    tm = 512
    return pl.pallas_call(
        _norm_mod_kernel,
        grid=(b, s // tm),
        in_specs=[pl.BlockSpec((1, tm, d), lambda bi, i: (bi, i, 0)),
                  pl.BlockSpec((1, 6, d), lambda bi, i: (bi, 0, 0)),
                  pl.BlockSpec((1, d), lambda bi, i: (0, 0))],
        out_specs=pl.BlockSpec((1, tm, d), lambda bi, i: (bi, i, 0)),
        out_shape=jax.ShapeDtypeStruct((b, s, d), BF16),
        name="norm_mod",
    )(x, mod, g)


def _mm_kernel(h_ref, w_ref, o_ref):
    o_ref[0] = jnp.dot(h_ref[0], w_ref[...], preferred_element_type=F32).astype(o_ref.dtype)


def _mm(h, w, out_dtype):
    b, s, k = h.shape
    n = w.shape[1]
    tm, tn = 1024, 512
    return pl.pallas_call(
        _mm_kernel,
        grid=(b, s // tm, n // tn),
        in_specs=[pl.BlockSpec((1, tm, k), lambda bi, i, j: (bi, i, 0)),
                  pl.BlockSpec((k, tn), lambda bi, i, j: (0, j))],
        out_specs=pl.BlockSpec((1, tm, tn), lambda bi, i, j: (bi, i, j)),
        out_shape=jax.ShapeDtypeStruct((b, s, n), out_dtype),
        compiler_params=_cparams(),
        name="proj",
    )(h, w)


def _proj_attn_kernel(h_ref, w_ref, cos_ref, sin_ref, o_ref, *scratch, dil):
    is_qk = pl.program_id(2) < 2
    tm = h_ref.shape[1]
    n_slab = w_ref.shape[1] // LANES
    for s0 in range(0, tm, MM_SUB):
        rows = slice(s0, s0 + MM_SUB)
        acc = jnp.dot(h_ref[0, rows, :], w_ref[...], preferred_element_type=F32)
        cos = jnp.where(is_qk, cos_ref[rows, :], 1.0)
        sin = jnp.where(is_qk, sin_ref[rows, :], 0.0)
        for c in range(n_slab):
            t = acc[:, c * LANES:(c + 1) * LANES]
            val = t * cos + pltpu.roll(t, LANES // 2, 1) * sin
            if dil == 1:
                o_ref[0, 0, 0, rows, c * LANES:(c + 1) * LANES] = val.astype(o_ref.dtype)
            else:
                scratch[0][c, rows, :] = val
        if dil > 1:
            n = MM_SUB // dil
            m0 = s0 // dil
            for r in range(dil):
                for c in range(n_slab):
                    o_ref[0, 0, r, m0:m0 + n, c * LANES:(c + 1) * LANES] = (
                        scratch[0][c, pl.ds(s0 + r, n, stride=dil), :].astype(o_ref.dtype))


def _proj_attn(h, w, rope, dil, gi):
    b, s, k = h.shape
    tm, tn = 1024, GROUP_W
    scratch = [pltpu.VMEM((tn // LANES, tm, LANES), F32)] if dil > 1 else []
    return pl.pallas_call(
        functools.partial(_proj_attn_kernel, dil=dil),
        grid=(b, s // tm, 3),
        in_specs=[pl.BlockSpec((1, tm, k), lambda bi, i, j: (bi, i, 0)),
                  pl.BlockSpec((k, tn), lambda bi, i, j: (0, j)),
                  pl.BlockSpec((tm, LANES), lambda bi, i, j: (i, 0)),
                  pl.BlockSpec((tm, LANES), lambda bi, i, j: (i, 0))],
        out_specs=pl.BlockSpec((1, 1, dil, tm // dil, tn), lambda bi, i, j: (bi, j, 0, i, 0)),
        out_shape=jax.ShapeDtypeStruct((b, 3, dil, s // dil, tn), BF16),
        scratch_shapes=scratch,
        compiler_params=_cparams(),
        name=f"proj_attn_g{gi}",
    )(h, w, *rope)


def _gates_kernel(h_ref, w_ref, a_ref, dt_ref, o_ref, gt_ref):
    acc = jnp.dot(h_ref[0], w_ref[...], preferred_element_type=F32)
    c = acc.shape[0]
    x = acc + dt_ref[...]
    softplus = jnp.maximum(x, 0.0) + jnp.log1p(jnp.exp(-jnp.abs(x)))
    g = -jnp.exp(a_ref[...]) * softplus
    beta = _sigmoid(acc)
    r = lax.broadcasted_iota(jnp.int32, (c, c), 0)
    cc = lax.broadcasted_iota(jnp.int32, (c, c), 1)
    pre = jnp.dot(jnp.where(cc <= r, 1.0, 0.0).astype(F32), g, preferred_element_type=F32,
                  precision=HIGHEST)
    suf = jnp.dot(jnp.where(cc >= r, 1.0, 0.0).astype(F32), g, preferred_element_type=F32,
                  precision=HIGHEST)
    lane = lax.broadcasted_iota(jnp.int32, acc.shape, 1)
    out = jnp.where(lane < DN_HEADS, pre,
                    jnp.where(lane < 2 * DN_HEADS, suf,
                              jnp.where(lane < 4 * DN_HEADS, beta, 0.0)))
    o_ref[0] = out
    gt_ref[0, 0] = out.T[0:2 * DN_HEADS, :]


def _gates(h, w, a_row, dt_row):
    b, s, k = h.shape
    tm = DN_CHUNK
    return pl.pallas_call(
        _gates_kernel,
        grid=(b, s // tm),
        in_specs=[pl.BlockSpec((1, tm, k), lambda bi, i: (bi, i, 0)),
                  pl.BlockSpec((k, LANES), lambda bi, i: (0, 0)),
                  pl.BlockSpec((1, LANES), lambda bi, i: (0, 0)),
                  pl.BlockSpec((1, LANES), lambda bi, i: (0, 0))],
        out_specs=[pl.BlockSpec((1, tm, LANES), lambda bi, i: (bi, i, 0)),
                   pl.BlockSpec((1, 1, 2 * DN_HEADS, tm), lambda bi, i: (bi, i, 0, 0))],
        out_shape=[jax.ShapeDtypeStruct((b, s, LANES), F32),
                   jax.ShapeDtypeStruct((b, s // tm, 2 * DN_HEADS, tm), F32)],
        name="dn_gates",
    )(h, w, a_row, dt_row)


def _attn_kernel(q_ref, k_ref, v_ref, o_ref, l_ref, *scratch, seq, kb, dil):
    bq = ATTN_BQ
    n_slab = GROUP_W // LANES
    q0 = pl.program_id(1) * bq
    ks = pl.multiple_of(jnp.clip(q0 - ATTN_HALF, 0, seq - kb), ATTN_HALF)
    qpos = q0 + lax.broadcasted_iota(jnp.int32, (bq, kb), 0)
    kpos = ks + lax.broadcasted_iota(jnp.int32, (bq, kb), 1)
    valid = jnp.abs(qpos - kpos) <= ATTN_HALF
    lane = lax.broadcasted_iota(jnp.int32, (bq, LANES), 1)

    def residue(r):
        q = q_ref[0, 0, r].astype(F32) * (HEAD_DIM ** -0.5)
        k = k_ref[0, 0, r, pl.ds(ks, kb), :]
        v = v_ref[0, 0, r, pl.ds(ks, kb), :]
        heads = [(hp, sub) for hp in range(n_slab) for sub in range(2)]
        in_head = [lane < HEAD_DIM, lane >= HEAD_DIM]
        qk_lanes = [(lane & (HEAD_DIM // 2)) == 0, (lane & (HEAD_DIM // 2)) != 0]
        slab = lambda x, hp: x[:, hp * LANES:(hp + 1) * LANES]
        s_ = [_dot_nt(jnp.where(qk_lanes[sub], slab(q, hp), 0.0), slab(k, hp)) for hp, sub in heads]
        s_ = [jnp.where(valid, s, NEG) for s in s_]
        mx_ = [jnp.max(s, axis=-1, keepdims=True) for s in s_]
        p_ = [jnp.exp(s - mx) for s, mx in zip(s_, mx_)]
        den_ = [jnp.sum(p, axis=-1, keepdims=True) for p in p_]
        o_ = [_dot(p, slab(v, hp)) / den for p, den, (hp, _) in zip(p_, den_, heads)]
        lse_ = [mx + jnp.log(den) for mx, den in zip(mx_, den_)]
        for hp in range(n_slab):
            sl = slice(hp * LANES, (hp + 1) * LANES)
            o_pair = jnp.where(in_head[1], o_[2 * hp + 1], o_[2 * hp])
            l_pair = jnp.where(in_head[1], lse_[2 * hp + 1], lse_[2 * hp])
            if dil == 1:
                o_ref[0, :, sl] = o_pair.astype(o_ref.dtype)
                l_ref[0, :, sl] = l_pair
            else:
                scratch[0][hp, pl.ds(r, bq, stride=dil), :] = o_pair
                scratch[1][hp, pl.ds(r, bq, stride=dil), :] = l_pair

    if dil == 1:
        residue(0)
    else:
        def body(r, carry):
            residue(r)
            return carry
        lax.fori_loop(0, dil, body, 0)
        for hp in range(n_slab):
            sl = slice(hp * LANES, (hp + 1) * LANES)
            o_ref[0, :, sl] = scratch[0][hp].astype(o_ref.dtype)
            l_ref[0, :, sl] = scratch[1][hp]


def _attn(qkv, gi, dil):
    b, _, _, seq, _ = qkv.shape
    s = seq * dil
    kb = min(2 * ATTN_BQ, seq)
    rows = ATTN_BQ * dil
    scratch = [pltpu.VMEM((GROUP_W // LANES, rows, LANES), F32)] * 2 if dil > 1 else []
    kv = lambda kind: pl.BlockSpec((1, 1, dil, seq, GROUP_W), lambda bi, m: (bi, kind, 0, 0, 0))
    return pl.pallas_call(
        functools.partial(_attn_kernel, seq=seq, kb=kb, dil=dil),
        grid=(b, seq // ATTN_BQ),
        in_specs=[pl.BlockSpec((1, 1, dil, ATTN_BQ, GROUP_W), lambda bi, m: (bi, 0, 0, m, 0)),
                  kv(1), kv(2)],
        out_specs=[pl.BlockSpec((1, rows, GROUP_W), lambda bi, m: (bi, m, 0))] * 2,
        out_shape=[jax.ShapeDtypeStruct((b, s, GROUP_W), BF16),
                   jax.ShapeDtypeStruct((b, s, GROUP_W), F32)],
        scratch_shapes=scratch,
        compiler_params=_cparams(),
        name=f"attn_g{gi}",
    )(qkv, qkv, qkv)


def _tri_inverse(a_list, masks_ref, bases, eye):
    ps = [-a * masks_ref[b] for a, b in zip(a_list, bases)]
    ts = [eye + x for x in ps]
    n = 2
    while n < DN_BASE:
        ps = [_dot(p, p) for p in ps]
        ts = [t + _dot(t, p) for t, p in zip(ts, ps)]
        n *= 2
    n = DN_BASE
    lvl = 1
    while n < DN_CHUNK:
        left = [_dot(t, a * masks_ref[b + lvl]) for t, a, b in zip(ts, a_list, bases)]
        ts = [t - _dot(l, t) for t, l in zip(ts, left)]
        n *= 2
        lvl += 1
    return ts


def _dn_kernel(q_ref, k_ref, v_ref, z_ref, cwq_ref, cwk_ref, cwv_ref, gates_ref, gt_ref, ng_ref,
               o_ref, pad_ref, qn_ref, kn_ref, vn_ref, masks_ref, u_ref, wq_ref, kd_ref, p_ref,
               gl_ref, *, seq):
    h = pl.program_id(1)
    c = DN_CHUNK
    nc = seq // c
    n_lvl = 1
    while DN_BASE << (n_lvl - 1) < c:
        n_lvl += 1

    r = lax.broadcasted_iota(jnp.int32, (c, c), 0)
    cc = lax.broadcasted_iota(jnp.int32, (c, c), 1)
    as_f32 = lambda m: jnp.where(m, 1.0, 0.0).astype(F32)
    eye = as_f32(r == cc)
    per_dir = 2 + n_lvl
    base_shift = DN_BASE.bit_length() - 1
    for d, (lo, hi) in enumerate(((cc, r), (r, cc))):
        masks_ref[d * per_dir + 0] = as_f32(lo <= hi)
        masks_ref[d * per_dir + 1] = as_f32(lo < hi)
        masks_ref[d * per_dir + 2] = as_f32((lo < hi) & ((lo >> base_shift) == (hi >> base_shift)))
        for lvl in range(1, n_lvl):
            sh = base_shift + lvl - 1
            lo_blk, hi_blk = lo >> sh, hi >> sh
            masks_ref[d * per_dir + 2 + lvl] = as_f32((hi_blk == lo_blk + 1) & ((lo_blk & 1) == 0))

    def prep(x_ref, w_ref, out_ref, kind):
        pad_ref[0:8, :] = jnp.zeros((8, LANES), F32)
        pad_ref[seq + 8:seq + 16, :] = jnp.zeros((8, LANES), F32)
        pad_ref[8:seq + 8, :] = x_ref[0].astype(F32)
        w = w_ref[...]
        for ci in range(nc):
            r0 = ci * c
            y = (pad_ref[r0 + 7:r0 + 7 + c, :] * w[0:1, :] + pad_ref[r0 + 8:r0 + 8 + c, :] * w[1:2, :]
                 + pad_ref[r0 + 9:r0 + 9 + c, :] * w[2:3, :])
            y = y * _sigmoid(y)
            if kind != "v":
                y = y * lax.rsqrt(jnp.sum(y * y, axis=-1, keepdims=True) + EPS)
            if kind == "q":
                y = y * (DN_HEAD_DIM ** -0.5)
            out_ref[r0:r0 + c, :] = y

    prep(q_ref, cwq_ref, qn_ref, "q")
    prep(k_ref, cwk_ref, kn_ref, "k")
    prep(v_ref, cwv_ref, vn_ref, "v")
    acc_ref = pad_ref
    acc_ref[0:seq, :] = jnp.zeros((seq, LANES), F32)

    lane = lax.broadcasted_iota(jnp.int32, (c, LANES), 1)

    def phase_a(i, carry):
        sys_, a_list, bases = [], [], []
        for ci in [DN_GROUP * i + j for j in range(DN_GROUP)]:
            r0 = pl.multiple_of(ci * c, c)
            q = qn_ref[pl.ds(r0, c), :]
            k = kn_ref[pl.ds(r0, c), :]
            v = vn_ref[pl.ds(r0, c), :]
            gch = gates_ref[0, pl.ds(r0, c), :]
            kk = _dot_nt(k, k)
            qk = _dot_nt(q, k)
            for d in range(2):
                col = lambda j: jnp.sum(jnp.where(lane == j, gch, 0.0), axis=1, keepdims=True)
                gc = col(d * DN_HEADS + h)
                beta = col((2 + d) * DN_HEADS + h)
                grow = gt_ref[0, ci, pl.ds(d * DN_HEADS + h, 1), :]
                g_end = gc[0:1, :] if d else gc[c - 1:c, :]
                dec = jnp.exp(jnp.minimum(gc - grow, 0.0))
                e_g = jnp.exp(gc)
                a_list.append(kk * (beta * dec) * masks_ref[d * per_dir + 1])
                bases.append(d * per_dir + 2)
                sys_.append(dict(
                    ci=ci, r0=r0, d=d,
                    rhs=jnp.concatenate([v * beta, k * (beta * e_g)], axis=1),
                    qg=(q * e_g).astype(BF16),
                    kd=(k * jnp.exp(g_end - gc)).astype(BF16),
                    p=(qk * dec * masks_ref[d * per_dir]).astype(BF16),
                    gl=jnp.broadcast_to(jnp.exp(g_end), (1, LANES))))
        t_list = _tri_inverse(a_list, masks_ref, bases, eye)
        uw_list = [_dot(t, s["rhs"]) for t, s in zip(t_list, sys_)]
        for s, uw in zip(sys_, uw_list):
            d, ci, r0 = s["d"], s["ci"], s["r0"]
            u_ref[d, pl.ds(r0, c), :] = uw[:, 0:LANES]
            wq_ref[d, ci, 0:c, :] = uw[:, LANES:2 * LANES].astype(BF16)
            wq_ref[d, ci, c:2 * c, :] = s["qg"]
            kd_ref[d, pl.ds(r0, c), :] = s["kd"]
            p_ref[d, ci] = s["p"]
            gl_ref[d, pl.ds(ci, 1), :] = s["gl"]
        return carry

    lax.fori_loop(0, nc // DN_GROUP, phase_a, 0)

    def phase_b(i, carry):
        cis = (i, nc - 1 - i)
        r0s = [pl.multiple_of(ci * c, c) for ci in cis]
        ws_qs = [jnp.dot(wq_ref[d, cis[d]], carry[d].astype(BF16), preferred_element_type=F32)
                 for d in range(2)]
        v_new = [(u_ref[d, pl.ds(r0s[d], c), :] - ws_qs[d][0:c]).astype(BF16) for d in range(2)]
        outs = [ws_qs[d][c:2 * c] + jnp.dot(p_ref[d, cis[d]], v_new[d], preferred_element_type=F32)
                for d in range(2)]
        new = [carry[d] * gl_ref[d, pl.ds(cis[d], 1), :]
               + _dot_tn(kd_ref[d, pl.ds(r0s[d], c), :], v_new[d]) for d in range(2)]
        for d in range(2):
            acc_ref[pl.ds(r0s[d], c), :] += outs[d]
        return tuple(new)

    zero = jnp.zeros((DN_HEAD_DIM, DN_HEAD_DIM), F32)
    lax.fori_loop(0, nc, phase_b, (zero, zero))

    def finish(ci, _):
        r0 = pl.multiple_of(ci * c, c)
        od = acc_ref[pl.ds(r0, c), :]
        z = z_ref[0, pl.ds(r0, c), :].astype(F32)
        y = _rms(od, ng_ref[...]) * (z * _sigmoid(z))
        o_ref[0, pl.ds(r0, c), :] = y.astype(o_ref.dtype)
        return 0

    lax.fori_loop(0, nc, finish, 0)


def _deltanet(qkvd, zd, gates, gt, conv_w, norm_g):
    b, s, _ = qkvd.shape
    hd = DN_HEAD_DIM
    nc = s // DN_CHUNK
    n_lvl = 1
    while DN_BASE << (n_lvl - 1) < DN_CHUNK:
        n_lvl += 1
    n_masks = 2 * (2 + n_lvl)
    col = lambda off: pl.BlockSpec((1, s, hd), lambda bi, h: (bi, 0, off + h))
    cw = lambda off: pl.BlockSpec((3, hd), lambda bi, h: (0, off + h))
    return pl.pallas_call(
        functools.partial(_dn_kernel, seq=s),
        grid=(b, DN_HEADS),
        in_specs=[col(0), col(DN_HEADS), col(2 * DN_HEADS), col(0),
                  cw(0), cw(DN_HEADS), cw(2 * DN_HEADS),
                  pl.BlockSpec((1, s, LANES), lambda bi, h: (bi, 0, 0)),
                  pl.BlockSpec((1, nc, 2 * DN_HEADS, DN_CHUNK), lambda bi, h: (bi, 0, 0, 0)),
                  pl.BlockSpec((1, hd), lambda bi, h: (0, 0))],
        out_specs=pl.BlockSpec((1, s, hd), lambda bi, h: (bi, 0, h)),
        out_shape=jax.ShapeDtypeStruct((b, s, DN_W), BF16),
        scratch_shapes=[pltpu.VMEM((s + 16, hd), F32),
                        pltpu.VMEM((s, hd), F32), pltpu.VMEM((s, hd), F32), pltpu.VMEM((s, hd), F32),
                        pltpu.VMEM((n_masks, DN_CHUNK, DN_CHUNK), F32),
                        pltpu.VMEM((2, s, hd), F32),
                        pltpu.VMEM((2, nc, 2 * DN_CHUNK, hd), BF16),
                        pltpu.VMEM((2, s, hd), BF16),
                        pltpu.VMEM((2, nc, DN_CHUNK, DN_CHUNK), BF16),
                        pltpu.VMEM((2, max(nc, 8), hd), F32)],
        compiler_params=_cparams(),
        name="deltanet",
    )(qkvd, qkvd, qkvd, zd, conv_w, conv_w, conv_w, gates, gt, norm_g)


def _merge_kernel(o1_ref, o2_ref, o3_ref, l1_ref, l2_ref, l3_ref, yd_ref, h1_ref, x_ref, mod_ref,
                  wg_ref, wba_ref, wbd_ref, wo_ref, g2_ref, x1_ref, h2_ref):
    l1, l2, l3 = l1_ref[0], l2_ref[0], l3_ref[0]
    mx = jnp.maximum(l1, jnp.maximum(l2, l3))
    e1, e2, e3 = jnp.exp(l1 - mx), jnp.exp(l2 - mx), jnp.exp(l3 - mx)
    ya = (e1 * o1_ref[0] + e2 * o2_ref[0] + e3 * o3_ref[0]) / (e1 + e2 + e3)
    d = D_MODEL
    h1 = h1_ref[0]
    gate_a = _sigmoid(jnp.dot(h1, wg_ref[:, 0:d], preferred_element_type=F32))
    gate_d = _sigmoid(jnp.dot(h1, wg_ref[:, d:2 * d], preferred_element_type=F32))
    merged = gate_a * _dot(ya, wba_ref[...]) + gate_d * _dot(yd_ref[0], wbd_ref[...])
    x1 = x_ref[0] + mod_ref[0, 2:3, :] * _dot(merged, wo_ref[...])
    x1_ref[0] = x1
    h2 = _rms(x1, g2_ref[...]) * (1.0 + mod_ref[0, 4:5, :]) + mod_ref[0, 3:4, :]
    h2_ref[0] = h2.astype(h2_ref.dtype)


def _merge_out(os_, ls_, yd, h1, x, mod, wg, wba, wbd, wo, g2):
    b, s, d = x.shape
    tm = 256
    row = lambda w: pl.BlockSpec((1, tm, w), lambda bi, i: (bi, i, 0))
    full = lambda a: pl.BlockSpec(a.shape, lambda bi, i: (0,) * a.ndim)
    return pl.pallas_call(
        _merge_kernel,
        grid=(b, s // tm),
        in_specs=[row(GROUP_W)] * 6 + [row(DN_W), row(d), row(d),
                                       pl.BlockSpec((1, 6, d), lambda bi, i: (bi, 0, 0)),
                                       full(wg), full(wba), full(wbd), full(wo), full(g2)],
        out_specs=[row(d), row(d)],
        out_shape=[jax.ShapeDtypeStruct((b, s, d), F32), jax.ShapeDtypeStruct((b, s, d), BF16)],
        compiler_params=_cparams(),
        name="merge_out",
    )(*os_, *ls_, yd, h1, x, mod, wg, wba, wbd, wo, g2)


FFN_HALO = 16


def _ffn_up_kernel(hp_ref, hm_ref, hn_ref, wv_ref, wg_ref, cwv_ref, cwg_ref, bv_ref, bg_ref,
                   o_ref, *, nt):
    i = pl.program_id(1)
    tm = hm_ref.shape[1]
    lhs_all = jnp.concatenate([hp_ref[0], hm_ref[0], hn_ref[0]], axis=0)
    n = MM_SUB + 2 * FFN_HALO
    for s0 in range(0, tm, MM_SUB):
        lhs = lhs_all[s0:s0 + n]
        keep_prev = jnp.where(i > 0, 1.0, 0.0) if s0 == 0 else None
        keep_next = jnp.where(i < nt - 1, 1.0, 0.0) if s0 + MM_SUB == tm else None
        ups = [jnp.dot(lhs, w_ref[...], preferred_element_type=F32) for w_ref in (wv_ref, wg_ref)]

        def conv(up, cw_ref, b_ref):
            head, tail = up[0:FFN_HALO], up[FFN_HALO + MM_SUB:]
            if keep_prev is not None:
                head = head * keep_prev
            if keep_next is not None:
                tail = tail * keep_next
            up = jnp.concatenate([head, up[FFN_HALO:FFN_HALO + MM_SUB], tail], axis=0)
            cw = cw_ref[...]
            y = (pltpu.roll(up, 1, 0) * cw[0:1, :] + up * cw[1:2, :]
                 + pltpu.roll(up, n - 1, 0) * cw[2:3, :])
            return y[FFN_HALO:FFN_HALO + MM_SUB] + b_ref[...]

        val = conv(ups[0], cwv_ref, bv_ref)
        gate = conv(ups[1], cwg_ref, bg_ref)
        o_ref[0, s0:s0 + MM_SUB, :] = (gate * _sigmoid(gate) * val).astype(o_ref.dtype)


def _ffn_up(h2, w_up, conv_w, conv_b):
    b, s, d = h2.shape
    tm, tn = 1024, 256
    nt = s // tm
    nj = D_FF // tn
    hb = tm // FFN_HALO
    return pl.pallas_call(
        functools.partial(_ffn_up_kernel, nt=nt),
        grid=(b, nt, nj),
        in_specs=[pl.BlockSpec((1, FFN_HALO, d), lambda bi, i, j: (bi, jnp.maximum(i * hb - 1, 0), 0)),
                  pl.BlockSpec((1, tm, d), lambda bi, i, j: (bi, i, 0)),
                  pl.BlockSpec((1, FFN_HALO, d),
                               lambda bi, i, j: (bi, jnp.minimum((i + 1) * hb, s // FFN_HALO - 1), 0)),
                  pl.BlockSpec((d, tn), lambda bi, i, j: (0, j)),
                  pl.BlockSpec((d, tn), lambda bi, i, j: (0, j + nj)),
                  pl.BlockSpec((3, tn), lambda bi, i, j: (0, j)),
                  pl.BlockSpec((3, tn), lambda bi, i, j: (0, j + nj)),
                  pl.BlockSpec((1, tn), lambda bi, i, j: (0, j)),
                  pl.BlockSpec((1, tn), lambda bi, i, j: (0, j + nj))],
        out_specs=pl.BlockSpec((1, tm, tn), lambda bi, i, j: (bi, i, j)),
        out_shape=jax.ShapeDtypeStruct((b, s, D_FF), BF16),
        compiler_params=_cparams(),
        name="ffn_up",
    )(h2, h2, h2, w_up, w_up, conv_w, conv_w, conv_b, conv_b)


def _ffn_down_kernel(a_ref, w_ref, x_ref, mod_ref, g_ref, o_ref):
    x2 = x_ref[0] + mod_ref[0, 5:6, :] * jnp.dot(a_ref[0], w_ref[...], preferred_element_type=F32)
    o_ref[0] = _rms(x2, g_ref[...])


def _ffn_down(act, w_down, x1, mod, g):
    b, s, d = x1.shape
    tm = 512
    return pl.pallas_call(
        _ffn_down_kernel,
        grid=(b, s // tm),
        in_specs=[pl.BlockSpec((1, tm, D_FF), lambda bi, i: (bi, i, 0)),
                  pl.BlockSpec((D_FF, d), lambda bi, i: (0, 0)),
                  pl.BlockSpec((1, tm, d), lambda bi, i: (bi, i, 0)),
                  pl.BlockSpec((1, 6, d), lambda bi, i: (bi, 0, 0)),
                  pl.BlockSpec((1, d), lambda bi, i: (0, 0))],
        out_specs=pl.BlockSpec((1, tm, d), lambda bi, i: (bi, i, 0)),
        out_shape=jax.ShapeDtypeStruct((b, s, d), F32),
        compiler_params=_cparams(),
        name="ffn_down",
    )(act, w_down, x1, mod, g)


def _qk_column_order():
    half = HEAD_DIM // 2
    return [(pair * 2 + j) * HEAD_DIM + f * half + i
            for pair in range(GROUP_W // LANES) for f in range(2) for j in range(2)
            for i in range(half)]


_QK_COLS = np.asarray(_qk_column_order(), np.int32)


def _trunk(x, mod, p, rope):
    b, s, _ = x.shape
    h1 = _norm_mod(x, mod, p["norm1_g"])
    qkvd = _mm(h1, p["w_qkvd"], BF16)
    zd = _mm(h1, p["w_zd"], BF16)
    gates, gt = _gates(h1, p["w_ab"], p["a_row"], p["dt_row"])
    os_, ls_ = [], []
    for gi, (_, dil) in enumerate(ATTN_GROUPS):
        o, l = _attn(_proj_attn(h1, p["w_attn"][gi], rope, dil, gi), gi, dil)
        os_.append(o)
        ls_.append(l)
    yd = _deltanet(qkvd, zd, gates, gt, p["conv_qkv_w"], p["dn_norm_g"])
    x1, h2 = _merge_out(os_, ls_, yd, h1, x, mod, p["w_gate"], p["w_br_attn"], p["w_br_dn"],
                        p["w_out"], p["norm2_g"])
    act = _ffn_up(h2, p["w_up"], p["ffn_conv_w"], p["ffn_conv_b"])
    return _ffn_down(act, p["w_down"], x1, mod, p["norm_f_g"])


def kernel(x_prompt, x_sample, c_prompt, c_sample, w_ada, b_ada, norm1_g, w_in, conv_qkv_w, a_log, dt_bias, dn_norm_g, w_br_attn, w_br_dn, w_out, norm2_g, w_up, ffn_conv_w, ffn_conv_b, w_down, norm_f_g):
    d = D_MODEL
    assert w_ada.shape[0] == 1, "single layer"
    w = w_in[0]
    o_qd = 3 * ATTN_W
    o_zd = o_qd + 3 * DN_W
    o_ab = o_zd + DN_W
    o_gate = o_ab + 4 * DN_HEADS
    pad16 = lambda v: jnp.pad(v.reshape(1, 2 * DN_HEADS).astype(F32), ((0, 0), (0, LANES - 2 * DN_HEADS)))
    p = {
        "norm1_g": norm1_g[0].reshape(1, d),
        "w_attn": [jnp.concatenate(
            [w[:, kind * ATTN_W + gi * GROUP_W:kind * ATTN_W + (gi + 1) * GROUP_W][:, cols]
             for kind, cols in ((0, _QK_COLS), (1, _QK_COLS), (2, slice(None)))], axis=1).astype(BF16)
            for gi in range(N_GROUPS)],
        "w_qkvd": w[:, o_qd:o_zd].astype(BF16),
        "w_zd": w[:, o_zd:o_ab].astype(BF16),
        "w_ab": jnp.pad(w[:, o_ab:o_gate], ((0, 0), (0, LANES - 4 * DN_HEADS))).astype(BF16),
        "w_gate": w[:, o_gate:].astype(BF16),
        "a_row": pad16(a_log[0]),
        "dt_row": pad16(dt_bias[0]),
        "conv_qkv_w": conv_qkv_w[0],
        "dn_norm_g": dn_norm_g[0].reshape(1, DN_HEAD_DIM),
        "w_br_attn": w_br_attn[0].astype(BF16),
        "w_br_dn": w_br_dn[0].astype(BF16),
        "w_out": w_out[0].astype(BF16),
        "norm2_g": norm2_g[0].reshape(1, d),
        "w_up": w_up[0].astype(BF16),
        "ffn_conv_w": ffn_conv_w[0],
        "ffn_conv_b": ffn_conv_b[0].reshape(1, 2 * D_FF),
        "w_down": w_down[0].astype(BF16),
        "norm_f_g": norm_f_g.reshape(1, d),
    }
    nb = x_prompt.shape[0]
    mod = _ada(jnp.concatenate([c_prompt, c_sample], axis=0), w_ada[0], b_ada[0].reshape(1, 6 * d))
    mod = mod.reshape(-1, 6, d)
    rope = _rope_tables(max(x_prompt.shape[1], x_sample.shape[1]))
    y_prompt = _trunk(x_prompt, mod[:nb], p, rope)
    y_sample = _trunk(x_sample, mod[nb:], p, rope)
    return (y_prompt, y_sample)
```

```python
import functools

import jax
import jax.numpy as jnp
import numpy as np
from jax import lax
from jax.experimental import pallas as pl
from jax.experimental.pallas import tpu as pltpu

F32 = jnp.float32
BF16 = jnp.bfloat16
HIGHEST = lax.Precision.HIGHEST

D_MODEL = 1024
ATTN_GROUPS = ((128, 1), (512, 4), (2048, 16))
N_GROUPS = 3
HEAD_DIM = 64
GROUP_W = 512
ATTN_W = N_GROUPS * GROUP_W
ATTN_HALF = 64
ATTN_BQ = 128
ROPE_THETA = 10000.0
NEG = -1e30
DN_HEADS = 8
DN_HEAD_DIM = 128
DN_W = DN_HEADS * DN_HEAD_DIM
DN_CHUNK = 256
DN_BASE = 16
DN_GROUP = 2
DN_HALO = 16
MM_SUB = 256
D_FF = 2816
EPS = 1e-6
LANES = 128
VMEM_LIMIT = 56 * 1024 * 1024


def _cparams(**kw):
    return pltpu.CompilerParams(vmem_limit_bytes=VMEM_LIMIT, **kw)


def _sigmoid(x):
    return 1.0 / (1.0 + jnp.exp(-x))


def _dot(a, b):
    return jnp.dot(a.astype(BF16), b.astype(BF16), preferred_element_type=F32)


def _dot_nt(a, b):
    return lax.dot_general(a.astype(BF16), b.astype(BF16), (((1,), (1,)), ((), ())),
                           preferred_element_type=F32)


def _dot_tn(a, b):
    return lax.dot_general(a.astype(BF16), b.astype(BF16), (((0,), (0,)), ((), ())),
                           preferred_element_type=F32)


def _rms(x, g):
    return x * lax.rsqrt(jnp.mean(x * x, axis=-1, keepdims=True) + EPS) * g


def _ada_kernel(c_ref, w_ref, b_ref, o_ref):
    c = c_ref[...]
    s = c * _sigmoid(c)
    o_ref[...] = jnp.dot(s, w_ref[...], preferred_element_type=F32, precision=HIGHEST) + b_ref[...]


def _ada(c, w, b):
    bt, d = c.shape
    n = w.shape[1]
    tn = 1024
    return pl.pallas_call(
        _ada_kernel,
        grid=(n // tn,),
        in_specs=[pl.BlockSpec((bt, d), lambda j: (0, 0)),
                  pl.BlockSpec((d, tn), lambda j: (0, j)),
                  pl.BlockSpec((1, tn), lambda j: (0, j))],
        out_specs=pl.BlockSpec((bt, tn), lambda j: (0, j)),
        out_shape=jax.ShapeDtypeStruct((bt, n), F32),
        name="ada",
    )(c, w, b)


def _rope_table_kernel(inv_ref, cos_ref, sin_ref):
    tm = cos_ref.shape[0]
    pos = (pl.program_id(0) * tm + lax.broadcasted_iota(jnp.int32, (tm, LANES), 0)).astype(F32)
    ang = pos * inv_ref[...]
    lane = lax.broadcasted_iota(jnp.int32, (tm, LANES), 1)
    cos_ref[...] = jnp.cos(ang)
    sin_ref[...] = jnp.sin(ang) * jnp.where(lane < LANES // 2, -1.0, 1.0)


def _rope_tables(s):
    half = HEAD_DIM // 2
    inv = ROPE_THETA ** (-(jnp.arange(LANES) % half).astype(F32) / half)
    tm = 512
    return pl.pallas_call(
        _rope_table_kernel,
        grid=(s // tm,),
        in_specs=[pl.BlockSpec((1, LANES), lambda i: (0, 0))],
        out_specs=[pl.BlockSpec((tm, LANES), lambda i: (i, 0))] * 2,
        out_shape=[jax.ShapeDtypeStruct((s, LANES), F32)] * 2,
        name="rope_tables",
    )(inv.reshape(1, LANES))


def _norm_mod_kernel(x_ref, mod_ref, g_ref, o_ref):
    h = _rms(x_ref[0], g_ref[...]) * (1.0 + mod_ref[0, 1:2, :]) + mod_ref[0, 0:1, :]
    o_ref[0] = h.astype(o_ref.dtype)


def _norm_mod(x, mod, g):
    b, s, d = x.shape
    tm = 512
    return pl.pallas_call(
        _norm_mod_kernel,
        grid=(b, s // tm),
        in_specs=[pl.BlockSpec((1, tm, d), lambda bi, i: (bi, i, 0)),
                  pl.BlockSpec((1, 6, d), lambda bi, i: (bi, 0, 0)),
                  pl.BlockSpec((1, d), lambda bi, i: (0, 0))],
        out_specs=pl.BlockSpec((1, tm, d), lambda bi, i: (bi, i, 0)),
        out_shape=jax.ShapeDtypeStruct((b, s, d), BF16),
        name="norm_mod",
    )(x, mod, g)


def _mm_kernel(h_ref, w_ref, o_ref):
    o_ref[0] = jnp.dot(h_ref[0], w_ref[...], preferred_element_type=F32).astype(o_ref.dtype)


def _mm(h, w, out_dtype):
    b, s, k = h.shape
    n = w.shape[1]
    tm, tn = 1024, 512
    return pl.pallas_call(
        _mm_kernel,
        grid=(b, s // tm, n // tn),
        in_specs=[pl.BlockSpec((1, tm, k), lambda bi, i, j: (bi, i, 0)),
                  pl.BlockSpec((k, tn), lambda bi, i, j: (0, j))],
        out_specs=pl.BlockSpec((1, tm, tn), lambda bi, i, j: (bi, i, j)),
        out_shape=jax.ShapeDtypeStruct((b, s, n), out_dtype),
        compiler_params=_cparams(),
        name="proj",
    )(h, w)


def _proj_attn_kernel(h_ref, w_ref, cos_ref, sin_ref, o_ref, *scratch, dil):
    is_qk = pl.program_id(2) < 2
    tm = h_ref.shape[1]
    n_slab = w_ref.shape[1] // LANES
    for s0 in range(0, tm, MM_SUB):
        rows = slice(s0, s0 + MM_SUB)
        acc = jnp.dot(h_ref[0, rows, :], w_ref[...], preferred_element_type=F32)
        cos = jnp.where(is_qk, cos_ref[rows, :], 1.0)
        sin = jnp.where(is_qk, sin_ref[rows, :], 0.0)
        for c in range(n_slab):
            t = acc[:, c * LANES:(c + 1) * LANES]
            val = t * cos + pltpu.roll(t, LANES // 2, 1) * sin
            if dil == 1:
                o_ref[0, 0, 0, rows, c * LANES:(c + 1) * LANES] = val.astype(o_ref.dtype)
            else:
                scratch[0][c, rows, :] = val
        if dil > 1:
            n = MM_SUB // dil
            m0 = s0 // dil
            for r in range(dil):
                for c in range(n_slab):
                    o_ref[0, 0, r, m0:m0 + n, c * LANES:(c + 1) * LANES] = (
                        scratch[0][c, pl.ds(s0 + r, n, stride=dil), :].astype(o_ref.dtype))


def _proj_attn(h, w, rope, dil, gi):
    b, s, k = h.shape
    tm, tn = 1024, GROUP_W
    scratch = [pltpu.VMEM((tn // LANES, tm, LANES), F32)] if dil > 1 else []
    return pl.pallas_call(
        functools.partial(_proj_attn_kernel, dil=dil),
        grid=(b, s // tm, 3),
        in_specs=[pl.BlockSpec((1, tm, k), lambda bi, i, j: (bi, i, 0)),
                  pl.BlockSpec((k, tn), lambda bi, i, j: (0, j)),
                  pl.BlockSpec((tm, LANES), lambda bi, i, j: (i, 0)),
                  pl.BlockSpec((tm, LANES), lambda bi, i, j: (i, 0))],
        out_specs=pl.BlockSpec((1, 1, dil, tm // dil, tn), lambda bi, i, j: (bi, j, 0, i, 0)),
        out_shape=jax.ShapeDtypeStruct((b, 3, dil, s // dil, tn), BF16),
        scratch_shapes=scratch,
        compiler_params=_cparams(),
        name=f"proj_attn_g{gi}",
    )(h, w, *rope)


def _gates_kernel(h_ref, w_ref, a_ref, dt_ref, o_ref, gt_ref):
    acc = jnp.dot(h_ref[0], w_ref[...], preferred_element_type=F32)
    c = acc.shape[0]
    x = acc + dt_ref[...]
    softplus = jnp.maximum(x, 0.0) + jnp.log1p(jnp.exp(-jnp.abs(x)))
    g = -jnp.exp(a_ref[...]) * softplus
    beta = _sigmoid(acc)
    r = lax.broadcasted_iota(jnp.int32, (c, c), 0)
    cc = lax.broadcasted_iota(jnp.int32, (c, c), 1)
    pre = jnp.dot(jnp.where(cc <= r, 1.0, 0.0).astype(F32), g, preferred_element_type=F32,
                  precision=HIGHEST)
    suf = jnp.dot(jnp.where(cc >= r, 1.0, 0.0).astype(F32), g, preferred_element_type=F32,
                  precision=HIGHEST)
    lane = lax.broadcasted_iota(jnp.int32, acc.shape, 1)
    out = jnp.where(lane < DN_HEADS, pre,
                    jnp.where(lane < 2 * DN_HEADS, suf,
                              jnp.where(lane < 4 * DN_HEADS, beta, 0.0)))
    o_ref[0] = out
    gt_ref[0, 0] = out.T[0:2 * DN_HEADS, :]


def _gates(h, w, a_row, dt_row):
    b, s, k = h.shape
    tm = DN_CHUNK
    return pl.pallas_call(
        _gates_kernel,
        grid=(b, s // tm),
        in_specs=[pl.BlockSpec((1, tm, k), lambda bi, i: (bi, i, 0)),
                  pl.BlockSpec((k, LANES), lambda bi, i: (0, 0)),
                  pl.BlockSpec((1, LANES), lambda bi, i: (0, 0)),
                  pl.BlockSpec((1, LANES), lambda bi, i: (0, 0))],
        out_specs=[pl.BlockSpec((1, tm, LANES), lambda bi, i: (bi, i, 0)),
                   pl.BlockSpec((1, 1, 2 * DN_HEADS, tm), lambda bi, i: (bi, i, 0, 0))],
        out_shape=[jax.ShapeDtypeStruct((b, s, LANES), F32),
                   jax.ShapeDtypeStruct((b, s // tm, 2 * DN_HEADS, tm), F32)],
        name="dn_gates",
    )(h, w, a_row, dt_row)


def _attn_kernel(q_ref, k_ref, v_ref, o_ref, l_ref, *scratch, seq, bq, kb, dil):
    n_slab = GROUP_W // LANES
    q0 = pl.program_id(1) * bq
    ks = pl.multiple_of(jnp.clip(q0 - ATTN_HALF, 0, seq - kb), ATTN_HALF)
    qpos = q0 + lax.broadcasted_iota(jnp.int32, (bq, kb), 0)
    kpos = ks + lax.broadcasted_iota(jnp.int32, (bq, kb), 1)
    valid = jnp.abs(qpos - kpos) <= ATTN_HALF
    lane = lax.broadcasted_iota(jnp.int32, (bq, LANES), 1)

    def residue(r):
        q = q_ref[0, 0, r].astype(F32) * (HEAD_DIM ** -0.5)
        k = k_ref[0, 0, r, pl.ds(ks, kb), :]
        v = v_ref[0, 0, r, pl.ds(ks, kb), :]
        heads = [(hp, sub) for hp in range(n_slab) for sub in range(2)]
        in_head = [lane < HEAD_DIM, lane >= HEAD_DIM]
        qk_lanes = [(lane & (HEAD_DIM // 2)) == 0, (lane & (HEAD_DIM // 2)) != 0]
        slab = lambda x, hp: x[:, hp * LANES:(hp + 1) * LANES]
        s_ = [_dot_nt(jnp.where(qk_lanes[sub], slab(q, hp), 0.0), slab(k, hp)) for hp, sub in heads]
        s_ = [jnp.where(valid, s, NEG) for s in s_]
        mx_ = [jnp.max(s, axis=-1, keepdims=True) for s in s_]
        p_ = [jnp.exp(s - mx) for s, mx in zip(s_, mx_)]
        den_ = [jnp.sum(p, axis=-1, keepdims=True) for p in p_]
        o_ = [_dot(p, slab(v, hp)) / den for p, den, (hp, _) in zip(p_, den_, heads)]
        lse_ = [mx + jnp.log(den) for mx, den in zip(mx_, den_)]
        for hp in range(n_slab):
            sl = slice(hp * LANES, (hp + 1) * LANES)
            o_pair = jnp.where(in_head[1], o_[2 * hp + 1], o_[2 * hp])
            l_pair = jnp.where(in_head[1], lse_[2 * hp + 1], lse_[2 * hp])
            if dil == 1:
                o_ref[0, :, sl] = o_pair.astype(o_ref.dtype)
                l_ref[0, :, sl] = l_pair
            else:
                scratch[0][hp, pl.ds(r, bq, stride=dil), :] = o_pair
                scratch[1][hp, pl.ds(r, bq, stride=dil), :] = l_pair

    if dil == 1:
        residue(0)
    else:
        def body(r, carry):
            residue(r)
            return carry
        lax.fori_loop(0, dil, body, 0)
        for hp in range(n_slab):
            sl = slice(hp * LANES, (hp + 1) * LANES)
            o_ref[0, :, sl] = scratch[0][hp].astype(o_ref.dtype)
            l_ref[0, :, sl] = scratch[1][hp]


def _attn(qkv, gi, dil):
    b, _, _, seq, _ = qkv.shape
    s = seq * dil
    bq = min(ATTN_BQ, seq)
    kb = min(bq + 2 * ATTN_HALF, seq)
    rows = bq * dil
    scratch = [pltpu.VMEM((GROUP_W // LANES, rows, LANES), F32)] * 2 if dil > 1 else []
    kv = lambda kind: pl.BlockSpec((1, 1, dil, seq, GROUP_W), lambda bi, m: (bi, kind, 0, 0, 0))
    return pl.pallas_call(
        functools.partial(_attn_kernel, seq=seq, bq=bq, kb=kb, dil=dil),
        grid=(b, seq // bq),
        in_specs=[pl.BlockSpec((1, 1, dil, bq, GROUP_W), lambda bi, m: (bi, 0, 0, m, 0)),
                  kv(1), kv(2)],
        out_specs=[pl.BlockSpec((1, rows, GROUP_W), lambda bi, m: (bi, m, 0))] * 2,
        out_shape=[jax.ShapeDtypeStruct((b, s, GROUP_W), BF16),
                   jax.ShapeDtypeStruct((b, s, GROUP_W), F32)],
        scratch_shapes=scratch,
        compiler_params=_cparams(),
        name=f"attn_g{gi}",
    )(qkv, qkv, qkv)


def _tri_inverse(a_list, masks_ref, bases, uppers, eye):
    ps = [-a * masks_ref[b] for a, b in zip(a_list, bases)]
    ts = [eye + x for x in ps]
    n = 2
    while n < DN_BASE:
        ps = [_dot(p, p).astype(BF16) for p in ps]
        ts = [t + _dot(t, p) for t, p in zip(ts, ps)]
        n *= 2
    n = DN_BASE
    lvl = 1
    while 2 * n < DN_CHUNK:
        left = [_dot(t, a * masks_ref[b + lvl]) for t, a, b in zip(ts, a_list, bases)]
        ts = [t - _dot(l, t) for t, l in zip(ts, left)]
        n *= 2
        lvl += 1
    t11 = [t[0:n, 0:n] for t in ts]
    t22 = [t[n:, n:] for t in ts]
    zero = jnp.zeros((n, n), F32)
    left = [_dot(x11, a[0:n, n:]) if up else _dot(x22, a[n:, 0:n])
            for x11, x22, a, up in zip(t11, t22, a_list, uppers)]
    off = [-_dot(l, x22) if up else -_dot(l, x11) for l, x11, x22, up in zip(left, t11, t22, uppers)]
    return [jnp.concatenate([jnp.concatenate([x11, o if up else zero], axis=1),
                             jnp.concatenate([zero if up else o, x22], axis=1)], axis=0)
            for x11, x22, o, up in zip(t11, t22, off, uppers)]


def _dn_kernel(q_ref, k_ref, v_ref, z_ref, cwq_ref, cwk_ref, cwv_ref, gates_ref, gt_ref, ng_ref,
               o_ref, acc_ref, qkv_a_ref, qkv_b_ref, masks_ref, u_ref, wq_ref, kd_ref, p_ref,
               gl_ref, *, seq):
    h = pl.program_id(1)
    c = DN_CHUNK
    nc = seq // c
    n_lvl = 1
    while DN_BASE << n_lvl < c:
        n_lvl += 1

    r = lax.broadcasted_iota(jnp.int32, (c, c), 0)
    cc = lax.broadcasted_iota(jnp.int32, (c, c), 1)
    as_f32 = lambda m: jnp.where(m, 1.0, 0.0).astype(F32)
    eye = as_f32(r == cc)
    per_dir = 2 + n_lvl
    base_shift = DN_BASE.bit_length() - 1
    for d, (lo, hi) in enumerate(((cc, r), (r, cc))):
        masks_ref[d * per_dir + 0] = as_f32(lo <= hi)
        masks_ref[d * per_dir + 1] = as_f32(lo < hi)
        masks_ref[d * per_dir + 2] = as_f32((lo < hi) & ((lo >> base_shift) == (hi >> base_shift)))
        for lvl in range(1, n_lvl):
            sh = base_shift + lvl - 1
            lo_blk, hi_blk = lo >> sh, hi >> sh
            masks_ref[d * per_dir + 2 + lvl] = as_f32((hi_blk == lo_blk + 1) & ((lo_blk & 1) == 0))

    n_groups = nc // DN_GROUP

    def prep_group(g, dst_ref):
        halo = DN_HALO
        n = c + 2 * halo
        for j in range(DN_GROUP):
            ci = DN_GROUP * g + j
            r0 = pl.multiple_of(ci * c, c)
            lo = pl.multiple_of(jnp.maximum(r0 - halo, 0), halo)
            hi = pl.multiple_of(jnp.minimum(r0 + c, seq - halo), halo)
            keep_lo = jnp.where(ci > 0, 1.0, 0.0)
            keep_hi = jnp.where(ci < nc - 1, 1.0, 0.0)
            for idx, (x_ref, w_ref, kind) in enumerate(((q_ref, cwq_ref, "q"), (k_ref, cwk_ref, "k"),
                                                        (v_ref, cwv_ref, "v"))):
                x = jnp.concatenate([x_ref[0, pl.ds(lo, halo), :].astype(F32) * keep_lo,
                                     x_ref[0, pl.ds(r0, c), :].astype(F32),
                                     x_ref[0, pl.ds(hi, halo), :].astype(F32) * keep_hi], axis=0)
                w = w_ref[...]
                y = (pltpu.roll(x, 1, 0) * w[0:1, :] + x * w[1:2, :]
                     + pltpu.roll(x, n - 1, 0) * w[2:3, :])[halo:halo + c]
                y = y * _sigmoid(y)
                if kind != "v":
                    y = y * lax.rsqrt(jnp.sum(y * y, axis=-1, keepdims=True) + EPS)
                if kind == "q":
                    y = y * (DN_HEAD_DIM ** -0.5)
                dst_ref[idx, j * c:(j + 1) * c, :] = y

    prep_group(jnp.int32(0), qkv_a_ref)
    acc_ref[...] = jnp.zeros_like(acc_ref)

    lane = lax.broadcasted_iota(jnp.int32, (c, LANES), 1)

    def solve_group(g, src_ref):
        sys_, a_list, bases = [], [], []
        for j in range(DN_GROUP):
            ci = DN_GROUP * g + j
            r0 = pl.multiple_of(ci * c, c)
            q = src_ref[0, j * c:(j + 1) * c, :]
            k = src_ref[1, j * c:(j + 1) * c, :]
            v = src_ref[2, j * c:(j + 1) * c, :]
            gch = gates_ref[0, pl.ds(r0, c), :]
            kk = _dot_nt(k, k)
            qk = _dot_nt(q, k)
            for d in range(2):
                col = lambda j: jnp.sum(jnp.where(lane == j, gch, 0.0), axis=1, keepdims=True)
                gc = col(d * DN_HEADS + h)
                beta = col((2 + d) * DN_HEADS + h)
                grow = gt_ref[0, ci, pl.ds(d * DN_HEADS + h, 1), :]
                g_end = gc[0:1, :] if d else gc[c - 1:c, :]
                dec = jnp.exp(jnp.minimum(gc - grow, 0.0))
                e_g = jnp.exp(gc)
                a_list.append(kk * (beta * dec) * masks_ref[d * per_dir + 1])
                bases.append(d * per_dir + 2)
                sys_.append(dict(
                    ci=ci, r0=r0, d=d,
                    rhs=jnp.concatenate([v * beta, k * (beta * e_g)], axis=1),
                    qg=(q * e_g).astype(BF16),
                    kd=(k * jnp.exp(g_end - gc)).astype(BF16),
                    p=(qk * dec * masks_ref[d * per_dir]).astype(BF16),
                    gl=jnp.broadcast_to(jnp.exp(g_end), (1, LANES))))
        t_list = _tri_inverse(a_list, masks_ref, bases, [s["d"] == 1 for s in sys_], eye)
        uw_list = [_dot(t, s["rhs"]) for t, s in zip(t_list, sys_)]
        for s, uw in zip(sys_, uw_list):
            d, ci, r0 = s["d"], s["ci"], s["r0"]
            u_ref[d, pl.ds(r0, c), :] = uw[:, 0:LANES]
            wq_ref[d, ci, 0:c, :] = uw[:, LANES:2 * LANES].astype(BF16)
            wq_ref[d, ci, c:2 * c, :] = s["qg"]
            kd_ref[d, pl.ds(r0, c), :] = s["kd"]
            p_ref[d, ci] = s["p"]
            gl_ref[d, pl.ds(ci, 1), :] = s["gl"]

    def phase_a(m, carry):
        prep_group(2 * m + 1, qkv_b_ref)
        solve_group(2 * m, qkv_a_ref)
        prep_group(jnp.minimum(2 * m + 2, n_groups - 1), qkv_a_ref)
        solve_group(2 * m + 1, qkv_b_ref)
        return carry

    lax.fori_loop(0, n_groups // 2, phase_a, 0)

    def phase_b(i, carry):
        cis = (i, nc - 1 - i)
        r0s = [pl.multiple_of(ci * c, c) for ci in cis]
        ws_qs = [jnp.dot(wq_ref[d, cis[d]], carry[d].astype(BF16), preferred_element_type=F32)
                 for d in range(2)]
        v_new = [(u_ref[d, pl.ds(r0s[d], c), :] - ws_qs[d][0:c]).astype(BF16) for d in range(2)]
        outs = [ws_qs[d][c:2 * c] + jnp.dot(p_ref[d, cis[d]], v_new[d], preferred_element_type=F32)
                for d in range(2)]
        new = [carry[d] * gl_ref[d, pl.ds(cis[d], 1), :]
               + _dot_tn(kd_ref[d, pl.ds(r0s[d], c), :], v_new[d]) for d in range(2)]
        for d in range(2):
            acc_ref[pl.ds(r0s[d], c), :] += outs[d]
        return tuple(new)

    zero = jnp.zeros((DN_HEAD_DIM, DN_HEAD_DIM), F32)
    lax.fori_loop(0, nc, phase_b, (zero, zero))

    def finish(ci, _):
        r0 = pl.multiple_of(ci * c, c)
        od = acc_ref[pl.ds(r0, c), :]
        z = z_ref[0, pl.ds(r0, c), :].astype(F32)
        y = _rms(od, ng_ref[...]) * (z * _sigmoid(z))
        o_ref[0, pl.ds(r0, c), :] = y.astype(o_ref.dtype)
        return 0

    lax.fori_loop(0, nc, finish, 0)


def _deltanet(qkvd, zd, gates, gt, conv_w, norm_g):
    b, s, _ = qkvd.shape
    hd = DN_HEAD_DIM
    nc = s // DN_CHUNK
    n_lvl = 1
    while DN_BASE << n_lvl < DN_CHUNK:
        n_lvl += 1
    n_masks = 2 * (2 + n_lvl)
    col = lambda off: pl.BlockSpec((1, s, hd), lambda bi, h: (bi, 0, off + h))
    cw = lambda off: pl.BlockSpec((3, hd), lambda bi, h: (0, off + h))
    return pl.pallas_call(
        functools.partial(_dn_kernel, seq=s),
        grid=(b, DN_HEADS),
        in_specs=[col(0), col(DN_HEADS), col(2 * DN_HEADS), col(0),
                  cw(0), cw(DN_HEADS), cw(2 * DN_HEADS),
                  pl.BlockSpec((1, s, LANES), lambda bi, h: (bi, 0, 0)),
                  pl.BlockSpec((1, nc, 2 * DN_HEADS, DN_CHUNK), lambda bi, h: (bi, 0, 0, 0)),
                  pl.BlockSpec((1, hd), lambda bi, h: (0, 0))],
        out_specs=pl.BlockSpec((1, s, hd), lambda bi, h: (bi, 0, h)),
        out_shape=jax.ShapeDtypeStruct((b, s, DN_W), BF16),
        scratch_shapes=[pltpu.VMEM((s, hd), F32),
                        pltpu.VMEM((3, DN_GROUP * DN_CHUNK, hd), F32),
                        pltpu.VMEM((3, DN_GROUP * DN_CHUNK, hd), F32),
                        pltpu.VMEM((n_masks, DN_CHUNK, DN_CHUNK), F32),
                        pltpu.VMEM((2, s, hd), F32),
                        pltpu.VMEM((2, nc, 2 * DN_CHUNK, hd), BF16),
                        pltpu.VMEM((2, s, hd), BF16),
                        pltpu.VMEM((2, nc, DN_CHUNK, DN_CHUNK), BF16),
                        pltpu.VMEM((2, max(nc, 8), hd), F32)],
        compiler_params=_cparams(),
        name="deltanet",
    )(qkvd, qkvd, qkvd, zd, conv_w, conv_w, conv_w, gates, gt, norm_g)


def _merge_kernel(o1_ref, o2_ref, o3_ref, l1_ref, l2_ref, l3_ref, yd_ref, h1_ref, x_ref, mod_ref,
                  wg_ref, wba_ref, wbd_ref, wo_ref, g2_ref, x1_ref, h2_ref):
    l1, l2, l3 = l1_ref[0], l2_ref[0], l3_ref[0]
    mx = jnp.maximum(l1, jnp.maximum(l2, l3))
    e1, e2, e3 = jnp.exp(l1 - mx), jnp.exp(l2 - mx), jnp.exp(l3 - mx)
    ya = (e1 * o1_ref[0] + e2 * o2_ref[0] + e3 * o3_ref[0]) / (e1 + e2 + e3)
    d = D_MODEL
    h1 = h1_ref[0]
    gate_a = _sigmoid(jnp.dot(h1, wg_ref[:, 0:d], preferred_element_type=F32))
    gate_d = _sigmoid(jnp.dot(h1, wg_ref[:, d:2 * d], preferred_element_type=F32))
    merged = gate_a * _dot(ya, wba_ref[...]) + gate_d * _dot(yd_ref[0], wbd_ref[...])
    x1 = x_ref[0] + mod_ref[0, 2:3, :] * _dot(merged, wo_ref[...])
    x1_ref[0] = x1
    h2 = _rms(x1, g2_ref[...]) * (1.0 + mod_ref[0, 4:5, :]) + mod_ref[0, 3:4, :]
    h2_ref[0] = h2.astype(h2_ref.dtype)


def _merge_out(os_, ls_, yd, h1, x, mod, wg, wba, wbd, wo, g2):
    b, s, d = x.shape
    tm = 256
    row = lambda w: pl.BlockSpec((1, tm, w), lambda bi, i: (bi, i, 0))
    full = lambda a: pl.BlockSpec(a.shape, lambda bi, i: (0,) * a.ndim)
    return pl.pallas_call(
        _merge_kernel,
        grid=(b, s // tm),
        in_specs=[row(GROUP_W)] * 6 + [row(DN_W), row(d), row(d),
                                       pl.BlockSpec((1, 6, d), lambda bi, i: (bi, 0, 0)),
                                       full(wg), full(wba), full(wbd), full(wo), full(g2)],
        out_specs=[row(d), row(d)],
        out_shape=[jax.ShapeDtypeStruct((b, s, d), F32), jax.ShapeDtypeStruct((b, s, d), BF16)],
        compiler_params=_cparams(),
        name="merge_out",
    )(*os_, *ls_, yd, h1, x, mod, wg, wba, wbd, wo, g2)


FFN_HALO = 16
FFN_SUB = 512


def _ffn_up_kernel(hp_ref, hm_ref, hn_ref, wv_ref, wg_ref, cwv_ref, cwg_ref, bv_ref, bg_ref,
                   o_ref, *, nt):
    i = pl.program_id(1)
    tm = hm_ref.shape[1]
    lhs_all = jnp.concatenate([hp_ref[0], hm_ref[0], hn_ref[0]], axis=0)
    sub = FFN_SUB
    n = sub + 2 * FFN_HALO
    for s0 in range(0, tm, sub):
        lhs = lhs_all[s0:s0 + n]
        keep_prev = jnp.where(i > 0, 1.0, 0.0) if s0 == 0 else None
        keep_next = jnp.where(i < nt - 1, 1.0, 0.0) if s0 + sub == tm else None
        ups = [jnp.dot(lhs, w_ref[...], preferred_element_type=F32) for w_ref in (wv_ref, wg_ref)]

        def conv(up, cw_ref, b_ref):
            head, tail = up[0:FFN_HALO], up[FFN_HALO + sub:]
            if keep_prev is not None:
                head = head * keep_prev
            if keep_next is not None:
                tail = tail * keep_next
            up = jnp.concatenate([head, up[FFN_HALO:FFN_HALO + sub], tail], axis=0)
            cw = cw_ref[...]
            y = (pltpu.roll(up, 1, 0) * cw[0:1, :] + up * cw[1:2, :]
                 + pltpu.roll(up, n - 1, 0) * cw[2:3, :])
            return y[FFN_HALO:FFN_HALO + sub] + b_ref[...]

        val = conv(ups[0], cwv_ref, bv_ref)
        gate = conv(ups[1], cwg_ref, bg_ref)
        o_ref[0, s0:s0 + sub, :] = (gate * _sigmoid(gate) * val).astype(o_ref.dtype)


def _ffn_up(h2, w_up, conv_w, conv_b):
    b, s, d = h2.shape
    tm, tn = 1024, 256
    nt = s // tm
    nj = D_FF // tn
    hb = tm // FFN_HALO
    return pl.pallas_call(
        functools.partial(_ffn_up_kernel, nt=nt),
        grid=(b, nt, nj),
        in_specs=[pl.BlockSpec((1, FFN_HALO, d), lambda bi, i, j: (bi, jnp.maximum(i * hb - 1, 0), 0)),
                  pl.BlockSpec((1, tm, d), lambda bi, i, j: (bi, i, 0)),
                  pl.BlockSpec((1, FFN_HALO, d),
                               lambda bi, i, j: (bi, jnp.minimum((i + 1) * hb, s // FFN_HALO - 1), 0)),
                  pl.BlockSpec((d, tn), lambda bi, i, j: (0, j)),
                  pl.BlockSpec((d, tn), lambda bi, i, j: (0, j + nj)),
                  pl.BlockSpec((3, tn), lambda bi, i, j: (0, j)),
                  pl.BlockSpec((3, tn), lambda bi, i, j: (0, j + nj)),
                  pl.BlockSpec((1, tn), lambda bi, i, j: (0, j)),
                  pl.BlockSpec((1, tn), lambda bi, i, j: (0, j + nj))],
        out_specs=pl.BlockSpec((1, tm, tn), lambda bi, i, j: (bi, i, j)),
        out_shape=jax.ShapeDtypeStruct((b, s, D_FF), BF16),
        compiler_params=_cparams(),
        name="ffn_up",
    )(h2, h2, h2, w_up, w_up, conv_w, conv_w, conv_b, conv_b)


def _ffn_down_kernel(a_ref, w_ref, x_ref, mod_ref, g_ref, o_ref):
    x2 = x_ref[0] + mod_ref[0, 5:6, :] * jnp.dot(a_ref[0], w_ref[...], preferred_element_type=F32)
    o_ref[0] = _rms(x2, g_ref[...])


def _ffn_down(act, w_down, x1, mod, g):
    b, s, d = x1.shape
    tm = 512
    return pl.pallas_call(
        _ffn_down_kernel,
        grid=(b, s // tm),
        in_specs=[pl.BlockSpec((1, tm, D_FF), lambda bi, i: (bi, i, 0)),
                  pl.BlockSpec((D_FF, d), lambda bi, i: (0, 0)),
                  pl.BlockSpec((1, tm, d), lambda bi, i: (bi, i, 0)),
                  pl.BlockSpec((1, 6, d), lambda bi, i: (bi, 0, 0)),
                  pl.BlockSpec((1, d), lambda bi, i: (0, 0))],
        out_specs=pl.BlockSpec((1, tm, d), lambda bi, i: (bi, i, 0)),
        out_shape=jax.ShapeDtypeStruct((b, s, d), F32),
        compiler_params=_cparams(),
        name="ffn_down",
    )(act, w_down, x1, mod, g)


def _qk_column_order():
    half = HEAD_DIM // 2
    return [(pair * 2 + j) * HEAD_DIM + f * half + i
            for pair in range(GROUP_W // LANES) for f in range(2) for j in range(2)
            for i in range(half)]


_QK_COLS = np.asarray(_qk_column_order(), np.int32)


def _trunk(x, mod, p, rope):
    b, s, _ = x.shape
    h1 = _norm_mod(x, mod, p["norm1_g"])
    qkvd = _mm(h1, p["w_qkvd"], BF16)
    zd = _mm(h1, p["w_zd"], BF16)
    gates, gt = _gates(h1, p["w_ab"], p["a_row"], p["dt_row"])
    os_, ls_ = [], []
    for gi, (_, dil) in enumerate(ATTN_GROUPS):
        o, l = _attn(_proj_attn(h1, p["w_attn"][gi], rope, dil, gi), gi, dil)
        os_.append(o)
        ls_.append(l)
    yd = _deltanet(qkvd, zd, gates, gt, p["conv_qkv_w"], p["dn_norm_g"])
    x1, h2 = _merge_out(os_, ls_, yd, h1, x, mod, p["w_gate"], p["w_br_attn"], p["w_br_dn"],
                        p["w_out"], p["norm2_g"])
    act = _ffn_up(h2, p["w_up"], p["ffn_conv_w"], p["ffn_conv_b"])
    return _ffn_down(act, p["w_down"], x1, mod, p["norm_f_g"])


def kernel(x_prompt, x_sample, c_prompt, c_sample, w_ada, b_ada, norm1_g, w_in, conv_qkv_w, a_log, dt_bias, dn_norm_g, w_br_attn, w_br_dn, w_out, norm2_g, w_up, ffn_conv_w, ffn_conv_b, w_down, norm_f_g):
    d = D_MODEL
    assert w_ada.shape[0] == 1, "single layer"
    w = w_in[0]
    o_qd = 3 * ATTN_W
    o_zd = o_qd + 3 * DN_W
    o_ab = o_zd + DN_W
    o_gate = o_ab + 4 * DN_HEADS
    pad16 = lambda v: jnp.pad(v.reshape(1, 2 * DN_HEADS).astype(F32), ((0, 0), (0, LANES - 2 * DN_HEADS)))
    p = {
        "norm1_g": norm1_g[0].reshape(1, d),
        "w_attn": [jnp.concatenate(
            [w[:, kind * ATTN_W + gi * GROUP_W:kind * ATTN_W + (gi + 1) * GROUP_W][:, cols]
             for kind, cols in ((0, _QK_COLS), (1, _QK_COLS), (2, slice(None)))], axis=1).astype(BF16)
            for gi in range(N_GROUPS)],
        "w_qkvd": w[:, o_qd:o_zd].astype(BF16),
        "w_zd": w[:, o_zd:o_ab].astype(BF16),
        "w_ab": jnp.pad(w[:, o_ab:o_gate], ((0, 0), (0, LANES - 4 * DN_HEADS))).astype(BF16),
        "w_gate": w[:, o_gate:].astype(BF16),
        "a_row": pad16(a_log[0]),
        "dt_row": pad16(dt_bias[0]),
        "conv_qkv_w": conv_qkv_w[0],
        "dn_norm_g": dn_norm_g[0].reshape(1, DN_HEAD_DIM),
        "w_br_attn": w_br_attn[0].astype(BF16),
        "w_br_dn": w_br_dn[0].astype(BF16),
        "w_out": w_out[0].astype(BF16),
        "norm2_g": norm2_g[0].reshape(1, d),
        "w_up": w_up[0].astype(BF16),
        "ffn_conv_w": ffn_conv_w[0],
        "ffn_conv_b": ffn_conv_b[0].reshape(1, 2 * D_FF),
        "w_down": w_down[0].astype(BF16),
        "norm_f_g": norm_f_g.reshape(1, d),
    }
    nb = x_prompt.shape[0]
    mod = _ada(jnp.concatenate([c_prompt, c_sample], axis=0), w_ada[0], b_ada[0].reshape(1, 6 * d))
    mod = mod.reshape(-1, 6, d)
    rope = _rope_tables(max(x_prompt.shape[1], x_sample.shape[1]))
    y_prompt = _trunk(x_prompt, mod[:nb], p, rope)
    y_sample = _trunk(x_sample, mod[nb:], p, rope)
    return (y_prompt, y_sample)
```

```python
import functools

import jax
import jax.numpy as jnp
import numpy as np
from jax import lax
from jax.experimental import pallas as pl
from jax.experimental.pallas import tpu as pltpu

F32 = jnp.float32
BF16 = jnp.bfloat16
HIGHEST = lax.Precision.HIGHEST

D_MODEL = 1024
ATTN_GROUPS = ((128, 1), (512, 4), (2048, 16))
N_GROUPS = 3
HEAD_DIM = 64
GROUP_W = 512
ATTN_W = N_GROUPS * GROUP_W
ATTN_HALF = 64
ATTN_BQ = 128
ROPE_THETA = 10000.0
NEG = -1e30
DN_HEADS = 8
DN_HEAD_DIM = 128
DN_W = DN_HEADS * DN_HEAD_DIM
DN_CHUNK = 256
DN_BASE = 16
DN_GROUP = 2
DN_HALO = 16
MM_SUB = 256
D_FF = 2816
EPS = 1e-6
LANES = 128
VMEM_LIMIT = 56 * 1024 * 1024


def _cparams(**kw):
    return pltpu.CompilerParams(vmem_limit_bytes=VMEM_LIMIT, **kw)


def _sigmoid(x):
    return 1.0 / (1.0 + jnp.exp(-x))


def _dot(a, b):
    return jnp.dot(a.astype(BF16), b.astype(BF16), preferred_element_type=F32)


def _dot_nt(a, b):
    return lax.dot_general(a.astype(BF16), b.astype(BF16), (((1,), (1,)), ((), ())),
                           preferred_element_type=F32)


def _dot_tn(a, b):
    return lax.dot_general(a.astype(BF16), b.astype(BF16), (((0,), (0,)), ((), ())),
                           preferred_element_type=F32)


def _rms(x, g):
    return x * lax.rsqrt(jnp.mean(x * x, axis=-1, keepdims=True) + EPS) * g


def _ada_kernel(c_ref, w_ref, b_ref, o_ref):
    c = c_ref[...]
    s = c * _sigmoid(c)
    o_ref[...] = jnp.dot(s, w_ref[...], preferred_element_type=F32, precision=HIGHEST) + b_ref[...]


def _ada(c, w, b):
    bt, d = c.shape
    n = w.shape[1]
    tn = 1024
    return pl.pallas_call(
        _ada_kernel,
        grid=(n // tn,),
        in_specs=[pl.BlockSpec((bt, d), lambda j: (0, 0)),
                  pl.BlockSpec((d, tn), lambda j: (0, j)),
                  pl.BlockSpec((1, tn), lambda j: (0, j))],
        out_specs=pl.BlockSpec((bt, tn), lambda j: (0, j)),
        out_shape=jax.ShapeDtypeStruct((bt, n), F32),
        name="ada",
    )(c, w, b)


def _rope_table_kernel(inv_ref, cos_ref, sin_ref):
    tm = cos_ref.shape[0]
    pos = (pl.program_id(0) * tm + lax.broadcasted_iota(jnp.int32, (tm, LANES), 0)).astype(F32)
    ang = pos * inv_ref[...]
    lane = lax.broadcasted_iota(jnp.int32, (tm, LANES), 1)
    cos_ref[...] = jnp.cos(ang)
    sin_ref[...] = jnp.sin(ang) * jnp.where(lane < LANES // 2, -1.0, 1.0)


def _rope_tables(s):
    half = HEAD_DIM // 2
    inv = ROPE_THETA ** (-(jnp.arange(LANES) % half).astype(F32) / half)
    tm = 512
    return pl.pallas_call(
        _rope_table_kernel,
        grid=(s // tm,),
        in_specs=[pl.BlockSpec((1, LANES), lambda i: (0, 0))],
        out_specs=[pl.BlockSpec((tm, LANES), lambda i: (i, 0))] * 2,
        out_shape=[jax.ShapeDtypeStruct((s, LANES), F32)] * 2,
        name="rope_tables",
    )(inv.reshape(1, LANES))


def _norm_mod_kernel(x_ref, mod_ref, g_ref, o_ref):
    h = _rms(x_ref[0], g_ref[...]) * (1.0 + mod_ref[0, 1:2, :]) + mod_ref[0, 0:1, :]
    o_ref[0] = h.astype(o_ref.dtype)


def _norm_mod(x, mod, g):
    b, s, d = x.shape
    tm = 512
    return pl.pallas_call(
        _norm_mod_kernel,
        grid=(b, s // tm),
        in_specs=[pl.BlockSpec((1, tm, d), lambda bi, i: (bi, i, 0)),
                  pl.BlockSpec((1, 6, d), lambda bi, i: (bi, 0, 0)),
                  pl.BlockSpec((1, d), lambda bi, i: (0, 0))],
        out_specs=pl.BlockSpec((1, tm, d), lambda bi, i: (bi, i, 0)),
        out_shape=jax.ShapeDtypeStruct((b, s, d), BF16),
        name="norm_mod",
    )(x, mod, g)


def _mm_kernel(h_ref, w_ref, o_ref):
    o_ref[0] = jnp.dot(h_ref[0], w_ref[...], preferred_element_type=F32).astype(o_ref.dtype)


def _mm(h, w, out_dtype):
    b, s, k = h.shape
    n = w.shape[1]
    tm, tn = 1024, 512
    return pl.pallas_call(
        _mm_kernel,
        grid=(b, s // tm, n // tn),
        in_specs=[pl.BlockSpec((1, tm, k), lambda bi, i, j: (bi, i, 0)),
                  pl.BlockSpec((k, tn), lambda bi, i, j: (0, j))],
        out_specs=pl.BlockSpec((1, tm, tn), lambda bi, i, j: (bi, i, j)),
        out_shape=jax.ShapeDtypeStruct((b, s, n), out_dtype),
        compiler_params=_cparams(),
        name="proj",
    )(h, w)


def _proj_attn_kernel(h_ref, w_ref, cos_ref, sin_ref, o_ref, *scratch, dil):
    is_qk = pl.program_id(2) < 2
    tm = h_ref.shape[1]
    n_slab = w_ref.shape[1] // LANES
    for s0 in range(0, tm, MM_SUB):
        rows = slice(s0, s0 + MM_SUB)
        acc = jnp.dot(h_ref[0, rows, :], w_ref[...], preferred_element_type=F32)
        cos = jnp.where(is_qk, cos_ref[rows, :], 1.0)
        sin = jnp.where(is_qk, sin_ref[rows, :], 0.0)
        for c in range(n_slab):
            t = acc[:, c * LANES:(c + 1) * LANES]
            val = t * cos + pltpu.roll(t, LANES // 2, 1) * sin
            if dil == 1:
                o_ref[0, 0, 0, rows, c * LANES:(c + 1) * LANES] = val.astype(o_ref.dtype)
            else:
                scratch[0][c, rows, :] = val
        if dil > 1:
            n = MM_SUB // dil
            m0 = s0 // dil
            for r in range(dil):
                for c in range(n_slab):
                    o_ref[0, 0, r, m0:m0 + n, c * LANES:(c + 1) * LANES] = (
                        scratch[0][c, pl.ds(s0 + r, n, stride=dil), :].astype(o_ref.dtype))


def _proj_attn(h, w, rope, dil, gi):
    b, s, k = h.shape
    tm, tn = 1024, GROUP_W
    scratch = [pltpu.VMEM((tn // LANES, tm, LANES), F32)] if dil > 1 else []
    return pl.pallas_call(
        functools.partial(_proj_attn_kernel, dil=dil),
        grid=(b, s // tm, 3),
        in_specs=[pl.BlockSpec((1, tm, k), lambda bi, i, j: (bi, i, 0)),
                  pl.BlockSpec((k, tn), lambda bi, i, j: (0, j)),
                  pl.BlockSpec((tm, LANES), lambda bi, i, j: (i, 0)),
                  pl.BlockSpec((tm, LANES), lambda bi, i, j: (i, 0))],
        out_specs=pl.BlockSpec((1, 1, dil, tm // dil, tn), lambda bi, i, j: (bi, j, 0, i, 0)),
        out_shape=jax.ShapeDtypeStruct((b, 3, dil, s // dil, tn), BF16),
        scratch_shapes=scratch,
        compiler_params=_cparams(),
        name=f"proj_attn_g{gi}",
    )(h, w, *rope)


def _gates_kernel(h_ref, w_ref, a_ref, dt_ref, o_ref, gt_ref):
    acc = jnp.dot(h_ref[0], w_ref[...], preferred_element_type=F32)
    c = acc.shape[0]
    x = acc + dt_ref[...]
    softplus = jnp.maximum(x, 0.0) + jnp.log1p(jnp.exp(-jnp.abs(x)))
    g = -jnp.exp(a_ref[...]) * softplus
    beta = _sigmoid(acc)
    r = lax.broadcasted_iota(jnp.int32, (c, c), 0)
    cc = lax.broadcasted_iota(jnp.int32, (c, c), 1)
    pre = jnp.dot(jnp.where(cc <= r, 1.0, 0.0).astype(F32), g, preferred_element_type=F32,
                  precision=HIGHEST)
    suf = jnp.dot(jnp.where(cc >= r, 1.0, 0.0).astype(F32), g, preferred_element_type=F32,
                  precision=HIGHEST)
    lane = lax.broadcasted_iota(jnp.int32, acc.shape, 1)
    out = jnp.where(lane < DN_HEADS, pre,
                    jnp.where(lane < 2 * DN_HEADS, suf,
                              jnp.where(lane < 4 * DN_HEADS, beta, 0.0)))
    o_ref[0] = out
    gt_ref[0, 0] = out.T[0:2 * DN_HEADS, :]


def _gates(h, w, a_row, dt_row):
    b, s, k = h.shape
    tm = DN_CHUNK
    return pl.pallas_call(
        _gates_kernel,
        grid=(b, s // tm),
        in_specs=[pl.BlockSpec((1, tm, k), lambda bi, i: (bi, i, 0)),
                  pl.BlockSpec((k, LANES), lambda bi, i: (0, 0)),
                  pl.BlockSpec((1, LANES), lambda bi, i: (0, 0)),
                  pl.BlockSpec((1, LANES), lambda bi, i: (0, 0))],
        out_specs=[pl.BlockSpec((1, tm, LANES), lambda bi, i: (bi, i, 0)),
                   pl.BlockSpec((1, 1, 2 * DN_HEADS, tm), lambda bi, i: (bi, i, 0, 0))],
        out_shape=[jax.ShapeDtypeStruct((b, s, LANES), F32),
                   jax.ShapeDtypeStruct((b, s // tm, 2 * DN_HEADS, tm), F32)],
        name="dn_gates",
    )(h, w, a_row, dt_row)


def _attn_kernel(q_ref, k_ref, v_ref, o_ref, l_ref, *scratch, seq, bq, kb, dil):
    n_slab = GROUP_W // LANES
    q0 = pl.program_id(1) * bq
    ks = pl.multiple_of(jnp.clip(q0 - ATTN_HALF, 0, seq - kb), ATTN_HALF)
    qpos = q0 + lax.broadcasted_iota(jnp.int32, (bq, kb), 0)
    kpos = ks + lax.broadcasted_iota(jnp.int32, (bq, kb), 1)
    valid = jnp.abs(qpos - kpos) <= ATTN_HALF
    lane = lax.broadcasted_iota(jnp.int32, (bq, LANES), 1)

    def residue(r):
        q = q_ref[0, 0, r].astype(F32) * (HEAD_DIM ** -0.5)
        k = k_ref[0, 0, r, pl.ds(ks, kb), :]
        v = v_ref[0, 0, r, pl.ds(ks, kb), :]
        heads = [(hp, sub) for hp in range(n_slab) for sub in range(2)]
        in_head = [lane < HEAD_DIM, lane >= HEAD_DIM]
        qk_lanes = [(lane & (HEAD_DIM // 2)) == 0, (lane & (HEAD_DIM // 2)) != 0]
        slab = lambda x, hp: x[:, hp * LANES:(hp + 1) * LANES]
        s_ = [_dot_nt(jnp.where(qk_lanes[sub], slab(q, hp), 0.0), slab(k, hp)) for hp, sub in heads]
        s_ = [jnp.where(valid, s, NEG) for s in s_]
        mx_ = [jnp.max(s, axis=-1, keepdims=True) for s in s_]
        p_ = [jnp.exp(s - mx) for s, mx in zip(s_, mx_)]
        den_ = [jnp.sum(p, axis=-1, keepdims=True) for p in p_]
        o_ = [_dot(p, slab(v, hp)) / den for p, den, (hp, _) in zip(p_, den_, heads)]
        lse_ = [mx + jnp.log(den) for mx, den in zip(mx_, den_)]
        for hp in range(n_slab):
            sl = slice(hp * LANES, (hp + 1) * LANES)
            o_pair = jnp.where(in_head[1], o_[2 * hp + 1], o_[2 * hp])
            l_pair = jnp.where(in_head[1], lse_[2 * hp + 1], lse_[2 * hp])
            if dil == 1:
                o_ref[0, :, sl] = o_pair.astype(o_ref.dtype)
                l_ref[0, :, sl] = l_pair
            else:
                scratch[0][hp, pl.ds(r, bq, stride=dil), :] = o_pair
                scratch[1][hp, pl.ds(r, bq, stride=dil), :] = l_pair

    if dil == 1:
        residue(0)
    else:
        def body(r, carry):
            residue(r)
            return carry
        lax.fori_loop(0, dil, body, 0)
        for hp in range(n_slab):
            sl = slice(hp * LANES, (hp + 1) * LANES)
            o_ref[0, :, sl] = scratch[0][hp].astype(o_ref.dtype)
            l_ref[0, :, sl] = scratch[1][hp]


def _attn(qkv, gi, dil):
    b, _, _, seq, _ = qkv.shape
    s = seq * dil
    bq = min(ATTN_BQ, seq)
    kb = min(bq + 2 * ATTN_HALF, seq)
    rows = bq * dil
    scratch = [pltpu.VMEM((GROUP_W // LANES, rows, LANES), F32)] * 2 if dil > 1 else []
    kv = lambda kind: pl.BlockSpec((1, 1, dil, seq, GROUP_W), lambda bi, m: (bi, kind, 0, 0, 0))
    return pl.pallas_call(
        functools.partial(_attn_kernel, seq=seq, bq=bq, kb=kb, dil=dil),
        grid=(b, seq // bq),
        in_specs=[pl.BlockSpec((1, 1, dil, bq, GROUP_W), lambda bi, m: (bi, 0, 0, m, 0)),
                  kv(1), kv(2)],
        out_specs=[pl.BlockSpec((1, rows, GROUP_W), lambda bi, m: (bi, m, 0))] * 2,
        out_shape=[jax.ShapeDtypeStruct((b, s, GROUP_W), BF16),
                   jax.ShapeDtypeStruct((b, s, GROUP_W), F32)],
        scratch_shapes=scratch,
        compiler_params=_cparams(),
        name=f"attn_g{gi}",
    )(qkv, qkv, qkv)


def _tri_inverse(a_list, masks_ref, bases, uppers, eye):
    ps = [-a * masks_ref[b] for a, b in zip(a_list, bases)]
    ts = [eye + x for x in ps]
    n = 2
    while n < DN_BASE:
        ps = [_dot(p, p).astype(BF16) for p in ps]
        ts = [t + _dot(t, p) for t, p in zip(ts, ps)]
        n *= 2
    n = DN_BASE
    lvl = 1
    while 2 * n < DN_CHUNK:
        left = [_dot(t, a * masks_ref[b + lvl]) for t, a, b in zip(ts, a_list, bases)]
        ts = [t - _dot(l, t) for t, l in zip(ts, left)]
        n *= 2
        lvl += 1
    t11 = [t[0:n, 0:n] for t in ts]
    t22 = [t[n:, n:] for t in ts]
    zero = jnp.zeros((n, n), F32)
    left = [_dot(x11, a[0:n, n:]) if up else _dot(x22, a[n:, 0:n])
            for x11, x22, a, up in zip(t11, t22, a_list, uppers)]
    off = [-_dot(l, x22) if up else -_dot(l, x11) for l, x11, x22, up in zip(left, t11, t22, uppers)]
    return [jnp.concatenate([jnp.concatenate([x11, o if up else zero], axis=1),
                             jnp.concatenate([zero if up else o, x22], axis=1)], axis=0)
            for x11, x22, o, up in zip(t11, t22, off, uppers)]


def _dn_kernel(q_ref, k_ref, v_ref, z_ref, cwq_ref, cwk_ref, cwv_ref, gates_ref, gt_ref, ng_ref,
               o_ref, acc_ref, qkv_a_ref, qkv_b_ref, masks_ref, u_ref, wq_ref, kd_ref, p_ref,
               gl_ref, *, seq):
    h = pl.program_id(1)
    c = DN_CHUNK
    nc = seq // c
    n_lvl = 1
    while DN_BASE << n_lvl < c:
        n_lvl += 1

    r = lax.broadcasted_iota(jnp.int32, (c, c), 0)
    cc = lax.broadcasted_iota(jnp.int32, (c, c), 1)
    as_f32 = lambda m: jnp.where(m, 1.0, 0.0).astype(F32)
    eye = as_f32(r == cc)
    per_dir = 2 + n_lvl
    base_shift = DN_BASE.bit_length() - 1
    for d, (lo, hi) in enumerate(((cc, r), (r, cc))):
        masks_ref[d * per_dir + 0] = as_f32(lo <= hi)
        masks_ref[d * per_dir + 1] = as_f32(lo < hi)
        masks_ref[d * per_dir + 2] = as_f32((lo < hi) & ((lo >> base_shift) == (hi >> base_shift)))
        for lvl in range(1, n_lvl):
            sh = base_shift + lvl - 1
            lo_blk, hi_blk = lo >> sh, hi >> sh
            masks_ref[d * per_dir + 2 + lvl] = as_f32((hi_blk == lo_blk + 1) & ((lo_blk & 1) == 0))

    n_groups = nc // DN_GROUP

    def prep_group(g, dst_ref):
        halo = DN_HALO
        n = c + 2 * halo
        for j in range(DN_GROUP):
            ci = DN_GROUP * g + j
            r0 = pl.multiple_of(ci * c, c)
            lo = pl.multiple_of(jnp.maximum(r0 - halo, 0), halo)
            hi = pl.multiple_of(jnp.minimum(r0 + c, seq - halo), halo)
            keep_lo = jnp.where(ci > 0, 1.0, 0.0)
            keep_hi = jnp.where(ci < nc - 1, 1.0, 0.0)
            for idx, (x_ref, w_ref, kind) in enumerate(((q_ref, cwq_ref, "q"), (k_ref, cwk_ref, "k"),
                                                        (v_ref, cwv_ref, "v"))):
                x = jnp.concatenate([x_ref[0, pl.ds(lo, halo), :].astype(F32) * keep_lo,
                                     x_ref[0, pl.ds(r0, c), :].astype(F32),
                                     x_ref[0, pl.ds(hi, halo), :].astype(F32) * keep_hi], axis=0)
                w = w_ref[...]
                y = (pltpu.roll(x, 1, 0) * w[0:1, :] + x * w[1:2, :]
                     + pltpu.roll(x, n - 1, 0) * w[2:3, :])[halo:halo + c]
                y = y * _sigmoid(y)
                if kind != "v":
                    y = y * lax.rsqrt(jnp.sum(y * y, axis=-1, keepdims=True) + EPS)
                if kind == "q":
                    y = y * (DN_HEAD_DIM ** -0.5)
                dst_ref[idx, j * c:(j + 1) * c, :] = y

    prep_group(jnp.int32(0), qkv_a_ref)

    lane = lax.broadcasted_iota(jnp.int32, (c, LANES), 1)

    def solve_group(g, src_ref):
        sys_, a_list, bases = [], [], []
        for j in range(DN_GROUP):
            ci = DN_GROUP * g + j
            r0 = pl.multiple_of(ci * c, c)
            q = src_ref[0, j * c:(j + 1) * c, :]
            k = src_ref[1, j * c:(j + 1) * c, :]
            v = src_ref[2, j * c:(j + 1) * c, :]
            gch = gates_ref[0, pl.ds(r0, c), :]
            kk = _dot_nt(k, k)
            qk = _dot_nt(q, k)
            for d in range(2):
                col = lambda j: jnp.sum(jnp.where(lane == j, gch, 0.0), axis=1, keepdims=True)
                gc = col(d * DN_HEADS + h)
                beta = col((2 + d) * DN_HEADS + h)
                grow = gt_ref[0, ci, pl.ds(d * DN_HEADS + h, 1), :]
                g_end = gc[0:1, :] if d else gc[c - 1:c, :]
                dec = jnp.exp(jnp.minimum(gc - grow, 0.0))
                e_g = jnp.exp(gc)
                a_list.append(kk * (beta * dec) * masks_ref[d * per_dir + 1])
                bases.append(d * per_dir + 2)
                sys_.append(dict(
                    ci=ci, r0=r0, d=d,
                    rhs=jnp.concatenate([v * beta, k * (beta * e_g)], axis=1),
                    qg=(q * e_g).astype(BF16),
                    kd=(k * jnp.exp(g_end - gc)).astype(BF16),
                    p=(qk * dec * masks_ref[d * per_dir]).astype(BF16),
                    gl=jnp.broadcast_to(jnp.exp(g_end), (1, LANES))))
        t_list = _tri_inverse(a_list, masks_ref, bases, [s["d"] == 1 for s in sys_], eye)
        uw_list = [_dot(t, s["rhs"]) for t, s in zip(t_list, sys_)]
        for s, uw in zip(sys_, uw_list):
            d, ci, r0 = s["d"], s["ci"], s["r0"]
            u_ref[d, pl.ds(r0, c), :] = uw[:, 0:LANES]
            wq_ref[d, ci, 0:c, :] = uw[:, LANES:2 * LANES].astype(BF16)
            wq_ref[d, ci, c:2 * c, :] = s["qg"]
            kd_ref[d, pl.ds(r0, c), :] = s["kd"]
            p_ref[d, ci] = s["p"]
            gl_ref[d, pl.ds(ci, 1), :] = s["gl"]

    def phase_a(m, carry):
        prep_group(2 * m + 1, qkv_b_ref)
        solve_group(2 * m, qkv_a_ref)
        prep_group(jnp.minimum(2 * m + 2, n_groups - 1), qkv_a_ref)
        solve_group(2 * m + 1, qkv_b_ref)
        return carry

    lax.fori_loop(0, n_groups // 2, phase_a, 0)

    def phase_b(i, carry, second_half):
        cis = (i, nc - 1 - i)
        r0s = [pl.multiple_of(ci * c, c) for ci in cis]
        ws_qs = [jnp.dot(wq_ref[d, cis[d]], carry[d].astype(BF16), preferred_element_type=F32)
                 for d in range(2)]
        v_new = [(u_ref[d, pl.ds(r0s[d], c), :] - ws_qs[d][0:c]).astype(BF16) for d in range(2)]
        outs = [ws_qs[d][c:2 * c] + jnp.dot(p_ref[d, cis[d]], v_new[d], preferred_element_type=F32)
                for d in range(2)]
        new = [carry[d] * gl_ref[d, pl.ds(cis[d], 1), :]
               + _dot_tn(kd_ref[d, pl.ds(r0s[d], c), :], v_new[d]) for d in range(2)]
        for d in range(2):
            rows = pl.ds(r0s[d], c)
            if not second_half:
                acc_ref[rows, :] = outs[d]
            else:
                z = z_ref[0, rows, :].astype(F32)
                y = _rms(acc_ref[rows, :] + outs[d], ng_ref[...]) * (z * _sigmoid(z))
                o_ref[0, rows, :] = y.astype(o_ref.dtype)
        return tuple(new)

    zero = jnp.zeros((DN_HEAD_DIM, DN_HEAD_DIM), F32)
    mid = lax.fori_loop(0, nc // 2, functools.partial(phase_b, second_half=False), (zero, zero))
    lax.fori_loop(nc // 2, nc, functools.partial(phase_b, second_half=True), mid)


def _deltanet(qkvd, zd, gates, gt, conv_w, norm_g):
    b, s, _ = qkvd.shape
    hd = DN_HEAD_DIM
    nc = s // DN_CHUNK
    n_lvl = 1
    while DN_BASE << n_lvl < DN_CHUNK:
        n_lvl += 1
    n_masks = 2 * (2 + n_lvl)
    col = lambda off: pl.BlockSpec((1, s, hd), lambda bi, h: (bi, 0, off + h))
    cw = lambda off: pl.BlockSpec((3, hd), lambda bi, h: (0, off + h))
    return pl.pallas_call(
        functools.partial(_dn_kernel, seq=s),
        grid=(b, DN_HEADS),
        in_specs=[col(0), col(DN_HEADS), col(2 * DN_HEADS), col(0),
                  cw(0), cw(DN_HEADS), cw(2 * DN_HEADS),
                  pl.BlockSpec((1, s, LANES), lambda bi, h: (bi, 0, 0)),
                  pl.BlockSpec((1, nc, 2 * DN_HEADS, DN_CHUNK), lambda bi, h: (bi, 0, 0, 0)),
                  pl.BlockSpec((1, hd), lambda bi, h: (0, 0))],
        out_specs=pl.BlockSpec((1, s, hd), lambda bi, h: (bi, 0, h)),
        out_shape=jax.ShapeDtypeStruct((b, s, DN_W), BF16),
        scratch_shapes=[pltpu.VMEM((s, hd), F32),
                        pltpu.VMEM((3, DN_GROUP * DN_CHUNK, hd), F32),
                        pltpu.VMEM((3, DN_GROUP * DN_CHUNK, hd), F32),
                        pltpu.VMEM((n_masks, DN_CHUNK, DN_CHUNK), F32),
                        pltpu.VMEM((2, s, hd), F32),
                        pltpu.VMEM((2, nc, 2 * DN_CHUNK, hd), BF16),
                        pltpu.VMEM((2, s, hd), BF16),
                        pltpu.VMEM((2, nc, DN_CHUNK, DN_CHUNK), BF16),
                        pltpu.VMEM((2, max(nc, 8), hd), F32)],
        compiler_params=_cparams(),
        name="deltanet",
    )(qkvd, qkvd, qkvd, zd, conv_w, conv_w, conv_w, gates, gt, norm_g)


def _merge_kernel(o1_ref, o2_ref, o3_ref, l1_ref, l2_ref, l3_ref, yd_ref, h1_ref, x_ref, mod_ref,
                  wg_ref, wba_ref, wbd_ref, wo_ref, g2_ref, x1_ref, h2_ref):
    l1, l2, l3 = l1_ref[0], l2_ref[0], l3_ref[0]
    mx = jnp.maximum(l1, jnp.maximum(l2, l3))
    e1, e2, e3 = jnp.exp(l1 - mx), jnp.exp(l2 - mx), jnp.exp(l3 - mx)
    ya = (e1 * o1_ref[0] + e2 * o2_ref[0] + e3 * o3_ref[0]) / (e1 + e2 + e3)
    d = D_MODEL
    h1 = h1_ref[0]
    gate_a = _sigmoid(jnp.dot(h1, wg_ref[:, 0:d], preferred_element_type=F32))
    gate_d = _sigmoid(jnp.dot(h1, wg_ref[:, d:2 * d], preferred_element_type=F32))
    merged = gate_a * _dot(ya, wba_ref[...]) + gate_d * _dot(yd_ref[0], wbd_ref[...])
    x1 = x_ref[0] + mod_ref[0, 2:3, :] * _dot(merged, wo_ref[...])
    x1_ref[0] = x1
    h2 = _rms(x1, g2_ref[...]) * (1.0 + mod_ref[0, 4:5, :]) + mod_ref[0, 3:4, :]
    h2_ref[0] = h2.astype(h2_ref.dtype)


def _merge_out(os_, ls_, yd, h1, x, mod, wg, wba, wbd, wo, g2):
    b, s, d = x.shape
    tm = 256
    row = lambda w: pl.BlockSpec((1, tm, w), lambda bi, i: (bi, i, 0))
    full = lambda a: pl.BlockSpec(a.shape, lambda bi, i: (0,) * a.ndim)
    return pl.pallas_call(
        _merge_kernel,
        grid=(b, s // tm),
        in_specs=[row(GROUP_W)] * 6 + [row(DN_W), row(d), row(d),
                                       pl.BlockSpec((1, 6, d), lambda bi, i: (bi, 0, 0)),
                                       full(wg), full(wba), full(wbd), full(wo), full(g2)],
        out_specs=[row(d), row(d)],
        out_shape=[jax.ShapeDtypeStruct((b, s, d), F32), jax.ShapeDtypeStruct((b, s, d), BF16)],
        compiler_params=_cparams(),
        name="merge_out",
    )(*os_, *ls_, yd, h1, x, mod, wg, wba, wbd, wo, g2)


FFN_HALO = 16
FFN_SUB = 512


def _ffn_up_kernel(hp_ref, hm_ref, hn_ref, wv_ref, wg_ref, cwv_ref, cwg_ref, bv_ref, bg_ref,
                   o_ref, *, nt):
    i = pl.program_id(1)
    tm = hm_ref.shape[1]
    lhs_all = jnp.concatenate([hp_ref[0], hm_ref[0], hn_ref[0]], axis=0)
    sub = FFN_SUB
    n = sub + 2 * FFN_HALO
    for s0 in range(0, tm, sub):
        lhs = lhs_all[s0:s0 + n]
        keep_prev = jnp.where(i > 0, 1.0, 0.0) if s0 == 0 else None
        keep_next = jnp.where(i < nt - 1, 1.0, 0.0) if s0 + sub == tm else None
        ups = [jnp.dot(lhs, w_ref[...], preferred_element_type=F32) for w_ref in (wv_ref, wg_ref)]

        def conv(up, cw_ref, b_ref):
            head, tail = up[0:FFN_HALO], up[FFN_HALO + sub:]
            if keep_prev is not None:
                head = head * keep_prev
            if keep_next is not None:
                tail = tail * keep_next
            up = jnp.concatenate([head, up[FFN_HALO:FFN_HALO + sub], tail], axis=0)
            cw = cw_ref[...]
            y = (pltpu.roll(up, 1, 0) * cw[0:1, :] + up * cw[1:2, :]
                 + pltpu.roll(up, n - 1, 0) * cw[2:3, :])
            return y[FFN_HALO:FFN_HALO + sub] + b_ref[...]

        val = conv(ups[0], cwv_ref, bv_ref)
        gate = conv(ups[1], cwg_ref, bg_ref)
        o_ref[0, s0:s0 + sub, :] = (gate * _sigmoid(gate) * val).astype(o_ref.dtype)


def _ffn_up(h2, w_up, conv_w, conv_b):
    b, s, d = h2.shape
    tm, tn = 2048, 256
    nt = s // tm
    nj = D_FF // tn
    hb = tm // FFN_HALO
    return pl.pallas_call(
        functools.partial(_ffn_up_kernel, nt=nt),
        grid=(b, nt, nj),
        in_specs=[pl.BlockSpec((1, FFN_HALO, d), lambda bi, i, j: (bi, jnp.maximum(i * hb - 1, 0), 0)),
                  pl.BlockSpec((1, tm, d), lambda bi, i, j: (bi, i, 0)),
                  pl.BlockSpec((1, FFN_HALO, d),
                               lambda bi, i, j: (bi, jnp.minimum((i + 1) * hb, s // FFN_HALO - 1), 0)),
                  pl.BlockSpec((d, tn), lambda bi, i, j: (0, j)),
                  pl.BlockSpec((d, tn), lambda bi, i, j: (0, j + nj)),
                  pl.BlockSpec((3, tn), lambda bi, i, j: (0, j)),
                  pl.BlockSpec((3, tn), lambda bi, i, j: (0, j + nj)),
                  pl.BlockSpec((1, tn), lambda bi, i, j: (0, j)),
                  pl.BlockSpec((1, tn), lambda bi, i, j: (0, j + nj))],
        out_specs=pl.BlockSpec((1, tm, tn), lambda bi, i, j: (bi, i, j)),
        out_shape=jax.ShapeDtypeStruct((b, s, D_FF), BF16),
        compiler_params=_cparams(),
        name="ffn_up",
    )(h2, h2, h2, w_up, w_up, conv_w, conv_w, conv_b, conv_b)


def _ffn_down_kernel(a_ref, w_ref, x_ref, mod_ref, g_ref, o_ref):
    x2 = x_ref[0] + mod_ref[0, 5:6, :] * jnp.dot(a_ref[0], w_ref[...], preferred_element_type=F32)
    o_ref[0] = _rms(x2, g_ref[...])


def _ffn_down(act, w_down, x1, mod, g):
    b, s, d = x1.shape
    tm = 512
    return pl.pallas_call(
        _ffn_down_kernel,
        grid=(b, s // tm),
        in_specs=[pl.BlockSpec((1, tm, D_FF), lambda bi, i: (bi, i, 0)),
                  pl.BlockSpec((D_FF, d), lambda bi, i: (0, 0)),
                  pl.BlockSpec((1, tm, d), lambda bi, i: (bi, i, 0)),
                  pl.BlockSpec((1, 6, d), lambda bi, i: (bi, 0, 0)),
                  pl.BlockSpec((1, d), lambda bi, i: (0, 0))],
        out_specs=pl.BlockSpec((1, tm, d), lambda bi, i: (bi, i, 0)),
        out_shape=jax.ShapeDtypeStruct((b, s, d), F32),
        compiler_params=_cparams(),
        name="ffn_down",
    )(act, w_down, x1, mod, g)


def _qk_column_order():
    half = HEAD_DIM // 2
    return [(pair * 2 + j) * HEAD_DIM + f * half + i
            for pair in range(GROUP_W // LANES) for f in range(2) for j in range(2)
            for i in range(half)]


_QK_COLS = np.asarray(_qk_column_order(), np.int32)


def _trunk(x, mod, p, rope):
    b, s, _ = x.shape
    h1 = _norm_mod(x, mod, p["norm1_g"])
    qkvd = _mm(h1, p["w_qkvd"], BF16)
    zd = _mm(h1, p["w_zd"], BF16)
    gates, gt = _gates(h1, p["w_ab"], p["a_row"], p["dt_row"])
    os_, ls_ = [], []
    for gi, (_, dil) in enumerate(ATTN_GROUPS):
        o, l = _attn(_proj_attn(h1, p["w_attn"][gi], rope, dil, gi), gi, dil)
        os_.append(o)
        ls_.append(l)
    yd = _deltanet(qkvd, zd, gates, gt, p["conv_qkv_w"], p["dn_norm_g"])
    x1, h2 = _merge_out(os_, ls_, yd, h1, x, mod, p["w_gate"], p["w_br_attn"], p["w_br_dn"],
                        p["w_out"], p["norm2_g"])
    act = _ffn_up(h2, p["w_up"], p["ffn_conv_w"], p["ffn_conv_b"])
    return _ffn_down(act, p["w_down"], x1, mod, p["norm_f_g"])


def kernel(x_prompt, x_sample, c_prompt, c_sample, w_ada, b_ada, norm1_g, w_in, conv_qkv_w, a_log, dt_bias, dn_norm_g, w_br_attn, w_br_dn, w_out, norm2_g, w_up, ffn_conv_w, ffn_conv_b, w_down, norm_f_g):
    d = D_MODEL
    assert w_ada.shape[0] == 1, "single layer"
    w = w_in[0]
    o_qd = 3 * ATTN_W
    o_zd = o_qd + 3 * DN_W
    o_ab = o_zd + DN_W
    o_gate = o_ab + 4 * DN_HEADS
    pad16 = lambda v: jnp.pad(v.reshape(1, 2 * DN_HEADS).astype(F32), ((0, 0), (0, LANES - 2 * DN_HEADS)))
    p = {
        "norm1_g": norm1_g[0].reshape(1, d),
        "w_attn": [jnp.concatenate(
            [w[:, kind * ATTN_W + gi * GROUP_W:kind * ATTN_W + (gi + 1) * GROUP_W][:, cols]
             for kind, cols in ((0, _QK_COLS), (1, _QK_COLS), (2, slice(None)))], axis=1).astype(BF16)
            for gi in range(N_GROUPS)],
        "w_qkvd": w[:, o_qd:o_zd].astype(BF16),
        "w_zd": w[:, o_zd:o_ab].astype(BF16),
        "w_ab": jnp.pad(w[:, o_ab:o_gate], ((0, 0), (0, LANES - 4 * DN_HEADS))).astype(BF16),
        "w_gate": w[:, o_gate:].astype(BF16),
        "a_row": pad16(a_log[0]),
        "dt_row": pad16(dt_bias[0]),
        "conv_qkv_w": conv_qkv_w[0],
        "dn_norm_g": dn_norm_g[0].reshape(1, DN_HEAD_DIM),
        "w_br_attn": w_br_attn[0].astype(BF16),
        "w_br_dn": w_br_dn[0].astype(BF16),
        "w_out": w_out[0].astype(BF16),
        "norm2_g": norm2_g[0].reshape(1, d),
        "w_up": w_up[0].astype(BF16),
        "ffn_conv_w": ffn_conv_w[0],
        "ffn_conv_b": ffn_conv_b[0].reshape(1, 2 * D_FF),
        "w_down": w_down[0].astype(BF16),
        "norm_f_g": norm_f_g.reshape(1, d),
    }
    nb = x_prompt.shape[0]
    mod = _ada(jnp.concatenate([c_prompt, c_sample], axis=0), w_ada[0], b_ada[0].reshape(1, 6 * d))
    mod = mod.reshape(-1, 6, d)
    rope = _rope_tables(max(x_prompt.shape[1], x_sample.shape[1]))
    y_prompt = _trunk(x_prompt, mod[:nb], p, rope)
    y_sample = _trunk(x_sample, mod[nb:], p, rope)
    return (y_prompt, y_sample)
```

```python
import functools

import jax
import jax.numpy as jnp
import numpy as np
from jax import lax
from jax.experimental import pallas as pl
from jax.experimental.pallas import tpu as pltpu

F32 = jnp.float32
BF16 = jnp.bfloat16
HIGHEST = lax.Precision.HIGHEST

D_MODEL = 1024
ATTN_GROUPS = ((128, 1), (512, 4), (2048, 16))
N_GROUPS = 3
HEAD_DIM = 64
GROUP_W = 512
ATTN_W = N_GROUPS * GROUP_W
ATTN_HALF = 64
ATTN_BQ = 128
ATTN_MAX_BLOCKS = 4
ATTN_MAX_ROWS = 2048
ROPE_THETA = 10000.0
NEG = -1e30
DN_HEADS = 8
DN_HEAD_DIM = 128
DN_W = DN_HEADS * DN_HEAD_DIM
DN_CHUNK = 256
DN_BASE = 16
DN_GROUP = 2
DN_HALO = 16
MM_SUB = 256
D_FF = 2816
EPS = 1e-6
LANES = 128
VMEM_LIMIT = 56 * 1024 * 1024


def _cparams(**kw):
    return pltpu.CompilerParams(vmem_limit_bytes=VMEM_LIMIT, **kw)


def _sigmoid(x):
    return 1.0 / (1.0 + jnp.exp(-x))


def _dot(a, b):
    return jnp.dot(a.astype(BF16), b.astype(BF16), preferred_element_type=F32)


def _dot_nt(a, b):
    return lax.dot_general(a.astype(BF16), b.astype(BF16), (((1,), (1,)), ((), ())),
                           preferred_element_type=F32)


def _dot_tn(a, b):
    return lax.dot_general(a.astype(BF16), b.astype(BF16), (((0,), (0,)), ((), ())),
                           preferred_element_type=F32)


def _rms(x, g):
    return x * lax.rsqrt(jnp.mean(x * x, axis=-1, keepdims=True) + EPS) * g


def _ada_kernel(c_ref, w_ref, b_ref, o_ref):
    c = c_ref[...]
    s = c * _sigmoid(c)
    o_ref[...] = jnp.dot(s, w_ref[...], preferred_element_type=F32, precision=HIGHEST) + b_ref[...]


def _ada(c, w, b):
    bt, d = c.shape
    n = w.shape[1]
    tn = 1024
    return pl.pallas_call(
        _ada_kernel,
        grid=(n // tn,),
        in_specs=[pl.BlockSpec((bt, d), lambda j: (0, 0)),
                  pl.BlockSpec((d, tn), lambda j: (0, j)),
                  pl.BlockSpec((1, tn), lambda j: (0, j))],
        out_specs=pl.BlockSpec((bt, tn), lambda j: (0, j)),
        out_shape=jax.ShapeDtypeStruct((bt, n), F32),
        name="ada",
    )(c, w, b)


def _rope_table_kernel(inv_ref, cos_ref, sin_ref):
    tm = cos_ref.shape[0]
    pos = (pl.program_id(0) * tm + lax.broadcasted_iota(jnp.int32, (tm, LANES), 0)).astype(F32)
    ang = pos * inv_ref[...]
    lane = lax.broadcasted_iota(jnp.int32, (tm, LANES), 1)
    cos_ref[...] = jnp.cos(ang)
    sin_ref[...] = jnp.sin(ang) * jnp.where(lane < LANES // 2, -1.0, 1.0)


def _rope_tables(s):
    half = HEAD_DIM // 2
    inv = ROPE_THETA ** (-(jnp.arange(LANES) % half).astype(F32) / half)
    tm = 512
    return pl.pallas_call(
        _rope_table_kernel,
        grid=(s // tm,),
        in_specs=[pl.BlockSpec((1, LANES), lambda i: (0, 0))],
        out_specs=[pl.BlockSpec((tm, LANES), lambda i: (i, 0))] * 2,
        out_shape=[jax.ShapeDtypeStruct((s, LANES), F32)] * 2,
        name="rope_tables",
    )(inv.reshape(1, LANES))


def _norm_mod_kernel(x_ref, mod_ref, g_ref, o_ref):
    h = _rms(x_ref[0], g_ref[...]) * (1.0 + mod_ref[0, 1:2, :]) + mod_ref[0, 0:1, :]
    o_ref[0] = h.astype(o_ref.dtype)


def _norm_mod(x, mod, g):
    b, s, d = x.shape
    tm = 512
    return pl.pallas_call(
        _norm_mod_kernel,
        grid=(b, s // tm),
        in_specs=[pl.BlockSpec((1, tm, d), lambda bi, i: (bi, i, 0)),
                  pl.BlockSpec((1, 6, d), lambda bi, i: (bi, 0, 0)),
                  pl.BlockSpec((1, d), lambda bi, i: (0, 0))],
        out_specs=pl.BlockSpec((1, tm, d), lambda bi, i: (bi, i, 0)),
        out_shape=jax.ShapeDtypeStruct((b, s, d), BF16),
        name="norm_mod",
    )(x, mod, g)


def _mm_kernel(h_ref, w_ref, o_ref):
    o_ref[0] = jnp.dot(h_ref[0], w_ref[...], preferred_element_type=F32).astype(o_ref.dtype)


def _mm(h, w, out_dtype):
    b, s, k = h.shape
    n = w.shape[1]
    tm, tn = 1024, 1024
    return pl.pallas_call(
        _mm_kernel,
        grid=(b, s // tm, n // tn),
        in_specs=[pl.BlockSpec((1, tm, k), lambda bi, i, j: (bi, i, 0)),
                  pl.BlockSpec((k, tn), lambda bi, i, j: (0, j))],
        out_specs=pl.BlockSpec((1, tm, tn), lambda bi, i, j: (bi, i, j)),
        out_shape=jax.ShapeDtypeStruct((b, s, n), out_dtype),
        compiler_params=_cparams(),
        name="proj",
    )(h, w)


def _proj_attn_kernel(h_ref, w_ref, cos_ref, sin_ref, o_ref, *scratch, dil):
    is_qk = pl.program_id(2) < 2
    tm = h_ref.shape[1]
    n_slab = w_ref.shape[1] // LANES
    for s0 in range(0, tm, MM_SUB):
        rows = slice(s0, s0 + MM_SUB)
        acc = jnp.dot(h_ref[0, rows, :], w_ref[...], preferred_element_type=F32)
        cos = jnp.where(is_qk, cos_ref[rows, :], 1.0)
        sin = jnp.where(is_qk, sin_ref[rows, :], 0.0)
        for c in range(n_slab):
            t = acc[:, c * LANES:(c + 1) * LANES]
            val = t * cos + pltpu.roll(t, LANES // 2, 1) * sin
            if dil == 1:
                o_ref[0, 0, 0, rows, c * LANES:(c + 1) * LANES] = val.astype(o_ref.dtype)
            else:
                scratch[0][c, rows, :] = val
        if dil > 1:
            n = MM_SUB // dil
            m0 = s0 // dil
            for r in range(dil):
                for c in range(n_slab):
                    o_ref[0, 0, r, m0:m0 + n, c * LANES:(c + 1) * LANES] = (
                        scratch[0][c, pl.ds(s0 + r, n, stride=dil), :].astype(o_ref.dtype))


def _proj_attn(h, w, rope, dil, gi):
    b, s, k = h.shape
    tm, tn = 2048, GROUP_W
    scratch = [pltpu.VMEM((tn // LANES, tm, LANES), F32)] if dil > 1 else []
    return pl.pallas_call(
        functools.partial(_proj_attn_kernel, dil=dil),
        grid=(b, s // tm, 3),
        in_specs=[pl.BlockSpec((1, tm, k), lambda bi, i, j: (bi, i, 0)),
                  pl.BlockSpec((k, tn), lambda bi, i, j: (0, j)),
                  pl.BlockSpec((tm, LANES), lambda bi, i, j: (i, 0)),
                  pl.BlockSpec((tm, LANES), lambda bi, i, j: (i, 0))],
        out_specs=pl.BlockSpec((1, 1, dil, tm // dil, tn), lambda bi, i, j: (bi, j, 0, i, 0)),
        out_shape=jax.ShapeDtypeStruct((b, 3, dil, s // dil, tn), BF16),
        scratch_shapes=scratch,
        compiler_params=_cparams(),
        name=f"proj_attn_g{gi}",
    )(h, w, *rope)


def _gates_kernel(h_ref, w_ref, a_ref, dt_ref, o_ref, gt_ref):
    acc = jnp.dot(h_ref[0], w_ref[...], preferred_element_type=F32)
    c = acc.shape[0]
    x = acc + dt_ref[...]
    softplus = jnp.maximum(x, 0.0) + jnp.log1p(jnp.exp(-jnp.abs(x)))
    g = -jnp.exp(a_ref[...]) * softplus
    beta = _sigmoid(acc)
    r = lax.broadcasted_iota(jnp.int32, (c, c), 0)
    cc = lax.broadcasted_iota(jnp.int32, (c, c), 1)
    pre = jnp.dot(jnp.where(cc <= r, 1.0, 0.0).astype(F32), g, preferred_element_type=F32,
                  precision=HIGHEST)
    suf = jnp.dot(jnp.where(cc >= r, 1.0, 0.0).astype(F32), g, preferred_element_type=F32,
                  precision=HIGHEST)
    lane = lax.broadcasted_iota(jnp.int32, acc.shape, 1)
    out = jnp.where(lane < DN_HEADS, pre,
                    jnp.where(lane < 2 * DN_HEADS, suf,
                              jnp.where(lane < 4 * DN_HEADS, beta, 0.0)))
    o_ref[0] = out
    gt_ref[0, 0] = out.T[0:2 * DN_HEADS, :]


def _gates(h, w, a_row, dt_row):
    b, s, k = h.shape
    tm = DN_CHUNK
    return pl.pallas_call(
        _gates_kernel,
        grid=(b, s // tm),
        in_specs=[pl.BlockSpec((1, tm, k), lambda bi, i: (bi, i, 0)),
                  pl.BlockSpec((k, LANES), lambda bi, i: (0, 0)),
                  pl.BlockSpec((1, LANES), lambda bi, i: (0, 0)),
                  pl.BlockSpec((1, LANES), lambda bi, i: (0, 0))],
        out_specs=[pl.BlockSpec((1, tm, LANES), lambda bi, i: (bi, i, 0)),
                   pl.BlockSpec((1, 1, 2 * DN_HEADS, tm), lambda bi, i: (bi, i, 0, 0))],
        out_shape=[jax.ShapeDtypeStruct((b, s, LANES), F32),
                   jax.ShapeDtypeStruct((b, s // tm, 2 * DN_HEADS, tm), F32)],
        name="dn_gates",
    )(h, w, a_row, dt_row)


def _attn_kernel(q_ref, k_ref, v_ref, o_ref, l_ref, *scratch, seq, bq, nq, kb, dil):
    n_slab = GROUP_W // LANES
    lane = lax.broadcasted_iota(jnp.int32, (bq, LANES), 1)
    blocks = []
    for qb in range(nq):
        row0 = qb * bq
        q0 = (pl.program_id(1) * nq + qb) * bq
        ks = pl.multiple_of(jnp.clip(q0 - ATTN_HALF, 0, seq - kb), ATTN_HALF)
        qpos = q0 + lax.broadcasted_iota(jnp.int32, (bq, kb), 0)
        kpos = ks + lax.broadcasted_iota(jnp.int32, (bq, kb), 1)
        valid = jnp.abs(qpos - kpos) <= ATTN_HALF

        def residue(r, row0=row0, ks=ks, valid=valid):
            q = q_ref[0, 0, r, row0:row0 + bq, :].astype(F32) * (HEAD_DIM ** -0.5)
            k = k_ref[0, 0, r, pl.ds(ks, kb), :]
            v = v_ref[0, 0, r, pl.ds(ks, kb), :]
            heads = [(hp, sub) for hp in range(n_slab) for sub in range(2)]
            in_head = [lane < HEAD_DIM, lane >= HEAD_DIM]
            qk_lanes = [(lane & (HEAD_DIM // 2)) == 0, (lane & (HEAD_DIM // 2)) != 0]
            slab = lambda x, hp: x[:, hp * LANES:(hp + 1) * LANES]
            s_ = [_dot_nt(jnp.where(qk_lanes[sub], slab(q, hp), 0.0), slab(k, hp)) for hp, sub in heads]
            s_ = [jnp.where(valid, s, NEG) for s in s_]
            mx_ = [jnp.max(s, axis=-1, keepdims=True) for s in s_]
            p_ = [jnp.exp(s - mx) for s, mx in zip(s_, mx_)]
            den_ = [jnp.sum(p, axis=-1, keepdims=True) for p in p_]
            o_ = [_dot(p, slab(v, hp)) / den for p, den, (hp, _) in zip(p_, den_, heads)]
            lse_ = [mx + jnp.log(den) for mx, den in zip(mx_, den_)]
            for hp in range(n_slab):
                sl = slice(hp * LANES, (hp + 1) * LANES)
                o_pair = jnp.where(in_head[1], o_[2 * hp + 1], o_[2 * hp])
                l_pair = jnp.where(in_head[1], lse_[2 * hp + 1], lse_[2 * hp])
                if dil == 1:
                    o_ref[0, row0:row0 + bq, sl] = o_pair.astype(o_ref.dtype)
                    l_ref[0, row0:row0 + bq, sl] = l_pair
                else:
                    scratch[0][hp, pl.ds(row0 * dil + r, bq, stride=dil), :] = o_pair
                    scratch[1][hp, pl.ds(row0 * dil + r, bq, stride=dil), :] = l_pair

        blocks.append(residue)

    if dil == 1:
        for block in blocks:
            block(0)
    else:
        def body(r, carry):
            for block in blocks:
                block(r)
            return carry
        lax.fori_loop(0, dil, body, 0, unroll=max(1, ATTN_MAX_BLOCKS // nq))
    if dil > 1:
        for hp in range(n_slab):
            sl = slice(hp * LANES, (hp + 1) * LANES)
            o_ref[0, :, sl] = scratch[0][hp].astype(o_ref.dtype)
            l_ref[0, :, sl] = scratch[1][hp]


def _attn(qkv, gi, dil):
    b, _, _, seq, _ = qkv.shape
    s = seq * dil
    bq = min(ATTN_BQ, seq)
    kb = min(bq + 2 * ATTN_HALF, seq)
    nq = 1
    while 2 * nq <= ATTN_MAX_BLOCKS and 2 * nq * bq * dil <= ATTN_MAX_ROWS and seq % (2 * nq * bq) == 0:
        nq *= 2
    rows = nq * bq * dil
    scratch = [pltpu.VMEM((GROUP_W // LANES, rows, LANES), F32)] * 2 if dil > 1 else []
    kv = lambda kind: pl.BlockSpec((1, 1, dil, seq, GROUP_W), lambda bi, m: (bi, kind, 0, 0, 0))
    return pl.pallas_call(
        functools.partial(_attn_kernel, seq=seq, bq=bq, nq=nq, kb=kb, dil=dil),
        grid=(b, seq // (nq * bq)),
        in_specs=[pl.BlockSpec((1, 1, dil, nq * bq, GROUP_W), lambda bi, m: (bi, 0, 0, m, 0)),
                  kv(1), kv(2)],
        out_specs=[pl.BlockSpec((1, rows, GROUP_W), lambda bi, m: (bi, m, 0))] * 2,
        out_shape=[jax.ShapeDtypeStruct((b, s, GROUP_W), BF16),
                   jax.ShapeDtypeStruct((b, s, GROUP_W), F32)],
        scratch_shapes=scratch,
        compiler_params=_cparams(),
        name=f"attn_g{gi}",
    )(qkv, qkv, qkv)


def _tri_inverse(a_list, masks_ref, bases, uppers, eye):
    ps = [-a * masks_ref[b] for a, b in zip(a_list, bases)]
    ts = [eye + x for x in ps]
    n = 2
    while n < DN_BASE:
        ps = [_dot(p, p).astype(BF16) for p in ps]
        ts = [t + _dot(t, p) for t, p in zip(ts, ps)]
        n *= 2
    n = DN_BASE
    lvl = 1
    while 2 * n < DN_CHUNK:
        left = [_dot(t, a * masks_ref[b + lvl]) for t, a, b in zip(ts, a_list, bases)]
        ts = [t - _dot(l, t) for t, l in zip(ts, left)]
        n *= 2
        lvl += 1
    t11 = [t[0:n, 0:n] for t in ts]
    t22 = [t[n:, n:] for t in ts]
    zero = jnp.zeros((n, n), F32)
    left = [_dot(x11, a[0:n, n:]) if up else _dot(x22, a[n:, 0:n])
            for x11, x22, a, up in zip(t11, t22, a_list, uppers)]
    off = [-_dot(l, x22) if up else -_dot(l, x11) for l, x11, x22, up in zip(left, t11, t22, uppers)]
    return [jnp.concatenate([jnp.concatenate([x11, o if up else zero], axis=1),
                             jnp.concatenate([zero if up else o, x22], axis=1)], axis=0)
            for x11, x22, o, up in zip(t11, t22, off, uppers)]


def _dn_kernel(q_ref, k_ref, v_ref, z_ref, cwq_ref, cwk_ref, cwv_ref, gates_ref, gt_ref, ng_ref,
               o_ref, acc_ref, qkv_a_ref, qkv_b_ref, masks_ref, u_ref, wq_ref, kd_ref, p_ref,
               gl_ref, *, seq):
    h = pl.program_id(1)
    c = DN_CHUNK
    nc = seq // c
    n_lvl = 1
    while DN_BASE << n_lvl < c:
        n_lvl += 1

    r = lax.broadcasted_iota(jnp.int32, (c, c), 0)
    cc = lax.broadcasted_iota(jnp.int32, (c, c), 1)
    as_f32 = lambda m: jnp.where(m, 1.0, 0.0).astype(F32)
    eye = as_f32(r == cc)
    per_dir = 2 + n_lvl
    base_shift = DN_BASE.bit_length() - 1
    for d, (lo, hi) in enumerate(((cc, r), (r, cc))):
        masks_ref[d * per_dir + 0] = as_f32(lo <= hi)
        masks_ref[d * per_dir + 1] = as_f32(lo < hi)
        masks_ref[d * per_dir + 2] = as_f32((lo < hi) & ((lo >> base_shift) == (hi >> base_shift)))
        for lvl in range(1, n_lvl):
            sh = base_shift + lvl - 1
            lo_blk, hi_blk = lo >> sh, hi >> sh
            masks_ref[d * per_dir + 2 + lvl] = as_f32((hi_blk == lo_blk + 1) & ((lo_blk & 1) == 0))

    n_groups = nc // DN_GROUP

    def prep_group(g, dst_ref):
        halo = DN_HALO
        n = c + 2 * halo
        for j in range(DN_GROUP):
            ci = DN_GROUP * g + j
            r0 = pl.multiple_of(ci * c, c)
            lo = pl.multiple_of(jnp.maximum(r0 - halo, 0), halo)
            hi = pl.multiple_of(jnp.minimum(r0 + c, seq - halo), halo)
            keep_lo = jnp.where(ci > 0, 1.0, 0.0)
            keep_hi = jnp.where(ci < nc - 1, 1.0, 0.0)
            for idx, (x_ref, w_ref, kind) in enumerate(((q_ref, cwq_ref, "q"), (k_ref, cwk_ref, "k"),
                                                        (v_ref, cwv_ref, "v"))):
                x = jnp.concatenate([x_ref[0, pl.ds(lo, halo), :].astype(F32) * keep_lo,
                                     x_ref[0, pl.ds(r0, c), :].astype(F32),
                                     x_ref[0, pl.ds(hi, halo), :].astype(F32) * keep_hi], axis=0)
                w = w_ref[...]
                y = (pltpu.roll(x, 1, 0) * w[0:1, :] + x * w[1:2, :]
                     + pltpu.roll(x, n - 1, 0) * w[2:3, :])[halo:halo + c]
                y = y * _sigmoid(y)
                if kind != "v":
                    y = y * lax.rsqrt(jnp.sum(y * y, axis=-1, keepdims=True) + EPS)
                if kind == "q":
                    y = y * (DN_HEAD_DIM ** -0.5)
                dst_ref[idx, j * c:(j + 1) * c, :] = y

    prep_group(jnp.int32(0), qkv_a_ref)

    lane = lax.broadcasted_iota(jnp.int32, (c, LANES), 1)

    def solve_group(g, src_ref):
        sys_, a_list, bases = [], [], []
        for j in range(DN_GROUP):
            ci = DN_GROUP * g + j
            r0 = pl.multiple_of(ci * c, c)
            q = src_ref[0, j * c:(j + 1) * c, :]
            k = src_ref[1, j * c:(j + 1) * c, :]
            v = src_ref[2, j * c:(j + 1) * c, :]
            gch = gates_ref[0, pl.ds(r0, c), :]
            kk = _dot_nt(k, k)
            qk = _dot_nt(q, k)
            for d in range(2):
                col = lambda j: jnp.sum(jnp.where(lane == j, gch, 0.0), axis=1, keepdims=True)
                gc = col(d * DN_HEADS + h)
                beta = col((2 + d) * DN_HEADS + h)
                grow = gt_ref[0, ci, pl.ds(d * DN_HEADS + h, 1), :]
                g_end = gc[0:1, :] if d else gc[c - 1:c, :]
                dec = jnp.exp(jnp.minimum(gc - grow, 0.0))
                e_g = jnp.exp(gc)
                a_list.append(kk * (beta * dec) * masks_ref[d * per_dir + 1])
                bases.append(d * per_dir + 2)
                sys_.append(dict(
                    ci=ci, r0=r0, d=d,
                    rhs=jnp.concatenate([v * beta, k * (beta * e_g)], axis=1),
                    qg=(q * e_g).astype(BF16),
                    kd=(k * jnp.exp(g_end - gc)).astype(BF16),
                    p=(qk * dec * masks_ref[d * per_dir]).astype(BF16),
                    gl=jnp.broadcast_to(jnp.exp(g_end), (1, LANES))))
        t_list = _tri_inverse(a_list, masks_ref, bases, [s["d"] == 1 for s in sys_], eye)
        uw_list = [_dot(t, s["rhs"]) for t, s in zip(t_list, sys_)]
        for s, uw in zip(sys_, uw_list):
            d, ci, r0 = s["d"], s["ci"], s["r0"]
            u_ref[d, pl.ds(r0, c), :] = uw[:, 0:LANES]
            wq_ref[d, ci, 0:c, :] = uw[:, LANES:2 * LANES].astype(BF16)
            wq_ref[d, ci, c:2 * c, :] = s["qg"]
            kd_ref[d, pl.ds(r0, c), :] = s["kd"]
            p_ref[d, ci] = s["p"]
            gl_ref[d, pl.ds(ci, 1), :] = s["gl"]

    def phase_a(m, carry):
        prep_group(2 * m + 1, qkv_b_ref)
        solve_group(2 * m, qkv_a_ref)
        prep_group(jnp.minimum(2 * m + 2, n_groups - 1), qkv_a_ref)
        solve_group(2 * m + 1, qkv_b_ref)
        return carry

    lax.fori_loop(0, n_groups // 2, phase_a, 0)

    def phase_b(i, carry, second_half):
        cis = (i, nc - 1 - i)
        r0s = [pl.multiple_of(ci * c, c) for ci in cis]
        ws_qs = [jnp.dot(wq_ref[d, cis[d]], carry[d].astype(BF16), preferred_element_type=F32)
                 for d in range(2)]
        v_new = [(u_ref[d, pl.ds(r0s[d], c), :] - ws_qs[d][0:c]).astype(BF16) for d in range(2)]
        outs = [ws_qs[d][c:2 * c] + jnp.dot(p_ref[d, cis[d]], v_new[d], preferred_element_type=F32)
                for d in range(2)]
        new = [carry[d] * gl_ref[d, pl.ds(cis[d], 1), :]
               + _dot_tn(kd_ref[d, pl.ds(r0s[d], c), :], v_new[d]) for d in range(2)]
        for d in range(2):
            rows = pl.ds(r0s[d], c)
            if not second_half:
                acc_ref[rows, :] = outs[d]
            else:
                z = z_ref[0, rows, :].astype(F32)
                y = _rms(acc_ref[rows, :] + outs[d], ng_ref[...]) * (z * _sigmoid(z))
                o_ref[0, rows, :] = y.astype(o_ref.dtype)
        return tuple(new)

    zero = jnp.zeros((DN_HEAD_DIM, DN_HEAD_DIM), F32)
    mid = lax.fori_loop(0, nc // 2, functools.partial(phase_b, second_half=False), (zero, zero))
    lax.fori_loop(nc // 2, nc, functools.partial(phase_b, second_half=True), mid)


def _deltanet(qkvd, zd, gates, gt, conv_w, norm_g):
    b, s, _ = qkvd.shape
    hd = DN_HEAD_DIM
    nc = s // DN_CHUNK
    n_lvl = 1
    while DN_BASE << n_lvl < DN_CHUNK:
        n_lvl += 1
    n_masks = 2 * (2 + n_lvl)
    col = lambda off: pl.BlockSpec((1, s, hd), lambda bi, h: (bi, 0, off + h))
    cw = lambda off: pl.BlockSpec((3, hd), lambda bi, h: (0, off + h))
    return pl.pallas_call(
        functools.partial(_dn_kernel, seq=s),
        grid=(b, DN_HEADS),
        in_specs=[col(0), col(DN_HEADS), col(2 * DN_HEADS), col(0),
                  cw(0), cw(DN_HEADS), cw(2 * DN_HEADS),
                  pl.BlockSpec((1, s, LANES), lambda bi, h: (bi, 0, 0)),
                  pl.BlockSpec((1, nc, 2 * DN_HEADS, DN_CHUNK), lambda bi, h: (bi, 0, 0, 0)),
                  pl.BlockSpec((1, hd), lambda bi, h: (0, 0))],
        out_specs=pl.BlockSpec((1, s, hd), lambda bi, h: (bi, 0, h)),
        out_shape=jax.ShapeDtypeStruct((b, s, DN_W), BF16),
        scratch_shapes=[pltpu.VMEM((s, hd), F32),
                        pltpu.VMEM((3, DN_GROUP * DN_CHUNK, hd), F32),
                        pltpu.VMEM((3, DN_GROUP * DN_CHUNK, hd), F32),
                        pltpu.VMEM((n_masks, DN_CHUNK, DN_CHUNK), F32),
                        pltpu.VMEM((2, s, hd), F32),
                        pltpu.VMEM((2, nc, 2 * DN_CHUNK, hd), BF16),
                        pltpu.VMEM((2, s, hd), BF16),
                        pltpu.VMEM((2, nc, DN_CHUNK, DN_CHUNK), BF16),
                        pltpu.VMEM((2, max(nc, 8), hd), F32)],
        compiler_params=_cparams(),
        name="deltanet",
    )(qkvd, qkvd, qkvd, zd, conv_w, conv_w, conv_w, gates, gt, norm_g)


def _merge_kernel(o1_ref, o2_ref, o3_ref, l1_ref, l2_ref, l3_ref, yd_ref, h1_ref, x_ref, mod_ref,
                  wg_ref, wba_ref, wbd_ref, wo_ref, g2_ref, x1_ref, h2_ref):
    l1, l2, l3 = l1_ref[0], l2_ref[0], l3_ref[0]
    mx = jnp.maximum(l1, jnp.maximum(l2, l3))
    e1, e2, e3 = jnp.exp(l1 - mx), jnp.exp(l2 - mx), jnp.exp(l3 - mx)
    ya = (e1 * o1_ref[0] + e2 * o2_ref[0] + e3 * o3_ref[0]) / (e1 + e2 + e3)
    d = D_MODEL
    h1 = h1_ref[0]
    gate_a = _sigmoid(jnp.dot(h1, wg_ref[:, 0:d], preferred_element_type=F32))
    gate_d = _sigmoid(jnp.dot(h1, wg_ref[:, d:2 * d], preferred_element_type=F32))
    merged = gate_a * _dot(ya, wba_ref[...]) + gate_d * _dot(yd_ref[0], wbd_ref[...])
    x1 = x_ref[0] + mod_ref[0, 2:3, :] * _dot(merged, wo_ref[...])
    x1_ref[0] = x1
    h2 = _rms(x1, g2_ref[...]) * (1.0 + mod_ref[0, 4:5, :]) + mod_ref[0, 3:4, :]
    h2_ref[0] = h2.astype(h2_ref.dtype)


def _merge_out(os_, ls_, yd, h1, x, mod, wg, wba, wbd, wo, g2):
    b, s, d = x.shape
    tm = 256
    row = lambda w: pl.BlockSpec((1, tm, w), lambda bi, i: (bi, i, 0))
    full = lambda a: pl.BlockSpec(a.shape, lambda bi, i: (0,) * a.ndim)
    return pl.pallas_call(
        _merge_kernel,
        grid=(b, s // tm),
        in_specs=[row(GROUP_W)] * 6 + [row(DN_W), row(d), row(d),
                                       pl.BlockSpec((1, 6, d), lambda bi, i: (bi, 0, 0)),
                                       full(wg), full(wba), full(wbd), full(wo), full(g2)],
        out_specs=[row(d), row(d)],
        out_shape=[jax.ShapeDtypeStruct((b, s, d), F32), jax.ShapeDtypeStruct((b, s, d), BF16)],
        compiler_params=_cparams(),
        name="merge_out",
    )(*os_, *ls_, yd, h1, x, mod, wg, wba, wbd, wo, g2)


FFN_HALO = 16
FFN_SUB = 512


def _ffn_up_kernel(hp_ref, hm_ref, hn_ref, wv_ref, wg_ref, cwv_ref, cwg_ref, bv_ref, bg_ref,
                   o_ref, *, nt):
    i = pl.program_id(1)
    tm = hm_ref.shape[1]
    lhs_all = jnp.concatenate([hp_ref[0], hm_ref[0], hn_ref[0]], axis=0)
    sub = FFN_SUB
    n = sub + 2 * FFN_HALO
    for s0 in range(0, tm, sub):
        lhs = lhs_all[s0:s0 + n]
        keep_prev = jnp.where(i > 0, 1.0, 0.0) if s0 == 0 else None
        keep_next = jnp.where(i < nt - 1, 1.0, 0.0) if s0 + sub == tm else None
        ups = [jnp.dot(lhs, w_ref[...], preferred_element_type=F32) for w_ref in (wv_ref, wg_ref)]

        def conv(up, cw_ref, b_ref):
            head, tail = up[0:FFN_HALO], up[FFN_HALO + sub:]
            if keep_prev is not None:
                head = head * keep_prev
            if keep_next is not None:
                tail = tail * keep_next
            up = jnp.concatenate([head, up[FFN_HALO:FFN_HALO + sub], tail], axis=0)
            cw = cw_ref[...]
            y = (pltpu.roll(up, 1, 0) * cw[0:1, :] + up * cw[1:2, :]
                 + pltpu.roll(up, n - 1, 0) * cw[2:3, :])
            return y[FFN_HALO:FFN_HALO + sub] + b_ref[...]

        val = conv(ups[0], cwv_ref, bv_ref)
        gate = conv(ups[1], cwg_ref, bg_ref)
        o_ref[0, s0:s0 + sub, :] = (gate * _sigmoid(gate) * val).astype(o_ref.dtype)


def _ffn_up(h2, w_up, conv_w, conv_b):
    b, s, d = h2.shape
    tm, tn = 2048, 256
    nt = s // tm
    nj = D_FF // tn
    hb = tm // FFN_HALO
    return pl.pallas_call(
        functools.partial(_ffn_up_kernel, nt=nt),
        grid=(b, nt, nj),
        in_specs=[pl.BlockSpec((1, FFN_HALO, d), lambda bi, i, j: (bi, jnp.maximum(i * hb - 1, 0), 0)),
                  pl.BlockSpec((1, tm, d), lambda bi, i, j: (bi, i, 0)),
                  pl.BlockSpec((1, FFN_HALO, d),
                               lambda bi, i, j: (bi, jnp.minimum((i + 1) * hb, s // FFN_HALO - 1), 0)),
                  pl.BlockSpec((d, tn), lambda bi, i, j: (0, j)),
                  pl.BlockSpec((d, tn), lambda bi, i, j: (0, j + nj)),
                  pl.BlockSpec((3, tn), lambda bi, i, j: (0, j)),
                  pl.BlockSpec((3, tn), lambda bi, i, j: (0, j + nj)),
                  pl.BlockSpec((1, tn), lambda bi, i, j: (0, j)),
                  pl.BlockSpec((1, tn), lambda bi, i, j: (0, j + nj))],
        out_specs=pl.BlockSpec((1, tm, tn), lambda bi, i, j: (bi, i, j)),
        out_shape=jax.ShapeDtypeStruct((b, s, D_FF), BF16),
        compiler_params=_cparams(),
        name="ffn_up",
    )(h2, h2, h2, w_up, w_up, conv_w, conv_w, conv_b, conv_b)


def _ffn_down_kernel(a_ref, w_ref, x_ref, mod_ref, g_ref, o_ref):
    x2 = x_ref[0] + mod_ref[0, 5:6, :] * jnp.dot(a_ref[0], w_ref[...], preferred_element_type=F32)
    o_ref[0] = _rms(x2, g_ref[...])


def _ffn_down(act, w_down, x1, mod, g):
    b, s, d = x1.shape
    tm = 512
    return pl.pallas_call(
        _ffn_down_kernel,
        grid=(b, s // tm),
        in_specs=[pl.BlockSpec((1, tm, D_FF), lambda bi, i: (bi, i, 0)),
                  pl.BlockSpec((D_FF, d), lambda bi, i: (0, 0)),
                  pl.BlockSpec((1, tm, d), lambda bi, i: (bi, i, 0)),
                  pl.BlockSpec((1, 6, d), lambda bi, i: (bi, 0, 0)),
                  pl.BlockSpec((1, d), lambda bi, i: (0, 0))],
        out_specs=pl.BlockSpec((1, tm, d), lambda bi, i: (bi, i, 0)),
        out_shape=jax.ShapeDtypeStruct((b, s, d), F32),
        compiler_params=_cparams(),
        name="ffn_down",
    )(act, w_down, x1, mod, g)


def _qk_column_order():
    half = HEAD_DIM // 2
    return [(pair * 2 + j) * HEAD_DIM + f * half + i
            for pair in range(GROUP_W // LANES) for f in range(2) for j in range(2)
            for i in range(half)]


_QK_COLS = np.asarray(_qk_column_order(), np.int32)


def _trunk(x, mod, p, rope):
    b, s, _ = x.shape
    h1 = _norm_mod(x, mod, p["norm1_g"])
    qkvd = _mm(h1, p["w_qkvd"], BF16)
    zd = _mm(h1, p["w_zd"], BF16)
    gates, gt = _gates(h1, p["w_ab"], p["a_row"], p["dt_row"])
    os_, ls_ = [], []
    for gi, (_, dil) in enumerate(ATTN_GROUPS):
        o, l = _attn(_proj_attn(h1, p["w_attn"][gi], rope, dil, gi), gi, dil)
        os_.append(o)
        ls_.append(l)
    yd = _deltanet(qkvd, zd, gates, gt, p["conv_qkv_w"], p["dn_norm_g"])
    x1, h2 = _merge_out(os_, ls_, yd, h1, x, mod, p["w_gate"], p["w_br_attn"], p["w_br_dn"],
                        p["w_out"], p["norm2_g"])
    act = _ffn_up(h2, p["w_up"], p["ffn_conv_w"], p["ffn_conv_b"])
    return _ffn_down(act, p["w_down"], x1, mod, p["norm_f_g"])


def kernel(x_prompt, x_sample, c_prompt, c_sample, w_ada, b_ada, norm1_g, w_in, conv_qkv_w, a_log, dt_bias, dn_norm_g, w_br_attn, w_br_dn, w_out, norm2_g, w_up, ffn_conv_w, ffn_conv_b, w_down, norm_f_g):
    d = D_MODEL
    assert w_ada.shape[0] == 1, "single layer"
    w = w_in[0]
    o_qd = 3 * ATTN_W
    o_zd = o_qd + 3 * DN_W
    o_ab = o_zd + DN_W
    o_gate = o_ab + 4 * DN_HEADS
    pad16 = lambda v: jnp.pad(v.reshape(1, 2 * DN_HEADS).astype(F32), ((0, 0), (0, LANES - 2 * DN_HEADS)))
    p = {
        "norm1_g": norm1_g[0].reshape(1, d),
        "w_attn": [jnp.concatenate(
            [w[:, kind * ATTN_W + gi * GROUP_W:kind * ATTN_W + (gi + 1) * GROUP_W][:, cols]
             for kind, cols in ((0, _QK_COLS), (1, _QK_COLS), (2, slice(None)))], axis=1).astype(BF16)
            for gi in range(N_GROUPS)],
        "w_qkvd": w[:, o_qd:o_zd].astype(BF16),
        "w_zd": w[:, o_zd:o_ab].astype(BF16),
        "w_ab": jnp.pad(w[:, o_ab:o_gate], ((0, 0), (0, LANES - 4 * DN_HEADS))).astype(BF16),
        "w_gate": w[:, o_gate:].astype(BF16),
        "a_row": pad16(a_log[0]),
        "dt_row": pad16(dt_bias[0]),
        "conv_qkv_w": conv_qkv_w[0],
        "dn_norm_g": dn_norm_g[0].reshape(1, DN_HEAD_DIM),
        "w_br_attn": w_br_attn[0].astype(BF16),
        "w_br_dn": w_br_dn[0].astype(BF16),
        "w_out": w_out[0].astype(BF16),
        "norm2_g": norm2_g[0].reshape(1, d),
        "w_up": w_up[0].astype(BF16),
        "ffn_conv_w": ffn_conv_w[0],
        "ffn_conv_b": ffn_conv_b[0].reshape(1, 2 * D_FF),
        "w_down": w_down[0].astype(BF16),
        "norm_f_g": norm_f_g.reshape(1, d),
    }
    nb = x_prompt.shape[0]
    mod = _ada(jnp.concatenate([c_prompt, c_sample], axis=0), w_ada[0], b_ada[0].reshape(1, 6 * d))
    mod = mod.reshape(-1, 6, d)
    rope = _rope_tables(max(x_prompt.shape[1], x_sample.shape[1]))
    y_prompt = _trunk(x_prompt, mod[:nb], p, rope)
    y_sample = _trunk(x_sample, mod[nb:], p, rope)
    return (y_prompt, y_sample)
```

```python
import functools

import jax
import jax.numpy as jnp
import numpy as np
from jax import lax
from jax.experimental import pallas as pl
from jax.experimental.pallas import tpu as pltpu

F32 = jnp.float32
BF16 = jnp.bfloat16
HIGHEST = lax.Precision.HIGHEST

D_MODEL = 1024
ATTN_GROUPS = ((128, 1), (512, 4), (2048, 16))
N_GROUPS = 3
HEAD_DIM = 64
GROUP_W = 512
ATTN_W = N_GROUPS * GROUP_W
ATTN_HALF = 64
ATTN_BQ = 128
ATTN_MAX_BLOCKS = 4
ATTN_MAX_ROWS = 2048
ROPE_THETA = 10000.0
NEG = -1e30
DN_HEADS = 8
DN_HEAD_DIM = 128
DN_W = DN_HEADS * DN_HEAD_DIM
DN_CHUNK = 256
DN_BASE = 16
DN_GROUP = 2
DN_HALO = 16
MM_SUB = 256
MERGE_SUB = 256
D_FF = 2816
EPS = 1e-6
LANES = 128
VMEM_LIMIT = 56 * 1024 * 1024


def _cparams(**kw):
    return pltpu.CompilerParams(vmem_limit_bytes=VMEM_LIMIT, **kw)


def _sigmoid(x):
    return 1.0 / (1.0 + jnp.exp(-x))


def _dot(a, b):
    return jnp.dot(a.astype(BF16), b.astype(BF16), preferred_element_type=F32)


def _dot_nt(a, b):
    return lax.dot_general(a.astype(BF16), b.astype(BF16), (((1,), (1,)), ((), ())),
                           preferred_element_type=F32)


def _dot_tn(a, b):
    return lax.dot_general(a.astype(BF16), b.astype(BF16), (((0,), (0,)), ((), ())),
                           preferred_element_type=F32)


def _rms(x, g):
    return x * lax.rsqrt(jnp.mean(x * x, axis=-1, keepdims=True) + EPS) * g


def _ada_kernel(c_ref, w_ref, b_ref, o_ref):
    c = c_ref[...]
    s = c * _sigmoid(c)
    o_ref[...] = jnp.dot(s, w_ref[...], preferred_element_type=F32, precision=HIGHEST) + b_ref[...]


def _ada(c, w, b):
    bt, d = c.shape
    n = w.shape[1]
    tn = 1024
    return pl.pallas_call(
        _ada_kernel,
        grid=(n // tn,),
        in_specs=[pl.BlockSpec((bt, d), lambda j: (0, 0)),
                  pl.BlockSpec((d, tn), lambda j: (0, j)),
                  pl.BlockSpec((1, tn), lambda j: (0, j))],
        out_specs=pl.BlockSpec((bt, tn), lambda j: (0, j)),
        out_shape=jax.ShapeDtypeStruct((bt, n), F32),
        name="ada",
    )(c, w, b)


def _rope_table_kernel(inv_ref, cos_ref, sin_ref):
    tm = cos_ref.shape[0]
    pos = (pl.program_id(0) * tm + lax.broadcasted_iota(jnp.int32, (tm, LANES), 0)).astype(F32)
    ang = pos * inv_ref[...]
    lane = lax.broadcasted_iota(jnp.int32, (tm, LANES), 1)
    cos_ref[...] = jnp.cos(ang)
    sin_ref[...] = jnp.sin(ang) * jnp.where(lane < LANES // 2, -1.0, 1.0)


def _rope_tables(s):
    half = HEAD_DIM // 2
    inv = ROPE_THETA ** (-(jnp.arange(LANES) % half).astype(F32) / half)
    tm = 512
    return pl.pallas_call(
        _rope_table_kernel,
        grid=(s // tm,),
        in_specs=[pl.BlockSpec((1, LANES), lambda i: (0, 0))],
        out_specs=[pl.BlockSpec((tm, LANES), lambda i: (i, 0))] * 2,
        out_shape=[jax.ShapeDtypeStruct((s, LANES), F32)] * 2,
        name="rope_tables",
    )(inv.reshape(1, LANES))


def _norm_mod_kernel(x_ref, mod_ref, g_ref, o_ref):
    h = _rms(x_ref[0], g_ref[...]) * (1.0 + mod_ref[0, 1:2, :]) + mod_ref[0, 0:1, :]
    o_ref[0] = h.astype(o_ref.dtype)


def _norm_mod(x, mod, g):
    b, s, d = x.shape
    tm = 512
    return pl.pallas_call(
        _norm_mod_kernel,
        grid=(b, s // tm),
        in_specs=[pl.BlockSpec((1, tm, d), lambda bi, i: (bi, i, 0)),
                  pl.BlockSpec((1, 6, d), lambda bi, i: (bi, 0, 0)),
                  pl.BlockSpec((1, d), lambda bi, i: (0, 0))],
        out_specs=pl.BlockSpec((1, tm, d), lambda bi, i: (bi, i, 0)),
        out_shape=jax.ShapeDtypeStruct((b, s, d), BF16),
        name="norm_mod",
    )(x, mod, g)


def _mm_kernel(h_ref, w_ref, o_ref):
    o_ref[0] = jnp.dot(h_ref[0], w_ref[...], preferred_element_type=F32).astype(o_ref.dtype)


def _mm(h, w, out_dtype):
    b, s, k = h.shape
    n = w.shape[1]
    tm, tn = 1024, 1024
    return pl.pallas_call(
        _mm_kernel,
        grid=(b, s // tm, n // tn),
        in_specs=[pl.BlockSpec((1, tm, k), lambda bi, i, j: (bi, i, 0)),
                  pl.BlockSpec((k, tn), lambda bi, i, j: (0, j))],
        out_specs=pl.BlockSpec((1, tm, tn), lambda bi, i, j: (bi, i, j)),
        out_shape=jax.ShapeDtypeStruct((b, s, n), out_dtype),
        compiler_params=_cparams(),
        name="proj",
    )(h, w)


def _proj_attn_kernel(h_ref, w_ref, cos_ref, sin_ref, o_ref, *scratch, dil):
    is_qk = pl.program_id(2) < 2
    tm = h_ref.shape[1]
    n_slab = w_ref.shape[1] // LANES
    for s0 in range(0, tm, MM_SUB):
        rows = slice(s0, s0 + MM_SUB)
        acc = jnp.dot(h_ref[0, rows, :], w_ref[...], preferred_element_type=F32)
        cos = jnp.where(is_qk, cos_ref[rows, :], 1.0)
        sin = jnp.where(is_qk, sin_ref[rows, :], 0.0)
        for c in range(n_slab):
            t = acc[:, c * LANES:(c + 1) * LANES]
            val = t * cos + pltpu.roll(t, LANES // 2, 1) * sin
            if dil == 1:
                o_ref[0, 0, 0, rows, c * LANES:(c + 1) * LANES] = val.astype(o_ref.dtype)
            else:
                scratch[0][c, rows, :] = val
        if dil > 1:
            n = MM_SUB // dil
            m0 = s0 // dil
            for r in range(dil):
                for c in range(n_slab):
                    o_ref[0, 0, r, m0:m0 + n, c * LANES:(c + 1) * LANES] = (
                        scratch[0][c, pl.ds(s0 + r, n, stride=dil), :].astype(o_ref.dtype))


def _proj_attn(h, w, rope, dil, gi):
    b, s, k = h.shape
    tm, tn = 2048, GROUP_W
    scratch = [pltpu.VMEM((tn // LANES, tm, LANES), F32)] if dil > 1 else []
    return pl.pallas_call(
        functools.partial(_proj_attn_kernel, dil=dil),
        grid=(b, s // tm, 3),
        in_specs=[pl.BlockSpec((1, tm, k), lambda bi, i, j: (bi, i, 0)),
                  pl.BlockSpec((k, tn), lambda bi, i, j: (0, j)),
                  pl.BlockSpec((tm, LANES), lambda bi, i, j: (i, 0)),
                  pl.BlockSpec((tm, LANES), lambda bi, i, j: (i, 0))],
        out_specs=pl.BlockSpec((1, 1, dil, tm // dil, tn), lambda bi, i, j: (bi, j, 0, i, 0)),
        out_shape=jax.ShapeDtypeStruct((b, 3, dil, s // dil, tn), BF16),
        scratch_shapes=scratch,
        compiler_params=_cparams(),
        name=f"proj_attn_g{gi}",
    )(h, w, *rope)


def _gates_kernel(h_ref, w_ref, a_ref, dt_ref, o_ref, gt_ref):
    c = DN_CHUNK
    r = lax.broadcasted_iota(jnp.int32, (c, c), 0)
    cc = lax.broadcasted_iota(jnp.int32, (c, c), 1)
    lower = jnp.where(cc <= r, 1.0, 0.0).astype(F32)
    upper = jnp.where(cc >= r, 1.0, 0.0).astype(F32)
    lane = lax.broadcasted_iota(jnp.int32, (c, LANES), 1)
    for j in range(h_ref.shape[1] // c):
        acc = jnp.dot(h_ref[0, j * c:(j + 1) * c, :], w_ref[...], preferred_element_type=F32)
        x = acc + dt_ref[...]
        softplus = jnp.maximum(x, 0.0) + jnp.log1p(jnp.exp(-jnp.abs(x)))
        g = -jnp.exp(a_ref[...]) * softplus
        beta = _sigmoid(acc)
        pre = jnp.dot(lower, g, preferred_element_type=F32, precision=HIGHEST)
        suf = jnp.dot(upper, g, preferred_element_type=F32, precision=HIGHEST)
        out = jnp.where(lane < DN_HEADS, pre,
                        jnp.where(lane < 2 * DN_HEADS, suf,
                                  jnp.where(lane < 4 * DN_HEADS, beta, 0.0)))
        o_ref[0, j * c:(j + 1) * c, :] = out
        gt_ref[0, j] = out.T[0:2 * DN_HEADS, :]


def _gates(h, w, a_row, dt_row):
    b, s, k = h.shape
    tm = 4 * DN_CHUNK
    return pl.pallas_call(
        _gates_kernel,
        grid=(b, s // tm),
        in_specs=[pl.BlockSpec((1, tm, k), lambda bi, i: (bi, i, 0)),
                  pl.BlockSpec((k, LANES), lambda bi, i: (0, 0)),
                  pl.BlockSpec((1, LANES), lambda bi, i: (0, 0)),
                  pl.BlockSpec((1, LANES), lambda bi, i: (0, 0))],
        out_specs=[pl.BlockSpec((1, tm, LANES), lambda bi, i: (bi, i, 0)),
                   pl.BlockSpec((1, tm // DN_CHUNK, 2 * DN_HEADS, DN_CHUNK),
                                lambda bi, i: (bi, i, 0, 0))],
        out_shape=[jax.ShapeDtypeStruct((b, s, LANES), F32),
                   jax.ShapeDtypeStruct((b, s // DN_CHUNK, 2 * DN_HEADS, DN_CHUNK), F32)],
        name="dn_gates",
    )(h, w, a_row, dt_row)


def _attn_kernel(q_ref, k_ref, v_ref, o_ref, l_ref, *scratch, seq, bq, nq, kb, dil):
    n_slab = GROUP_W // LANES
    lane = lax.broadcasted_iota(jnp.int32, (bq, LANES), 1)
    blocks = []
    for qb in range(nq):
        row0 = qb * bq
        q0 = (pl.program_id(1) * nq + qb) * bq
        ks = pl.multiple_of(jnp.clip(q0 - ATTN_HALF, 0, seq - kb), ATTN_HALF)
        qpos = q0 + lax.broadcasted_iota(jnp.int32, (bq, kb), 0)
        kpos = ks + lax.broadcasted_iota(jnp.int32, (bq, kb), 1)
        valid = jnp.abs(qpos - kpos) <= ATTN_HALF

        def residue(r, row0=row0, ks=ks, valid=valid):
            q = q_ref[0, 0, r, row0:row0 + bq, :].astype(F32) * (HEAD_DIM ** -0.5)
            k = k_ref[0, 0, r, pl.ds(ks, kb), :]
            v = v_ref[0, 0, r, pl.ds(ks, kb), :]
            heads = [(hp, sub) for hp in range(n_slab) for sub in range(2)]
            in_head = [lane < HEAD_DIM, lane >= HEAD_DIM]
            qk_lanes = [(lane & (HEAD_DIM // 2)) == 0, (lane & (HEAD_DIM // 2)) != 0]
            slab = lambda x, hp: x[:, hp * LANES:(hp + 1) * LANES]
            s_ = [_dot_nt(jnp.where(qk_lanes[sub], slab(q, hp), 0.0), slab(k, hp)) for hp, sub in heads]
            s_ = [jnp.where(valid, s, NEG) for s in s_]
            mx_ = [jnp.max(s, axis=-1, keepdims=True) for s in s_]
            p_ = [jnp.exp(s - mx) for s, mx in zip(s_, mx_)]
            den_ = [jnp.sum(p, axis=-1, keepdims=True) for p in p_]
            o_ = [_dot(p, slab(v, hp)) / den for p, den, (hp, _) in zip(p_, den_, heads)]
            lse_ = [mx + jnp.log(den) for mx, den in zip(mx_, den_)]
            for hp in range(n_slab):
                sl = slice(hp * LANES, (hp + 1) * LANES)
                o_pair = jnp.where(in_head[1], o_[2 * hp + 1], o_[2 * hp])
                l_pair = jnp.where(in_head[1], lse_[2 * hp + 1], lse_[2 * hp])
                if dil == 1:
                    o_ref[0, row0:row0 + bq, sl] = o_pair.astype(o_ref.dtype)
                    l_ref[0, row0:row0 + bq, sl] = l_pair
                else:
                    scratch[0][hp, pl.ds(row0 * dil + r, bq, stride=dil), :] = o_pair
                    scratch[1][hp, pl.ds(row0 * dil + r, bq, stride=dil), :] = l_pair

        blocks.append(residue)

    if dil == 1:
        for block in blocks:
            block(0)
    else:
        def body(r, carry):
            for block in blocks:
                block(r)
            return carry
        lax.fori_loop(0, dil, body, 0, unroll=max(1, ATTN_MAX_BLOCKS // nq))
    if dil > 1:
        for hp in range(n_slab):
            sl = slice(hp * LANES, (hp + 1) * LANES)
            o_ref[0, :, sl] = scratch[0][hp].astype(o_ref.dtype)
            l_ref[0, :, sl] = scratch[1][hp]


def _attn(qkv, gi, dil):
    b, _, _, seq, _ = qkv.shape
    s = seq * dil
    bq = min(ATTN_BQ, seq)
    kb = min(bq + 2 * ATTN_HALF, seq)
    nq = 1
    while 2 * nq <= ATTN_MAX_BLOCKS and 2 * nq * bq * dil <= ATTN_MAX_ROWS and seq % (2 * nq * bq) == 0:
        nq *= 2
    rows = nq * bq * dil
    scratch = [pltpu.VMEM((GROUP_W // LANES, rows, LANES), F32)] * 2 if dil > 1 else []
    kv = lambda kind: pl.BlockSpec((1, 1, dil, seq, GROUP_W), lambda bi, m: (bi, kind, 0, 0, 0))
    return pl.pallas_call(
        functools.partial(_attn_kernel, seq=seq, bq=bq, nq=nq, kb=kb, dil=dil),
        grid=(b, seq // (nq * bq)),
        in_specs=[pl.BlockSpec((1, 1, dil, nq * bq, GROUP_W), lambda bi, m: (bi, 0, 0, m, 0)),
                  kv(1), kv(2)],
        out_specs=[pl.BlockSpec((1, rows, GROUP_W), lambda bi, m: (bi, m, 0))] * 2,
        out_shape=[jax.ShapeDtypeStruct((b, s, GROUP_W), BF16),
                   jax.ShapeDtypeStruct((b, s, GROUP_W), F32)],
        scratch_shapes=scratch,
        compiler_params=_cparams(),
        name=f"attn_g{gi}",
    )(qkv, qkv, qkv)


def _tri_inverse(a_list, masks_ref, bases, uppers, eye):
    ps = [-a * masks_ref[b] for a, b in zip(a_list, bases)]
    ts = [eye + x for x in ps]
    n = 2
    while n < DN_BASE:
        ps = [_dot(p, p).astype(BF16) for p in ps]
        ts = [t + _dot(t, p) for t, p in zip(ts, ps)]
        n *= 2
    n = DN_BASE
    lvl = 1
    while 2 * n < DN_CHUNK:
        left = [_dot(t, a * masks_ref[b + lvl]) for t, a, b in zip(ts, a_list, bases)]
        ts = [t - _dot(l, t) for t, l in zip(ts, left)]
        n *= 2
        lvl += 1
    t11 = [t[0:n, 0:n] for t in ts]
    t22 = [t[n:, n:] for t in ts]
    zero = jnp.zeros((n, n), F32)
    left = [_dot(x11, a[0:n, n:]) if up else _dot(x22, a[n:, 0:n])
            for x11, x22, a, up in zip(t11, t22, a_list, uppers)]
    off = [-_dot(l, x22) if up else -_dot(l, x11) for l, x11, x22, up in zip(left, t11, t22, uppers)]
    return [jnp.concatenate([jnp.concatenate([x11, o if up else zero], axis=1),
                             jnp.concatenate([zero if up else o, x22], axis=1)], axis=0)
            for x11, x22, o, up in zip(t11, t22, off, uppers)]


def _dn_kernel(q_ref, k_ref, v_ref, z_ref, cwq_ref, cwk_ref, cwv_ref, gates_ref, gt_ref, ng_ref,
               o_ref, acc_ref, qkv_a_ref, qkv_b_ref, masks_ref, u_ref, wq_ref, kd_ref, p_ref,
               gl_ref, *, seq):
    h = pl.program_id(1)
    c = DN_CHUNK
    nc = seq // c
    n_lvl = 1
    while DN_BASE << n_lvl < c:
        n_lvl += 1

    r = lax.broadcasted_iota(jnp.int32, (c, c), 0)
    cc = lax.broadcasted_iota(jnp.int32, (c, c), 1)
    as_f32 = lambda m: jnp.where(m, 1.0, 0.0).astype(F32)
    eye = as_f32(r == cc)
    per_dir = 2 + n_lvl
    base_shift = DN_BASE.bit_length() - 1

    @pl.when((pl.program_id(0) == 0) & (h == 0))
    def _():
        for d, (lo, hi) in enumerate(((cc, r), (r, cc))):
            masks_ref[d * per_dir + 0] = as_f32(lo <= hi)
            masks_ref[d * per_dir + 1] = as_f32(lo < hi)
            masks_ref[d * per_dir + 2] = as_f32((lo < hi) & ((lo >> base_shift) == (hi >> base_shift)))
            for lvl in range(1, n_lvl):
                sh = base_shift + lvl - 1
                lo_blk, hi_blk = lo >> sh, hi >> sh
                masks_ref[d * per_dir + 2 + lvl] = as_f32((hi_blk == lo_blk + 1) & ((lo_blk & 1) == 0))

    n_groups = nc // DN_GROUP

    def prep_group(g, dst_ref):
        halo = DN_HALO
        n = c + 2 * halo
        for j in range(DN_GROUP):
            ci = DN_GROUP * g + j
            r0 = pl.multiple_of(ci * c, c)
            lo = pl.multiple_of(jnp.maximum(r0 - halo, 0), halo)
            hi = pl.multiple_of(jnp.minimum(r0 + c, seq - halo), halo)
            keep_lo = jnp.where(ci > 0, 1.0, 0.0)
            keep_hi = jnp.where(ci < nc - 1, 1.0, 0.0)
            for idx, (x_ref, w_ref, kind) in enumerate(((q_ref, cwq_ref, "q"), (k_ref, cwk_ref, "k"),
                                                        (v_ref, cwv_ref, "v"))):
                x = jnp.concatenate([x_ref[0, pl.ds(lo, halo), :].astype(F32) * keep_lo,
                                     x_ref[0, pl.ds(r0, c), :].astype(F32),
                                     x_ref[0, pl.ds(hi, halo), :].astype(F32) * keep_hi], axis=0)
                w = w_ref[...]
                y = (pltpu.roll(x, 1, 0) * w[0:1, :] + x * w[1:2, :]
                     + pltpu.roll(x, n - 1, 0) * w[2:3, :])[halo:halo + c]
                y = y * _sigmoid(y)
                if kind != "v":
                    y = y * lax.rsqrt(jnp.sum(y * y, axis=-1, keepdims=True) + EPS)
                if kind == "q":
                    y = y * (DN_HEAD_DIM ** -0.5)
                dst_ref[idx, j * c:(j + 1) * c, :] = y

    prep_group(jnp.int32(0), qkv_a_ref)

    lane = lax.broadcasted_iota(jnp.int32, (c, LANES), 1)

    def solve_group(g, src_ref):
        sys_, a_list, bases = [], [], []
        for j in range(DN_GROUP):
            ci = DN_GROUP * g + j
            r0 = pl.multiple_of(ci * c, c)
            q = src_ref[0, j * c:(j + 1) * c, :]
            k = src_ref[1, j * c:(j + 1) * c, :]
            v = src_ref[2, j * c:(j + 1) * c, :]
            gch = gates_ref[0, pl.ds(r0, c), :]
            kk = _dot_nt(k, k)
            qk = _dot_nt(q, k)
            for d in range(2):
                col = lambda j: jnp.sum(jnp.where(lane == j, gch, 0.0), axis=1, keepdims=True)
                gc = col(d * DN_HEADS + h)
                beta = col((2 + d) * DN_HEADS + h)
                grow = gt_ref[0, ci, pl.ds(d * DN_HEADS + h, 1), :]
                g_end = gc[0:1, :] if d else gc[c - 1:c, :]
                dec = jnp.exp(jnp.minimum(gc - grow, 0.0))
                e_g = jnp.exp(gc)
                a_list.append(kk * (beta * dec) * masks_ref[d * per_dir + 1])
                bases.append(d * per_dir + 2)
                sys_.append(dict(
                    ci=ci, r0=r0, d=d,
                    rhs=jnp.concatenate([v * beta, k * (beta * e_g)], axis=1),
                    qg=(q * e_g).astype(BF16),
                    kd=(k * jnp.exp(g_end - gc)).astype(BF16),
                    p=(qk * dec * masks_ref[d * per_dir]).astype(BF16),
                    gl=jnp.broadcast_to(jnp.exp(g_end), (1, LANES))))
        t_list = _tri_inverse(a_list, masks_ref, bases, [s["d"] == 1 for s in sys_], eye)
        uw_list = [_dot(t, s["rhs"]) for t, s in zip(t_list, sys_)]
        for s, uw in zip(sys_, uw_list):
            d, ci, r0 = s["d"], s["ci"], s["r0"]
            u_ref[d, pl.ds(r0, c), :] = uw[:, 0:LANES]
            wq_ref[d, ci, 0:c, :] = uw[:, LANES:2 * LANES].astype(BF16)
            wq_ref[d, ci, c:2 * c, :] = s["qg"]
            kd_ref[d, pl.ds(r0, c), :] = s["kd"]
            p_ref[d, ci] = s["p"]
            gl_ref[d, pl.ds(ci, 1), :] = s["gl"]

    def phase_a(m, carry):
        prep_group(2 * m + 1, qkv_b_ref)
        solve_group(2 * m, qkv_a_ref)
        prep_group(jnp.minimum(2 * m + 2, n_groups - 1), qkv_a_ref)
        solve_group(2 * m + 1, qkv_b_ref)
        return carry

    lax.fori_loop(0, n_groups // 2, phase_a, 0)

    def phase_b(i, carry, second_half):
        cis = (i, nc - 1 - i)
        r0s = [pl.multiple_of(ci * c, c) for ci in cis]
        ws_qs = [jnp.dot(wq_ref[d, cis[d]], carry[d].astype(BF16), preferred_element_type=F32)
                 for d in range(2)]
        v_new = [(u_ref[d, pl.ds(r0s[d], c), :] - ws_qs[d][0:c]).astype(BF16) for d in range(2)]
        outs = [ws_qs[d][c:2 * c] + jnp.dot(p_ref[d, cis[d]], v_new[d], preferred_element_type=F32)
                for d in range(2)]
        new = [carry[d] * gl_ref[d, pl.ds(cis[d], 1), :]
               + _dot_tn(kd_ref[d, pl.ds(r0s[d], c), :], v_new[d]) for d in range(2)]
        for d in range(2):
            rows = pl.ds(r0s[d], c)
            if not second_half:
                acc_ref[rows, :] = outs[d]
            else:
                z = z_ref[0, rows, :].astype(F32)
                y = _rms(acc_ref[rows, :] + outs[d], ng_ref[...]) * (z * _sigmoid(z))
                o_ref[0, rows, :] = y.astype(o_ref.dtype)
        return tuple(new)

    zero = jnp.zeros((DN_HEAD_DIM, DN_HEAD_DIM), F32)
    mid = lax.fori_loop(0, nc // 2, functools.partial(phase_b, second_half=False), (zero, zero))
    lax.fori_loop(nc // 2, nc, functools.partial(phase_b, second_half=True), mid)


def _deltanet(qkvd, zd, gates, gt, conv_w, norm_g):
    b, s, _ = qkvd.shape
    hd = DN_HEAD_DIM
    nc = s // DN_CHUNK
    n_lvl = 1
    while DN_BASE << n_lvl < DN_CHUNK:
        n_lvl += 1
    n_masks = 2 * (2 + n_lvl)
    col = lambda off: pl.BlockSpec((1, s, hd), lambda bi, h: (bi, 0, off + h))
    cw = lambda off: pl.BlockSpec((3, hd), lambda bi, h: (0, off + h))
    return pl.pallas_call(
        functools.partial(_dn_kernel, seq=s),
        grid=(b, DN_HEADS),
        in_specs=[col(0), col(DN_HEADS), col(2 * DN_HEADS), col(0),
                  cw(0), cw(DN_HEADS), cw(2 * DN_HEADS),
                  pl.BlockSpec((1, s, LANES), lambda bi, h: (bi, 0, 0)),
                  pl.BlockSpec((1, nc, 2 * DN_HEADS, DN_CHUNK), lambda bi, h: (bi, 0, 0, 0)),
                  pl.BlockSpec((1, hd), lambda bi, h: (0, 0))],
        out_specs=pl.BlockSpec((1, s, hd), lambda bi, h: (bi, 0, h)),
        out_shape=jax.ShapeDtypeStruct((b, s, DN_W), BF16),
        scratch_shapes=[pltpu.VMEM((s, hd), F32),
                        pltpu.VMEM((3, DN_GROUP * DN_CHUNK, hd), F32),
                        pltpu.VMEM((3, DN_GROUP * DN_CHUNK, hd), F32),
                        pltpu.VMEM((n_masks, DN_CHUNK, DN_CHUNK), F32),
                        pltpu.VMEM((2, s, hd), F32),
                        pltpu.VMEM((2, nc, 2 * DN_CHUNK, hd), BF16),
                        pltpu.VMEM((2, s, hd), BF16),
                        pltpu.VMEM((2, nc, DN_CHUNK, DN_CHUNK), BF16),
                        pltpu.VMEM((2, max(nc, 8), hd), F32)],
        compiler_params=_cparams(dimension_semantics=("arbitrary", "arbitrary")),
        name="deltanet",
    )(qkvd, qkvd, qkvd, zd, conv_w, conv_w, conv_w, gates, gt, norm_g)


def _merge_kernel(o1_ref, o2_ref, o3_ref, l1_ref, l2_ref, l3_ref, yd_ref, h1_ref, x_ref, mod_ref,
                  wg_ref, wba_ref, wbd_ref, wo_ref, g2_ref, x1_ref, h2_ref):
    d = D_MODEL
    for s0 in range(0, x_ref.shape[1], MERGE_SUB):
        rows = slice(s0, s0 + MERGE_SUB)
        l1, l2, l3 = l1_ref[0, rows, :], l2_ref[0, rows, :], l3_ref[0, rows, :]
        mx = jnp.maximum(l1, jnp.maximum(l2, l3))
        e1, e2, e3 = jnp.exp(l1 - mx), jnp.exp(l2 - mx), jnp.exp(l3 - mx)
        ya = (e1 * o1_ref[0, rows, :] + e2 * o2_ref[0, rows, :] + e3 * o3_ref[0, rows, :]) / (e1 + e2 + e3)
        h1 = h1_ref[0, rows, :]
        gate_a = _sigmoid(jnp.dot(h1, wg_ref[:, 0:d], preferred_element_type=F32))
        gate_d = _sigmoid(jnp.dot(h1, wg_ref[:, d:2 * d], preferred_element_type=F32))
        merged = gate_a * _dot(ya, wba_ref[...]) + gate_d * _dot(yd_ref[0, rows, :], wbd_ref[...])
        x1 = x_ref[0, rows, :] + mod_ref[0, 2:3, :] * _dot(merged, wo_ref[...])
        x1_ref[0, rows, :] = x1
        h2 = _rms(x1, g2_ref[...]) * (1.0 + mod_ref[0, 4:5, :]) + mod_ref[0, 3:4, :]
        h2_ref[0, rows, :] = h2.astype(h2_ref.dtype)


def _merge_out(os_, ls_, yd, h1, x, mod, wg, wba, wbd, wo, g2):
    b, s, d = x.shape
    tm = 2 * MERGE_SUB
    row = lambda w: pl.BlockSpec((1, tm, w), lambda bi, i: (bi, i, 0))
    full = lambda a: pl.BlockSpec(a.shape, lambda bi, i: (0,) * a.ndim)
    return pl.pallas_call(
        _merge_kernel,
        grid=(b, s // tm),
        in_specs=[row(GROUP_W)] * 6 + [row(DN_W), row(d), row(d),
                                       pl.BlockSpec((1, 6, d), lambda bi, i: (bi, 0, 0)),
                                       full(wg), full(wba), full(wbd), full(wo), full(g2)],
        out_specs=[row(d), row(d)],
        out_shape=[jax.ShapeDtypeStruct((b, s, d), F32), jax.ShapeDtypeStruct((b, s, d), BF16)],
        compiler_params=_cparams(),
        name="merge_out",
    )(*os_, *ls_, yd, h1, x, mod, wg, wba, wbd, wo, g2)


FFN_HALO = 16
FFN_SUB = 512


def _ffn_up_kernel(hp_ref, hm_ref, hn_ref, wv_ref, wg_ref, cwv_ref, cwg_ref, bv_ref, bg_ref,
                   o_ref, *, nt):
    i = pl.program_id(1)
    tm = hm_ref.shape[1]
    lhs_all = jnp.concatenate([hp_ref[0], hm_ref[0], hn_ref[0]], axis=0)
    sub = FFN_SUB
    n = sub + 2 * FFN_HALO
    for s0 in range(0, tm, sub):
        lhs = lhs_all[s0:s0 + n]
        keep_prev = jnp.where(i > 0, 1.0, 0.0) if s0 == 0 else None
        keep_next = jnp.where(i < nt - 1, 1.0, 0.0) if s0 + sub == tm else None
        ups = [jnp.dot(lhs, w_ref[...], preferred_element_type=F32) for w_ref in (wv_ref, wg_ref)]

        def conv(up, cw_ref, b_ref):
            head, tail = up[0:FFN_HALO], up[FFN_HALO + sub:]
            if keep_prev is not None:
                head = head * keep_prev
            if keep_next is not None:
                tail = tail * keep_next
            up = jnp.concatenate([head, up[FFN_HALO:FFN_HALO + sub], tail], axis=0)
            cw = cw_ref[...]
            y = (pltpu.roll(up, 1, 0) * cw[0:1, :] + up * cw[1:2, :]
                 + pltpu.roll(up, n - 1, 0) * cw[2:3, :])
            return y[FFN_HALO:FFN_HALO + sub] + b_ref[...]

        val = conv(ups[0], cwv_ref, bv_ref)
        gate = conv(ups[1], cwg_ref, bg_ref)
        o_ref[0, s0:s0 + sub, :] = (gate * _sigmoid(gate) * val).astype(o_ref.dtype)


def _ffn_up(h2, w_up, conv_w, conv_b):
    b, s, d = h2.shape
    tm, tn = 2048, 256
    nt = s // tm
    nj = D_FF // tn
    hb = tm // FFN_HALO
    return pl.pallas_call(
        functools.partial(_ffn_up_kernel, nt=nt),
        grid=(b, nt, nj),
        in_specs=[pl.BlockSpec((1, FFN_HALO, d), lambda bi, i, j: (bi, jnp.maximum(i * hb - 1, 0), 0)),
                  pl.BlockSpec((1, tm, d), lambda bi, i, j: (bi, i, 0)),
                  pl.BlockSpec((1, FFN_HALO, d),
                               lambda bi, i, j: (bi, jnp.minimum((i + 1) * hb, s // FFN_HALO - 1), 0)),
                  pl.BlockSpec((d, tn), lambda bi, i, j: (0, j)),
                  pl.BlockSpec((d, tn), lambda bi, i, j: (0, j + nj)),
                  pl.BlockSpec((3, tn), lambda bi, i, j: (0, j)),
                  pl.BlockSpec((3, tn), lambda bi, i, j: (0, j + nj)),
                  pl.BlockSpec((1, tn), lambda bi, i, j: (0, j)),
                  pl.BlockSpec((1, tn), lambda bi, i, j: (0, j + nj))],
        out_specs=pl.BlockSpec((1, tm, tn), lambda bi, i, j: (bi, i, j)),
        out_shape=jax.ShapeDtypeStruct((b, s, D_FF), BF16),
        compiler_params=_cparams(),
        name="ffn_up",
    )(h2, h2, h2, w_up, w_up, conv_w, conv_w, conv_b, conv_b)


def _ffn_down_kernel(a_ref, w_ref, x_ref, mod_ref, g_ref, o_ref):
    for s0 in range(0, x_ref.shape[1], FFN_SUB):
        rows = slice(s0, s0 + FFN_SUB)
        x2 = x_ref[0, rows, :] + mod_ref[0, 5:6, :] * jnp.dot(a_ref[0, rows, :], w_ref[...],
                                                             preferred_element_type=F32)
        o_ref[0, rows, :] = _rms(x2, g_ref[...])


def _ffn_down(act, w_down, x1, mod, g):
    b, s, d = x1.shape
    tm = 1024
    return pl.pallas_call(
        _ffn_down_kernel,
        grid=(b, s // tm),
        in_specs=[pl.BlockSpec((1, tm, D_FF), lambda bi, i: (bi, i, 0)),
                  pl.BlockSpec((D_FF, d), lambda bi, i: (0, 0)),
                  pl.BlockSpec((1, tm, d), lambda bi, i: (bi, i, 0)),
                  pl.BlockSpec((1, 6, d), lambda bi, i: (bi, 0, 0)),
                  pl.BlockSpec((1, d), lambda bi, i: (0, 0))],
        out_specs=pl.BlockSpec((1, tm, d), lambda bi, i: (bi, i, 0)),
        out_shape=jax.ShapeDtypeStruct((b, s, d), F32),
        compiler_params=_cparams(),
        name="ffn_down",
    )(act, w_down, x1, mod, g)


def _qk_column_order():
    half = HEAD_DIM // 2
    return [(pair * 2 + j) * HEAD_DIM + f * half + i
            for pair in range(GROUP_W // LANES) for f in range(2) for j in range(2)
            for i in range(half)]


_QK_COLS = np.asarray(_qk_column_order(), np.int32)


def _trunk(x, mod, p, rope):
    b, s, _ = x.shape
    h1 = _norm_mod(x, mod, p["norm1_g"])
    qkvd = _mm(h1, p["w_qkvd"], BF16)
    zd = _mm(h1, p["w_zd"], BF16)
    gates, gt = _gates(h1, p["w_ab"], p["a_row"], p["dt_row"])
    os_, ls_ = [], []
    for gi, (_, dil) in enumerate(ATTN_GROUPS):
        o, l = _attn(_proj_attn(h1, p["w_attn"][gi], rope, dil, gi), gi, dil)
        os_.append(o)
        ls_.append(l)
    yd = _deltanet(qkvd, zd, gates, gt, p["conv_qkv_w"], p["dn_norm_g"])
    x1, h2 = _merge_out(os_, ls_, yd, h1, x, mod, p["w_gate"], p["w_br_attn"], p["w_br_dn"],
                        p["w_out"], p["norm2_g"])
    act = _ffn_up(h2, p["w_up"], p["ffn_conv_w"], p["ffn_conv_b"])
    return _ffn_down(act, p["w_down"], x1, mod, p["norm_f_g"])


def kernel(x_prompt, x_sample, c_prompt, c_sample, w_ada, b_ada, norm1_g, w_in, conv_qkv_w, a_log, dt_bias, dn_norm_g, w_br_attn, w_br_dn, w_out, norm2_g, w_up, ffn_conv_w, ffn_conv_b, w_down, norm_f_g):
    d = D_MODEL
    assert w_ada.shape[0] == 1, "single layer"
    w = w_in[0]
    o_qd = 3 * ATTN_W
    o_zd = o_qd + 3 * DN_W
    o_ab = o_zd + DN_W
    o_gate = o_ab + 4 * DN_HEADS
    pad16 = lambda v: jnp.pad(v.reshape(1, 2 * DN_HEADS).astype(F32), ((0, 0), (0, LANES - 2 * DN_HEADS)))
    p = {
        "norm1_g": norm1_g[0].reshape(1, d),
        "w_attn": [jnp.concatenate(
            [w[:, kind * ATTN_W + gi * GROUP_W:kind * ATTN_W + (gi + 1) * GROUP_W][:, cols]
             for kind, cols in ((0, _QK_COLS), (1, _QK_COLS), (2, slice(None)))], axis=1).astype(BF16)
            for gi in range(N_GROUPS)],
        "w_qkvd": w[:, o_qd:o_zd].astype(BF16),
        "w_zd": w[:, o_zd:o_ab].astype(BF16),
        "w_ab": jnp.pad(w[:, o_ab:o_gate], ((0, 0), (0, LANES - 4 * DN_HEADS))).astype(BF16),
        "w_gate": w[:, o_gate:].astype(BF16),
        "a_row": pad16(a_log[0]),
        "dt_row": pad16(dt_bias[0]),
        "conv_qkv_w": conv_qkv_w[0],
        "dn_norm_g": dn_norm_g[0].reshape(1, DN_HEAD_DIM),
        "w_br_attn": w_br_attn[0].astype(BF16),
        "w_br_dn": w_br_dn[0].astype(BF16),
        "w_out": w_out[0].astype(BF16),
        "norm2_g": norm2_g[0].reshape(1, d),
        "w_up": w_up[0].astype(BF16),
        "ffn_conv_w": ffn_conv_w[0],
        "ffn_conv_b": ffn_conv_b[0].reshape(1, 2 * D_FF),
        "w_down": w_down[0].astype(BF16),
        "norm_f_g": norm_f_g.reshape(1, d),
    }
    nb = x_prompt.shape[0]
    mod = _ada(jnp.concatenate([c_prompt, c_sample], axis=0), w_ada[0], b_ada[0].reshape(1, 6 * d))
    mod = mod.reshape(-1, 6, d)
    rope = _rope_tables(max(x_prompt.shape[1], x_sample.shape[1]))
    y_prompt = _trunk(x_prompt, mod[:nb], p, rope)
    y_sample = _trunk(x_sample, mod[nb:], p, rope)
    return (y_prompt, y_sample)
```

```python
import functools

import jax
import jax.numpy as jnp
import numpy as np
from jax import lax
from jax.experimental import pallas as pl
from jax.experimental.pallas import tpu as pltpu

F32 = jnp.float32
BF16 = jnp.bfloat16
HIGHEST = lax.Precision.HIGHEST

D_MODEL = 1024
ATTN_GROUPS = ((128, 1), (512, 4), (2048, 16))
N_GROUPS = 3
HEAD_DIM = 64
GROUP_W = 512
ATTN_W = N_GROUPS * GROUP_W
ATTN_HALF = 64
ATTN_BQ = 128
ATTN_MAX_BLOCKS = 4
ATTN_MAX_ROWS = 2048
ROPE_THETA = 10000.0
NEG = -1e30
DN_HEADS = 8
DN_HEAD_DIM = 128
DN_W = DN_HEADS * DN_HEAD_DIM
DN_CHUNK = 256
DN_BASE = 16
DN_GROUP = 2
DN_HALO = 16
MM_SUB = 256
MERGE_SUB = 256
D_FF = 2816
EPS = 1e-6
LANES = 128
VMEM_LIMIT = 56 * 1024 * 1024


def _cparams(**kw):
    return pltpu.CompilerParams(vmem_limit_bytes=VMEM_LIMIT, **kw)


def _sigmoid(x):
    return 1.0 / (1.0 + jnp.exp(-x))


def _dot(a, b):
    return jnp.dot(a.astype(BF16), b.astype(BF16), preferred_element_type=F32)


def _dot_nt(a, b):
    return lax.dot_general(a.astype(BF16), b.astype(BF16), (((1,), (1,)), ((), ())),
                           preferred_element_type=F32)


def _dot_tn(a, b):
    return lax.dot_general(a.astype(BF16), b.astype(BF16), (((0,), (0,)), ((), ())),
                           preferred_element_type=F32)


def _rms(x, g):
    return x * lax.rsqrt(jnp.mean(x * x, axis=-1, keepdims=True) + EPS) * g


def _ada_kernel(c_ref, w_ref, b_ref, o_ref):
    c = c_ref[...]
    s = c * _sigmoid(c)
    o_ref[...] = jnp.dot(s, w_ref[...], preferred_element_type=F32, precision=HIGHEST) + b_ref[...]


def _ada(c, w, b):
    bt, d = c.shape
    n = w.shape[1]
    tn = 1024
    return pl.pallas_call(
        _ada_kernel,
        grid=(n // tn,),
        in_specs=[pl.BlockSpec((bt, d), lambda j: (0, 0)),
                  pl.BlockSpec((d, tn), lambda j: (0, j)),
                  pl.BlockSpec((1, tn), lambda j: (0, j))],
        out_specs=pl.BlockSpec((bt, tn), lambda j: (0, j)),
        out_shape=jax.ShapeDtypeStruct((bt, n), F32),
        name="ada",
    )(c, w, b)


def _rope_table_kernel(inv_ref, cos_ref, sin_ref):
    tm = cos_ref.shape[0]
    pos = (pl.program_id(0) * tm + lax.broadcasted_iota(jnp.int32, (tm, LANES), 0)).astype(F32)
    ang = pos * inv_ref[...]
    lane = lax.broadcasted_iota(jnp.int32, (tm, LANES), 1)
    cos_ref[...] = jnp.cos(ang)
    sin_ref[...] = jnp.sin(ang) * jnp.where(lane < LANES // 2, -1.0, 1.0)


def _rope_tables(s):
    half = HEAD_DIM // 2
    inv = ROPE_THETA ** (-(jnp.arange(LANES) % half).astype(F32) / half)
    tm = 512
    return pl.pallas_call(
        _rope_table_kernel,
        grid=(s // tm,),
        in_specs=[pl.BlockSpec((1, LANES), lambda i: (0, 0))],
        out_specs=[pl.BlockSpec((tm, LANES), lambda i: (i, 0))] * 2,
        out_shape=[jax.ShapeDtypeStruct((s, LANES), F32)] * 2,
        name="rope_tables",
    )(inv.reshape(1, LANES))


def _norm_mod_kernel(x_ref, mod_ref, g_ref, o_ref):
    h = _rms(x_ref[0], g_ref[...]) * (1.0 + mod_ref[0, 1:2, :]) + mod_ref[0, 0:1, :]
    o_ref[0] = h.astype(o_ref.dtype)


def _norm_mod(x, mod, g):
    b, s, d = x.shape
    tm = 512
    return pl.pallas_call(
        _norm_mod_kernel,
        grid=(b, s // tm),
        in_specs=[pl.BlockSpec((1, tm, d), lambda bi, i: (bi, i, 0)),
                  pl.BlockSpec((1, 6, d), lambda bi, i: (bi, 0, 0)),
                  pl.BlockSpec((1, d), lambda bi, i: (0, 0))],
        out_specs=pl.BlockSpec((1, tm, d), lambda bi, i: (bi, i, 0)),
        out_shape=jax.ShapeDtypeStruct((b, s, d), BF16),
        name="norm_mod",
    )(x, mod, g)


def _mm_kernel(h_ref, w_ref, o_ref):
    o_ref[0] = jnp.dot(h_ref[0], w_ref[...], preferred_element_type=F32).astype(o_ref.dtype)


def _mm(h, w, out_dtype):
    b, s, k = h.shape
    n = w.shape[1]
    tm, tn = 1024, 1024
    return pl.pallas_call(
        _mm_kernel,
        grid=(b, s // tm, n // tn),
        in_specs=[pl.BlockSpec((1, tm, k), lambda bi, i, j: (bi, i, 0)),
                  pl.BlockSpec((k, tn), lambda bi, i, j: (0, j))],
        out_specs=pl.BlockSpec((1, tm, tn), lambda bi, i, j: (bi, i, j)),
        out_shape=jax.ShapeDtypeStruct((b, s, n), out_dtype),
        compiler_params=_cparams(),
        name="proj",
    )(h, w)


def _proj_attn_kernel(h_ref, w_ref, cos_ref, sin_ref, o_ref, *scratch, dil):
    is_qk = pl.program_id(2) < 2
    tm = h_ref.shape[1]
    n_slab = w_ref.shape[1] // LANES
    for s0 in range(0, tm, MM_SUB):
        rows = slice(s0, s0 + MM_SUB)
        acc = jnp.dot(h_ref[0, rows, :], w_ref[...], preferred_element_type=F32)
        cos = jnp.where(is_qk, cos_ref[rows, :], 1.0)
        sin = jnp.where(is_qk, sin_ref[rows, :], 0.0)
        for c in range(n_slab):
            t = acc[:, c * LANES:(c + 1) * LANES]
            val = t * cos + pltpu.roll(t, LANES // 2, 1) * sin
            if dil == 1:
                o_ref[0, 0, 0, rows, c * LANES:(c + 1) * LANES] = val.astype(o_ref.dtype)
            else:
                scratch[0][c, rows, :] = val
        if dil > 1:
            n = MM_SUB // dil
            m0 = s0 // dil
            for r in range(dil):
                for c in range(n_slab):
                    o_ref[0, 0, r, m0:m0 + n, c * LANES:(c + 1) * LANES] = (
                        scratch[0][c, pl.ds(s0 + r, n, stride=dil), :].astype(o_ref.dtype))


def _proj_attn(h, w, rope, dil, gi):
    b, s, k = h.shape
    tm, tn = 2048, GROUP_W
    scratch = [pltpu.VMEM((tn // LANES, tm, LANES), F32)] if dil > 1 else []
    return pl.pallas_call(
        functools.partial(_proj_attn_kernel, dil=dil),
        grid=(b, s // tm, 3),
        in_specs=[pl.BlockSpec((1, tm, k), lambda bi, i, j: (bi, i, 0)),
                  pl.BlockSpec((k, tn), lambda bi, i, j: (0, j)),
                  pl.BlockSpec((tm, LANES), lambda bi, i, j: (i, 0)),
                  pl.BlockSpec((tm, LANES), lambda bi, i, j: (i, 0))],
        out_specs=pl.BlockSpec((1, 1, dil, tm // dil, tn), lambda bi, i, j: (bi, j, 0, i, 0)),
        out_shape=jax.ShapeDtypeStruct((b, 3, dil, s // dil, tn), BF16),
        scratch_shapes=scratch,
        compiler_params=_cparams(),
        name=f"proj_attn_g{gi}",
    )(h, w, *rope)


def _gates_kernel(h_ref, w_ref, a_ref, dt_ref, o_ref, gt_ref):
    c = DN_CHUNK
    r = lax.broadcasted_iota(jnp.int32, (c, c), 0)
    cc = lax.broadcasted_iota(jnp.int32, (c, c), 1)
    lower = jnp.where(cc <= r, 1.0, 0.0).astype(F32)
    upper = jnp.where(cc >= r, 1.0, 0.0).astype(F32)
    lane = lax.broadcasted_iota(jnp.int32, (c, LANES), 1)
    for j in range(h_ref.shape[1] // c):
        acc = jnp.dot(h_ref[0, j * c:(j + 1) * c, :], w_ref[...], preferred_element_type=F32)
        x = acc + dt_ref[...]
        softplus = jnp.maximum(x, 0.0) + jnp.log1p(jnp.exp(-jnp.abs(x)))
        g = -jnp.exp(a_ref[...]) * softplus
        beta = _sigmoid(acc)
        pre = jnp.dot(lower, g, preferred_element_type=F32, precision=HIGHEST)
        suf = jnp.dot(upper, g, preferred_element_type=F32, precision=HIGHEST)
        out = jnp.where(lane < DN_HEADS, pre,
                        jnp.where(lane < 2 * DN_HEADS, suf,
                                  jnp.where(lane < 4 * DN_HEADS, beta, 0.0)))
        o_ref[0, j * c:(j + 1) * c, :] = out
        gt_ref[0, j] = out.T[0:2 * DN_HEADS, :]


def _gates(h, w, a_row, dt_row):
    b, s, k = h.shape
    tm = 4 * DN_CHUNK
    return pl.pallas_call(
        _gates_kernel,
        grid=(b, s // tm),
        in_specs=[pl.BlockSpec((1, tm, k), lambda bi, i: (bi, i, 0)),
                  pl.BlockSpec((k, LANES), lambda bi, i: (0, 0)),
                  pl.BlockSpec((1, LANES), lambda bi, i: (0, 0)),
                  pl.BlockSpec((1, LANES), lambda bi, i: (0, 0))],
        out_specs=[pl.BlockSpec((1, tm, LANES), lambda bi, i: (bi, i, 0)),
                   pl.BlockSpec((1, tm // DN_CHUNK, 2 * DN_HEADS, DN_CHUNK),
                                lambda bi, i: (bi, i, 0, 0))],
        out_shape=[jax.ShapeDtypeStruct((b, s, LANES), F32),
                   jax.ShapeDtypeStruct((b, s // DN_CHUNK, 2 * DN_HEADS, DN_CHUNK), F32)],
        name="dn_gates",
    )(h, w, a_row, dt_row)


def _attn_kernel(q_ref, k_ref, v_ref, o_ref, l_ref, *scratch, seq, bq, nq, kb, dil):
    n_slab = GROUP_W // LANES
    lane = lax.broadcasted_iota(jnp.int32, (bq, LANES), 1)
    blocks = []
    for qb in range(nq):
        row0 = qb * bq
        q0 = (pl.program_id(1) * nq + qb) * bq
        ks = pl.multiple_of(jnp.clip(q0 - ATTN_HALF, 0, seq - kb), ATTN_HALF)
        qpos = q0 + lax.broadcasted_iota(jnp.int32, (bq, kb), 0)
        kpos = ks + lax.broadcasted_iota(jnp.int32, (bq, kb), 1)
        valid = jnp.abs(qpos - kpos) <= ATTN_HALF

        def residue(r, row0=row0, ks=ks, valid=valid):
            q = q_ref[0, 0, r, row0:row0 + bq, :].astype(F32) * (HEAD_DIM ** -0.5)
            k = k_ref[0, 0, r, pl.ds(ks, kb), :]
            v = v_ref[0, 0, r, pl.ds(ks, kb), :]
            heads = [(hp, sub) for hp in range(n_slab) for sub in range(2)]
            in_head = [lane < HEAD_DIM, lane >= HEAD_DIM]
            qk_lanes = [(lane & (HEAD_DIM // 2)) == 0, (lane & (HEAD_DIM // 2)) != 0]
            slab = lambda x, hp: x[:, hp * LANES:(hp + 1) * LANES]
            s_ = [_dot_nt(jnp.where(qk_lanes[sub], slab(q, hp), 0.0), slab(k, hp)) for hp, sub in heads]
            s_ = [jnp.where(valid, s, NEG) for s in s_]
            mx_ = [jnp.max(s, axis=-1, keepdims=True) for s in s_]
            p_ = [jnp.exp(s - mx) for s, mx in zip(s_, mx_)]
            den_ = [jnp.sum(p, axis=-1, keepdims=True) for p in p_]
            o_ = [_dot(p, slab(v, hp)) / den for p, den, (hp, _) in zip(p_, den_, heads)]
            lse_ = [mx + jnp.log(den) for mx, den in zip(mx_, den_)]
            for hp in range(n_slab):
                sl = slice(hp * LANES, (hp + 1) * LANES)
                o_pair = jnp.where(in_head[1], o_[2 * hp + 1], o_[2 * hp])
                l_pair = jnp.where(in_head[1], lse_[2 * hp + 1], lse_[2 * hp])
                if dil == 1:
                    o_ref[0, row0:row0 + bq, sl] = o_pair.astype(o_ref.dtype)
                    l_ref[0, row0:row0 + bq, sl] = l_pair
                else:
                    scratch[0][hp, pl.ds(row0 * dil + r, bq, stride=dil), :] = o_pair
                    scratch[1][hp, pl.ds(row0 * dil + r, bq, stride=dil), :] = l_pair

        blocks.append(residue)

    if dil == 1:
        for block in blocks:
            block(0)
    else:
        def body(r, carry):
            for block in blocks:
                block(r)
            return carry
        lax.fori_loop(0, dil, body, 0, unroll=max(1, ATTN_MAX_BLOCKS // nq))
    if dil > 1:
        for hp in range(n_slab):
            sl = slice(hp * LANES, (hp + 1) * LANES)
            o_ref[0, :, sl] = scratch[0][hp].astype(o_ref.dtype)
            l_ref[0, :, sl] = scratch[1][hp]


def _attn(qkv, gi, dil):
    b, _, _, seq, _ = qkv.shape
    s = seq * dil
    bq = min(ATTN_BQ, seq)
    kb = min(bq + 2 * ATTN_HALF, seq)
    nq = 1
    while 2 * nq <= ATTN_MAX_BLOCKS and 2 * nq * bq * dil <= ATTN_MAX_ROWS and seq % (2 * nq * bq) == 0:
        nq *= 2
    rows = nq * bq * dil
    scratch = [pltpu.VMEM((GROUP_W // LANES, rows, LANES), F32)] * 2 if dil > 1 else []
    kv = lambda kind: pl.BlockSpec((1, 1, dil, seq, GROUP_W), lambda bi, m: (bi, kind, 0, 0, 0))
    return pl.pallas_call(
        functools.partial(_attn_kernel, seq=seq, bq=bq, nq=nq, kb=kb, dil=dil),
        grid=(b, seq // (nq * bq)),
        in_specs=[pl.BlockSpec((1, 1, dil, nq * bq, GROUP_W), lambda bi, m: (bi, 0, 0, m, 0)),
                  kv(1), kv(2)],
        out_specs=[pl.BlockSpec((1, rows, GROUP_W), lambda bi, m: (bi, m, 0))] * 2,
        out_shape=[jax.ShapeDtypeStruct((b, s, GROUP_W), BF16),
                   jax.ShapeDtypeStruct((b, s, GROUP_W), F32)],
        scratch_shapes=scratch,
        compiler_params=_cparams(),
        name=f"attn_g{gi}",
    )(qkv, qkv, qkv)


def _tri_inverse(a_list, masks_ref, lmask_ref, bases, uppers, eye):
    ts = [eye - a * masks_ref[b] for a, b in zip(a_list, bases)]
    a_list = [a.astype(BF16) for a in a_list]
    ps = [-(a * lmask_ref[b]) for a, b in zip(a_list, bases)]
    n = 2
    while n < DN_BASE:
        ps = [_dot(p, p).astype(BF16) for p in ps]
        ts = [t + _dot(t, p) for t, p in zip(ts, ps)]
        n *= 2
    n = DN_BASE
    lvl = 1
    while 2 * n < DN_CHUNK:
        left = [_dot(t, a * lmask_ref[b + lvl]) for t, a, b in zip(ts, a_list, bases)]
        ts = [t - _dot(l, t) for t, l in zip(ts, left)]
        n *= 2
        lvl += 1
    t11 = [t[0:n, 0:n] for t in ts]
    t22 = [t[n:, n:] for t in ts]
    zero = jnp.zeros((n, n), F32)
    left = [_dot(x11, a[0:n, n:]) if up else _dot(x22, a[n:, 0:n])
            for x11, x22, a, up in zip(t11, t22, a_list, uppers)]
    off = [-_dot(l, x22) if up else -_dot(l, x11) for l, x11, x22, up in zip(left, t11, t22, uppers)]
    return [jnp.concatenate([jnp.concatenate([x11, o if up else zero], axis=1),
                             jnp.concatenate([zero if up else o, x22], axis=1)], axis=0)
            for x11, x22, o, up in zip(t11, t22, off, uppers)]


def _dn_kernel(q_ref, k_ref, v_ref, z_ref, cwq_ref, cwk_ref, cwv_ref, gates_ref, gt_ref, ng_ref,
               o_ref, acc_ref, qkv_a_ref, qkv_b_ref, masks_ref, lmask_ref, u_ref, wq_ref, kd_ref, p_ref,
               gl_ref, *, seq):
    h = pl.program_id(1)
    c = DN_CHUNK
    nc = seq // c
    n_lvl = 1
    while DN_BASE << n_lvl < c:
        n_lvl += 1

    r = lax.broadcasted_iota(jnp.int32, (c, c), 0)
    cc = lax.broadcasted_iota(jnp.int32, (c, c), 1)
    as_f32 = lambda m: jnp.where(m, 1.0, 0.0).astype(F32)
    eye = as_f32(r == cc)
    per_dir = 2 + n_lvl
    base_shift = DN_BASE.bit_length() - 1

    @pl.when((pl.program_id(0) == 0) & (h == 0))
    def _():
        for d, (lo, hi) in enumerate(((cc, r), (r, cc))):
            masks_ref[d * per_dir + 0] = as_f32(lo <= hi)
            masks_ref[d * per_dir + 1] = as_f32(lo < hi)
            base_mask = as_f32((lo < hi) & ((lo >> base_shift) == (hi >> base_shift)))
            masks_ref[d * per_dir + 2] = base_mask
            lmask_ref[d * per_dir + 2] = base_mask.astype(BF16)
            for lvl in range(1, n_lvl):
                sh = base_shift + lvl - 1
                lo_blk, hi_blk = lo >> sh, hi >> sh
                lmask_ref[d * per_dir + 2 + lvl] = as_f32(
                    (hi_blk == lo_blk + 1) & ((lo_blk & 1) == 0)).astype(BF16)

    n_groups = nc // DN_GROUP

    def prep_group(g, dst_ref):
        halo = DN_HALO
        n = c + 2 * halo
        for j in range(DN_GROUP):
            ci = DN_GROUP * g + j
            r0 = pl.multiple_of(ci * c, c)
            lo = pl.multiple_of(jnp.maximum(r0 - halo, 0), halo)
            hi = pl.multiple_of(jnp.minimum(r0 + c, seq - halo), halo)
            keep_lo = jnp.where(ci > 0, 1.0, 0.0)
            keep_hi = jnp.where(ci < nc - 1, 1.0, 0.0)
            for idx, (x_ref, w_ref, kind) in enumerate(((q_ref, cwq_ref, "q"), (k_ref, cwk_ref, "k"),
                                                        (v_ref, cwv_ref, "v"))):
                x = jnp.concatenate([x_ref[0, pl.ds(lo, halo), :].astype(F32) * keep_lo,
                                     x_ref[0, pl.ds(r0, c), :].astype(F32),
                                     x_ref[0, pl.ds(hi, halo), :].astype(F32) * keep_hi], axis=0)
                w = w_ref[...]
                y = (pltpu.roll(x, 1, 0) * w[0:1, :] + x * w[1:2, :]
                     + pltpu.roll(x, n - 1, 0) * w[2:3, :])[halo:halo + c]
                y = y * _sigmoid(y)
                if kind != "v":
                    y = y * lax.rsqrt(jnp.sum(y * y, axis=-1, keepdims=True) + EPS)
                if kind == "q":
                    y = y * (DN_HEAD_DIM ** -0.5)
                dst_ref[idx, j * c:(j + 1) * c, :] = y

    prep_group(jnp.int32(0), qkv_a_ref)

    lane = lax.broadcasted_iota(jnp.int32, (c, LANES), 1)

    def solve_group(g, src_ref):
        sys_, a_list, bases = [], [], []
        for j in range(DN_GROUP):
            ci = DN_GROUP * g + j
            r0 = pl.multiple_of(ci * c, c)
            q = src_ref[0, j * c:(j + 1) * c, :]
            k = src_ref[1, j * c:(j + 1) * c, :]
            v = src_ref[2, j * c:(j + 1) * c, :]
            gch = gates_ref[0, pl.ds(r0, c), :]
            kk = _dot_nt(k, k)
            qk = _dot_nt(q, k)
            for d in range(2):
                col = lambda j: jnp.sum(jnp.where(lane == j, gch, 0.0), axis=1, keepdims=True)
                gc = col(d * DN_HEADS + h)
                beta = col((2 + d) * DN_HEADS + h)
                grow = gt_ref[0, ci, pl.ds(d * DN_HEADS + h, 1), :]
                g_end = gc[0:1, :] if d else gc[c - 1:c, :]
                dec = jnp.exp(jnp.minimum(gc - grow, 0.0))
                e_g = jnp.exp(gc)
                a_list.append(kk * (beta * dec) * masks_ref[d * per_dir + 1])
                bases.append(d * per_dir + 2)
                sys_.append(dict(
                    ci=ci, r0=r0, d=d,
                    rhs=jnp.concatenate([v * beta, k * (beta * e_g)], axis=1),
                    qg=(q * e_g).astype(BF16),
                    kd=(k * jnp.exp(g_end - gc)).astype(BF16),
                    p=(qk * dec * masks_ref[d * per_dir]).astype(BF16),
                    gl=jnp.broadcast_to(jnp.exp(g_end), (1, LANES))))
        t_list = _tri_inverse(a_list, masks_ref, lmask_ref, bases, [s["d"] == 1 for s in sys_], eye)
        uw_list = [_dot(t, s["rhs"]) for t, s in zip(t_list, sys_)]
        for s, uw in zip(sys_, uw_list):
            d, ci, r0 = s["d"], s["ci"], s["r0"]
            u_ref[d, pl.ds(r0, c), :] = uw[:, 0:LANES]
            wq_ref[d, ci, 0:c, :] = uw[:, LANES:2 * LANES].astype(BF16)
            wq_ref[d, ci, c:2 * c, :] = s["qg"]
            kd_ref[d, pl.ds(r0, c), :] = s["kd"]
            p_ref[d, ci] = s["p"]
            gl_ref[d, pl.ds(ci, 1), :] = s["gl"]

    def phase_a(m, carry):
        prep_group(2 * m + 1, qkv_b_ref)
        solve_group(2 * m, qkv_a_ref)
        prep_group(jnp.minimum(2 * m + 2, n_groups - 1), qkv_a_ref)
        solve_group(2 * m + 1, qkv_b_ref)
        return carry

    lax.fori_loop(0, n_groups // 2, phase_a, 0)

    def phase_b(i, carry, second_half):
        cis = (i, nc - 1 - i)
        r0s = [pl.multiple_of(ci * c, c) for ci in cis]
        ws_qs = [jnp.dot(wq_ref[d, cis[d]], carry[d].astype(BF16), preferred_element_type=F32)
                 for d in range(2)]
        v_new = [(u_ref[d, pl.ds(r0s[d], c), :] - ws_qs[d][0:c]).astype(BF16) for d in range(2)]
        outs = [ws_qs[d][c:2 * c] + jnp.dot(p_ref[d, cis[d]], v_new[d], preferred_element_type=F32)
                for d in range(2)]
        new = [carry[d] * gl_ref[d, pl.ds(cis[d], 1), :]
               + _dot_tn(kd_ref[d, pl.ds(r0s[d], c), :], v_new[d]) for d in range(2)]
        for d in range(2):
            rows = pl.ds(r0s[d], c)
            if not second_half:
                acc_ref[rows, :] = outs[d]
            else:
                z = z_ref[0, rows, :].astype(F32)
                y = _rms(acc_ref[rows, :] + outs[d], ng_ref[...]) * (z * _sigmoid(z))
                o_ref[0, rows, :] = y.astype(o_ref.dtype)
        return tuple(new)

    zero = jnp.zeros((DN_HEAD_DIM, DN_HEAD_DIM), F32)
    mid = lax.fori_loop(0, nc // 2, functools.partial(phase_b, second_half=False), (zero, zero))
    lax.fori_loop(nc // 2, nc, functools.partial(phase_b, second_half=True), mid)


def _deltanet(qkvd, zd, gates, gt, conv_w, norm_g):
    b, s, _ = qkvd.shape
    hd = DN_HEAD_DIM
    nc = s // DN_CHUNK
    n_lvl = 1
    while DN_BASE << n_lvl < DN_CHUNK:
        n_lvl += 1
    n_masks = 2 * (2 + n_lvl)
    col = lambda off: pl.BlockSpec((1, s, hd), lambda bi, h: (bi, 0, off + h))
    cw = lambda off: pl.BlockSpec((3, hd), lambda bi, h: (0, off + h))
    return pl.pallas_call(
        functools.partial(_dn_kernel, seq=s),
        grid=(b, DN_HEADS),
        in_specs=[col(0), col(DN_HEADS), col(2 * DN_HEADS), col(0),
                  cw(0), cw(DN_HEADS), cw(2 * DN_HEADS),
                  pl.BlockSpec((1, s, LANES), lambda bi, h: (bi, 0, 0)),
                  pl.BlockSpec((1, nc, 2 * DN_HEADS, DN_CHUNK), lambda bi, h: (bi, 0, 0, 0)),
                  pl.BlockSpec((1, hd), lambda bi, h: (0, 0))],
        out_specs=pl.BlockSpec((1, s, hd), lambda bi, h: (bi, 0, h)),
        out_shape=jax.ShapeDtypeStruct((b, s, DN_W), BF16),
        scratch_shapes=[pltpu.VMEM((s, hd), F32),
                        pltpu.VMEM((3, DN_GROUP * DN_CHUNK, hd), F32),
                        pltpu.VMEM((3, DN_GROUP * DN_CHUNK, hd), F32),
                        pltpu.VMEM((n_masks, DN_CHUNK, DN_CHUNK), F32),
                        pltpu.VMEM((n_masks, DN_CHUNK, DN_CHUNK), BF16),
                        pltpu.VMEM((2, s, hd), F32),
                        pltpu.VMEM((2, nc, 2 * DN_CHUNK, hd), BF16),
                        pltpu.VMEM((2, s, hd), BF16),
                        pltpu.VMEM((2, nc, DN_CHUNK, DN_CHUNK), BF16),
                        pltpu.VMEM((2, max(nc, 8), hd), F32)],
        compiler_params=_cparams(dimension_semantics=("arbitrary", "arbitrary")),
        name="deltanet",
    )(qkvd, qkvd, qkvd, zd, conv_w, conv_w, conv_w, gates, gt, norm_g)


def _merge_kernel(o1_ref, o2_ref, o3_ref, l1_ref, l2_ref, l3_ref, yd_ref, h1_ref, x_ref, mod_ref,
                  wg_ref, wba_ref, wbd_ref, wo_ref, g2_ref, x1_ref, h2_ref):
    d = D_MODEL
    for s0 in range(0, x_ref.shape[1], MERGE_SUB):
        rows = slice(s0, s0 + MERGE_SUB)
        l1, l2, l3 = l1_ref[0, rows, :], l2_ref[0, rows, :], l3_ref[0, rows, :]
        mx = jnp.maximum(l1, jnp.maximum(l2, l3))
        e1, e2, e3 = jnp.exp(l1 - mx), jnp.exp(l2 - mx), jnp.exp(l3 - mx)
        ya = (e1 * o1_ref[0, rows, :] + e2 * o2_ref[0, rows, :] + e3 * o3_ref[0, rows, :]) / (e1 + e2 + e3)
        h1 = h1_ref[0, rows, :]
        gate_a = _sigmoid(jnp.dot(h1, wg_ref[:, 0:d], preferred_element_type=F32))
        gate_d = _sigmoid(jnp.dot(h1, wg_ref[:, d:2 * d], preferred_element_type=F32))
        merged = gate_a * _dot(ya, wba_ref[...]) + gate_d * _dot(yd_ref[0, rows, :], wbd_ref[...])
        x1 = x_ref[0, rows, :] + mod_ref[0, 2:3, :] * _dot(merged, wo_ref[...])
        x1_ref[0, rows, :] = x1
        h2 = _rms(x1, g2_ref[...]) * (1.0 + mod_ref[0, 4:5, :]) + mod_ref[0, 3:4, :]
        h2_ref[0, rows, :] = h2.astype(h2_ref.dtype)


def _merge_out(os_, ls_, yd, h1, x, mod, wg, wba, wbd, wo, g2):
    b, s, d = x.shape
    tm = 2 * MERGE_SUB
    row = lambda w: pl.BlockSpec((1, tm, w), lambda bi, i: (bi, i, 0))
    full = lambda a: pl.BlockSpec(a.shape, lambda bi, i: (0,) * a.ndim)
    return pl.pallas_call(
        _merge_kernel,
        grid=(b, s // tm),
        in_specs=[row(GROUP_W)] * 6 + [row(DN_W), row(d), row(d),
                                       pl.BlockSpec((1, 6, d), lambda bi, i: (bi, 0, 0)),
                                       full(wg), full(wba), full(wbd), full(wo), full(g2)],
        out_specs=[row(d), row(d)],
        out_shape=[jax.ShapeDtypeStruct((b, s, d), F32), jax.ShapeDtypeStruct((b, s, d), BF16)],
        compiler_params=_cparams(),
        name="merge_out",
    )(*os_, *ls_, yd, h1, x, mod, wg, wba, wbd, wo, g2)


FFN_HALO = 16
FFN_SUB = 512


def _ffn_up_kernel(hp_ref, hm_ref, hn_ref, wv_ref, wg_ref, cwv_ref, cwg_ref, bv_ref, bg_ref,
                   o_ref, *, nt):
    i = pl.program_id(1)
    tm = hm_ref.shape[1]
    lhs_all = jnp.concatenate([hp_ref[0], hm_ref[0], hn_ref[0]], axis=0)
    sub = FFN_SUB
    n = sub + 2 * FFN_HALO
    for s0 in range(0, tm, sub):
        lhs = lhs_all[s0:s0 + n]
        keep_prev = jnp.where(i > 0, 1.0, 0.0) if s0 == 0 else None
        keep_next = jnp.where(i < nt - 1, 1.0, 0.0) if s0 + sub == tm else None
        ups = [jnp.dot(lhs, w_ref[...], preferred_element_type=F32) for w_ref in (wv_ref, wg_ref)]

        def conv(up, cw_ref, b_ref):
            head, tail = up[0:FFN_HALO], up[FFN_HALO + sub:]
            if keep_prev is not None:
                head = head * keep_prev
            if keep_next is not None:
                tail = tail * keep_next
            up = jnp.concatenate([head, up[FFN_HALO:FFN_HALO + sub], tail], axis=0)
            cw = cw_ref[...]
            y = (pltpu.roll(up, 1, 0) * cw[0:1, :] + up * cw[1:2, :]
                 + pltpu.roll(up, n - 1, 0) * cw[2:3, :])
            return y[FFN_HALO:FFN_HALO + sub] + b_ref[...]

        val = conv(ups[0], cwv_ref, bv_ref)
        gate = conv(ups[1], cwg_ref, bg_ref)
        o_ref[0, s0:s0 + sub, :] = (gate * _sigmoid(gate) * val).astype(o_ref.dtype)


def _ffn_up(h2, w_up, conv_w, conv_b):
    b, s, d = h2.shape
    tm, tn = 2048, 256
    nt = s // tm
    nj = D_FF // tn
    hb = tm // FFN_HALO
    return pl.pallas_call(
        functools.partial(_ffn_up_kernel, nt=nt),
        grid=(b, nt, nj),
        in_specs=[pl.BlockSpec((1, FFN_HALO, d), lambda bi, i, j: (bi, jnp.maximum(i * hb - 1, 0), 0)),
                  pl.BlockSpec((1, tm, d), lambda bi, i, j: (bi, i, 0)),
                  pl.BlockSpec((1, FFN_HALO, d),
                               lambda bi, i, j: (bi, jnp.minimum((i + 1) * hb, s // FFN_HALO - 1), 0)),
                  pl.BlockSpec((d, tn), lambda bi, i, j: (0, j)),
                  pl.BlockSpec((d, tn), lambda bi, i, j: (0, j + nj)),
                  pl.BlockSpec((3, tn), lambda bi, i, j: (0, j)),
                  pl.BlockSpec((3, tn), lambda bi, i, j: (0, j + nj)),
                  pl.BlockSpec((1, tn), lambda bi, i, j: (0, j)),
                  pl.BlockSpec((1, tn), lambda bi, i, j: (0, j + nj))],
        out_specs=pl.BlockSpec((1, tm, tn), lambda bi, i, j: (bi, i, j)),
        out_shape=jax.ShapeDtypeStruct((b, s, D_FF), BF16),
        compiler_params=_cparams(),
        name="ffn_up",
    )(h2, h2, h2, w_up, w_up, conv_w, conv_w, conv_b, conv_b)


def _ffn_down_kernel(a_ref, w_ref, x_ref, mod_ref, g_ref, o_ref):
    for s0 in range(0, x_ref.shape[1], FFN_SUB):
        rows = slice(s0, s0 + FFN_SUB)
        x2 = x_ref[0, rows, :] + mod_ref[0, 5:6, :] * jnp.dot(a_ref[0, rows, :], w_ref[...],
                                                             preferred_element_type=F32)
        o_ref[0, rows, :] = _rms(x2, g_ref[...])


def _ffn_down(act, w_down, x1, mod, g):
    b, s, d = x1.shape
    tm = 1024
    return pl.pallas_call(
        _ffn_down_kernel,
        grid=(b, s // tm),
        in_specs=[pl.BlockSpec((1, tm, D_FF), lambda bi, i: (bi, i, 0)),
                  pl.BlockSpec((D_FF, d), lambda bi, i: (0, 0)),
                  pl.BlockSpec((1, tm, d), lambda bi, i: (bi, i, 0)),
                  pl.BlockSpec((1, 6, d), lambda bi, i: (bi, 0, 0)),
                  pl.BlockSpec((1, d), lambda bi, i: (0, 0))],
        out_specs=pl.BlockSpec((1, tm, d), lambda bi, i: (bi, i, 0)),
        out_shape=jax.ShapeDtypeStruct((b, s, d), F32),
        compiler_params=_cparams(),
        name="ffn_down",
    )(act, w_down, x1, mod, g)


def _qk_column_order():
    half = HEAD_DIM // 2
    return [(pair * 2 + j) * HEAD_DIM + f * half + i
            for pair in range(GROUP_W // LANES) for f in range(2) for j in range(2)
            for i in range(half)]


_QK_COLS = np.asarray(_qk_column_order(), np.int32)


def _trunk(x, mod, p, rope):
    b, s, _ = x.shape
    h1 = _norm_mod(x, mod, p["norm1_g"])
    qkvd = _mm(h1, p["w_qkvd"], BF16)
    zd = _mm(h1, p["w_zd"], BF16)
    gates, gt = _gates(h1, p["w_ab"], p["a_row"], p["dt_row"])
    os_, ls_ = [], []
    for gi, (_, dil) in enumerate(ATTN_GROUPS):
        o, l = _attn(_proj_attn(h1, p["w_attn"][gi], rope, dil, gi), gi, dil)
        os_.append(o)
        ls_.append(l)
    yd = _deltanet(qkvd, zd, gates, gt, p["conv_qkv_w"], p["dn_norm_g"])
    x1, h2 = _merge_out(os_, ls_, yd, h1, x, mod, p["w_gate"], p["w_br_attn"], p["w_br_dn"],
                        p["w_out"], p["norm2_g"])
    act = _ffn_up(h2, p["w_up"], p["ffn_conv_w"], p["ffn_conv_b"])
    return _ffn_down(act, p["w_down"], x1, mod, p["norm_f_g"])


def kernel(x_prompt, x_sample, c_prompt, c_sample, w_ada, b_ada, norm1_g, w_in, conv_qkv_w, a_log, dt_bias, dn_norm_g, w_br_attn, w_br_dn, w_out, norm2_g, w_up, ffn_conv_w, ffn_conv_b, w_down, norm_f_g):
    d = D_MODEL
    assert w_ada.shape[0] == 1, "single layer"
    w = w_in[0]
    o_qd = 3 * ATTN_W
    o_zd = o_qd + 3 * DN_W
    o_ab = o_zd + DN_W
    o_gate = o_ab + 4 * DN_HEADS
    pad16 = lambda v: jnp.pad(v.reshape(1, 2 * DN_HEADS).astype(F32), ((0, 0), (0, LANES - 2 * DN_HEADS)))
    p = {
        "norm1_g": norm1_g[0].reshape(1, d),
        "w_attn": [jnp.concatenate(
            [w[:, kind * ATTN_W + gi * GROUP_W:kind * ATTN_W + (gi + 1) * GROUP_W][:, cols]
             for kind, cols in ((0, _QK_COLS), (1, _QK_COLS), (2, slice(None)))], axis=1).astype(BF16)
            for gi in range(N_GROUPS)],
        "w_qkvd": w[:, o_qd:o_zd].astype(BF16),
        "w_zd": w[:, o_zd:o_ab].astype(BF16),
        "w_ab": jnp.pad(w[:, o_ab:o_gate], ((0, 0), (0, LANES - 4 * DN_HEADS))).astype(BF16),
        "w_gate": w[:, o_gate:].astype(BF16),
        "a_row": pad16(a_log[0]),
        "dt_row": pad16(dt_bias[0]),
        "conv_qkv_w": conv_qkv_w[0],
        "dn_norm_g": dn_norm_g[0].reshape(1, DN_HEAD_DIM),
        "w_br_attn": w_br_attn[0].astype(BF16),
        "w_br_dn": w_br_dn[0].astype(BF16),
        "w_out": w_out[0].astype(BF16),
        "norm2_g": norm2_g[0].reshape(1, d),
        "w_up": w_up[0].astype(BF16),
        "ffn_conv_w": ffn_conv_w[0],
        "ffn_conv_b": ffn_conv_b[0].reshape(1, 2 * D_FF),
        "w_down": w_down[0].astype(BF16),
        "norm_f_g": norm_f_g.reshape(1, d),
    }
    nb = x_prompt.shape[0]
    mod = _ada(jnp.concatenate([c_prompt, c_sample], axis=0), w_ada[0], b_ada[0].reshape(1, 6 * d))
    mod = mod.reshape(-1, 6, d)
    rope = _rope_tables(max(x_prompt.shape[1], x_sample.shape[1]))
    y_prompt = _trunk(x_prompt, mod[:nb], p, rope)
    y_sample = _trunk(x_sample, mod[nb:], p, rope)
    return (y_prompt, y_sample)
```

```python
import functools

import jax
import jax.numpy as jnp
import numpy as np
from jax import lax
from jax.experimental import pallas as pl
from jax.experimental.pallas import tpu as pltpu

F32 = jnp.float32
BF16 = jnp.bfloat16
HIGHEST = lax.Precision.HIGHEST

D_MODEL = 1024
ATTN_GROUPS = ((128, 1), (512, 4), (2048, 16))
N_GROUPS = 3
HEAD_DIM = 64
GROUP_W = 512
ATTN_W = N_GROUPS * GROUP_W
ATTN_HALF = 64
ATTN_BQ = 128
ATTN_MAX_BLOCKS = 4
ATTN_MAX_ROWS = 2048
ROPE_THETA = 10000.0
NEG = -1e30
DN_HEADS = 8
DN_HEAD_DIM = 128
DN_W = DN_HEADS * DN_HEAD_DIM
DN_CHUNK = 256
DN_BASE = 16
DN_GROUP = 2
DN_HALO = 16
DN_B_UNROLL = 4
MM_SUB = 256
MERGE_SUB = 256
D_FF = 2816
EPS = 1e-6
LANES = 128
VMEM_LIMIT = 56 * 1024 * 1024


def _cparams(**kw):
    return pltpu.CompilerParams(vmem_limit_bytes=VMEM_LIMIT, **kw)


def _sigmoid(x):
    return 1.0 / (1.0 + jnp.exp(-x))


def _dot(a, b):
    return jnp.dot(a.astype(BF16), b.astype(BF16), preferred_element_type=F32)


def _dot_nt(a, b):
    return lax.dot_general(a.astype(BF16), b.astype(BF16), (((1,), (1,)), ((), ())),
                           preferred_element_type=F32)


def _dot_tn(a, b):
    return lax.dot_general(a.astype(BF16), b.astype(BF16), (((0,), (0,)), ((), ())),
                           preferred_element_type=F32)


def _rms(x, g):
    return x * lax.rsqrt(jnp.mean(x * x, axis=-1, keepdims=True) + EPS) * g


def _ada_kernel(c_ref, w_ref, b_ref, o_ref):
    c = c_ref[...]
    s = c * _sigmoid(c)
    o_ref[...] = jnp.dot(s, w_ref[...], preferred_element_type=F32, precision=HIGHEST) + b_ref[...]


def _ada(c, w, b):
    bt, d = c.shape
    n = w.shape[1]
    tn = 1024
    return pl.pallas_call(
        _ada_kernel,
        grid=(n // tn,),
        in_specs=[pl.BlockSpec((bt, d), lambda j: (0, 0)),
                  pl.BlockSpec((d, tn), lambda j: (0, j)),
                  pl.BlockSpec((1, tn), lambda j: (0, j))],
        out_specs=pl.BlockSpec((bt, tn), lambda j: (0, j)),
        out_shape=jax.ShapeDtypeStruct((bt, n), F32),
        name="ada",
    )(c, w, b)


def _rope_table_kernel(inv_ref, cos_ref, sin_ref):
    tm = cos_ref.shape[0]
    pos = (pl.program_id(0) * tm + lax.broadcasted_iota(jnp.int32, (tm, LANES), 0)).astype(F32)
    ang = pos * inv_ref[...]
    lane = lax.broadcasted_iota(jnp.int32, (tm, LANES), 1)
    cos_ref[...] = jnp.cos(ang)
    sin_ref[...] = jnp.sin(ang) * jnp.where(lane < LANES // 2, -1.0, 1.0)


def _rope_tables(s):
    half = HEAD_DIM // 2
    inv = ROPE_THETA ** (-(jnp.arange(LANES) % half).astype(F32) / half)
    tm = 512
    return pl.pallas_call(
        _rope_table_kernel,
        grid=(s // tm,),
        in_specs=[pl.BlockSpec((1, LANES), lambda i: (0, 0))],
        out_specs=[pl.BlockSpec((tm, LANES), lambda i: (i, 0))] * 2,
        out_shape=[jax.ShapeDtypeStruct((s, LANES), F32)] * 2,
        name="rope_tables",
    )(inv.reshape(1, LANES))


def _norm_mod_kernel(x_ref, mod_ref, g_ref, o_ref):
    h = _rms(x_ref[0], g_ref[...]) * (1.0 + mod_ref[0, 1:2, :]) + mod_ref[0, 0:1, :]
    o_ref[0] = h.astype(o_ref.dtype)


def _norm_mod(x, mod, g):
    b, s, d = x.shape
    tm = 512
    return pl.pallas_call(
        _norm_mod_kernel,
        grid=(b, s // tm),
        in_specs=[pl.BlockSpec((1, tm, d), lambda bi, i: (bi, i, 0)),
                  pl.BlockSpec((1, 6, d), lambda bi, i: (bi, 0, 0)),
                  pl.BlockSpec((1, d), lambda bi, i: (0, 0))],
        out_specs=pl.BlockSpec((1, tm, d), lambda bi, i: (bi, i, 0)),
        out_shape=jax.ShapeDtypeStruct((b, s, d), BF16),
        name="norm_mod",
    )(x, mod, g)


def _mm_kernel(h_ref, w_ref, o_ref):
    o_ref[0] = jnp.dot(h_ref[0], w_ref[...], preferred_element_type=F32).astype(o_ref.dtype)


def _mm(h, w, out_dtype):
    b, s, k = h.shape
    n = w.shape[1]
    tm, tn = 1024, 1024
    return pl.pallas_call(
        _mm_kernel,
        grid=(b, s // tm, n // tn),
        in_specs=[pl.BlockSpec((1, tm, k), lambda bi, i, j: (bi, i, 0)),
                  pl.BlockSpec((k, tn), lambda bi, i, j: (0, j))],
        out_specs=pl.BlockSpec((1, tm, tn), lambda bi, i, j: (bi, i, j)),
        out_shape=jax.ShapeDtypeStruct((b, s, n), out_dtype),
        compiler_params=_cparams(),
        name="proj",
    )(h, w)


def _proj_attn_kernel(h_ref, w_ref, cos_ref, sin_ref, o_ref, *scratch, dil):
    is_qk = pl.program_id(2) < 2
    tm = h_ref.shape[1]
    n_slab = w_ref.shape[1] // LANES
    for s0 in range(0, tm, MM_SUB):
        rows = slice(s0, s0 + MM_SUB)
        acc = jnp.dot(h_ref[0, rows, :], w_ref[...], preferred_element_type=F32)
        cos = jnp.where(is_qk, cos_ref[rows, :], 1.0)
        sin = jnp.where(is_qk, sin_ref[rows, :], 0.0)
        for c in range(n_slab):
            t = acc[:, c * LANES:(c + 1) * LANES]
            val = t * cos + pltpu.roll(t, LANES // 2, 1) * sin
            if dil == 1:
                o_ref[0, 0, 0, rows, c * LANES:(c + 1) * LANES] = val.astype(o_ref.dtype)
            else:
                scratch[0][c, rows, :] = val
        if dil > 1:
            n = MM_SUB // dil
            m0 = s0 // dil
            for r in range(dil):
                for c in range(n_slab):
                    o_ref[0, 0, r, m0:m0 + n, c * LANES:(c + 1) * LANES] = (
                        scratch[0][c, pl.ds(s0 + r, n, stride=dil), :].astype(o_ref.dtype))


def _proj_attn(h, w, rope, dil, gi):
    b, s, k = h.shape
    tm, tn = 2048, GROUP_W
    scratch = [pltpu.VMEM((tn // LANES, tm, LANES), F32)] if dil > 1 else []
    return pl.pallas_call(
        functools.partial(_proj_attn_kernel, dil=dil),
        grid=(b, s // tm, 3),
        in_specs=[pl.BlockSpec((1, tm, k), lambda bi, i, j: (bi, i, 0)),
                  pl.BlockSpec((k, tn), lambda bi, i, j: (0, j)),
                  pl.BlockSpec((tm, LANES), lambda bi, i, j: (i, 0)),
                  pl.BlockSpec((tm, LANES), lambda bi, i, j: (i, 0))],
        out_specs=pl.BlockSpec((1, 1, dil, tm // dil, tn), lambda bi, i, j: (bi, j, 0, i, 0)),
        out_shape=jax.ShapeDtypeStruct((b, 3, dil, s // dil, tn), BF16),
        scratch_shapes=scratch,
        compiler_params=_cparams(),
        name=f"proj_attn_g{gi}",
    )(h, w, *rope)


def _gates_kernel(h_ref, w_ref, a_ref, dt_ref, o_ref, gt_ref):
    c = DN_CHUNK
    r = lax.broadcasted_iota(jnp.int32, (c, c), 0)
    cc = lax.broadcasted_iota(jnp.int32, (c, c), 1)
    lower = jnp.where(cc <= r, 1.0, 0.0).astype(F32)
    upper = jnp.where(cc >= r, 1.0, 0.0).astype(F32)
    lane = lax.broadcasted_iota(jnp.int32, (c, LANES), 1)
    for j in range(h_ref.shape[1] // c):
        acc = jnp.dot(h_ref[0, j * c:(j + 1) * c, :], w_ref[...], preferred_element_type=F32)
        x = acc + dt_ref[...]
        softplus = jnp.maximum(x, 0.0) + jnp.log1p(jnp.exp(-jnp.abs(x)))
        g = -jnp.exp(a_ref[...]) * softplus
        beta = _sigmoid(acc)
        pre = jnp.dot(lower, g, preferred_element_type=F32, precision=HIGHEST)
        suf = jnp.dot(upper, g, preferred_element_type=F32, precision=HIGHEST)
        out = jnp.where(lane < DN_HEADS, pre,
                        jnp.where(lane < 2 * DN_HEADS, suf,
                                  jnp.where(lane < 4 * DN_HEADS, beta, 0.0)))
        o_ref[0, j * c:(j + 1) * c, :] = out
        gt_ref[0, j] = out.T[0:2 * DN_HEADS, :]


def _gates(h, w, a_row, dt_row):
    b, s, k = h.shape
    tm = 4 * DN_CHUNK
    return pl.pallas_call(
        _gates_kernel,
        grid=(b, s // tm),
        in_specs=[pl.BlockSpec((1, tm, k), lambda bi, i: (bi, i, 0)),
                  pl.BlockSpec((k, LANES), lambda bi, i: (0, 0)),
                  pl.BlockSpec((1, LANES), lambda bi, i: (0, 0)),
                  pl.BlockSpec((1, LANES), lambda bi, i: (0, 0))],
        out_specs=[pl.BlockSpec((1, tm, LANES), lambda bi, i: (bi, i, 0)),
                   pl.BlockSpec((1, tm // DN_CHUNK, 2 * DN_HEADS, DN_CHUNK),
                                lambda bi, i: (bi, i, 0, 0))],
        out_shape=[jax.ShapeDtypeStruct((b, s, LANES), F32),
                   jax.ShapeDtypeStruct((b, s // DN_CHUNK, 2 * DN_HEADS, DN_CHUNK), F32)],
        name="dn_gates",
    )(h, w, a_row, dt_row)


def _attn_kernel(q_ref, k_ref, v_ref, o_ref, l_ref, *scratch, seq, bq, nq, kb, dil):
    n_slab = GROUP_W // LANES
    lane = lax.broadcasted_iota(jnp.int32, (bq, LANES), 1)
    blocks = []
    for qb in range(nq):
        row0 = qb * bq
        q0 = (pl.program_id(1) * nq + qb) * bq
        ks = pl.multiple_of(jnp.clip(q0 - ATTN_HALF, 0, seq - kb), ATTN_HALF)
        qpos = q0 + lax.broadcasted_iota(jnp.int32, (bq, kb), 0)
        kpos = ks + lax.broadcasted_iota(jnp.int32, (bq, kb), 1)
        valid = jnp.abs(qpos - kpos) <= ATTN_HALF

        def residue(r, row0=row0, ks=ks, valid=valid):
            q = q_ref[0, 0, r, row0:row0 + bq, :].astype(F32) * (HEAD_DIM ** -0.5)
            k = k_ref[0, 0, r, pl.ds(ks, kb), :]
            v = v_ref[0, 0, r, pl.ds(ks, kb), :]
            heads = [(hp, sub) for hp in range(n_slab) for sub in range(2)]
            in_head = [lane < HEAD_DIM, lane >= HEAD_DIM]
            qk_lanes = [(lane & (HEAD_DIM // 2)) == 0, (lane & (HEAD_DIM // 2)) != 0]
            slab = lambda x, hp: x[:, hp * LANES:(hp + 1) * LANES]
            s_ = [_dot_nt(jnp.where(qk_lanes[sub], slab(q, hp), 0.0), slab(k, hp)) for hp, sub in heads]
            s_ = [jnp.where(valid, s, NEG) for s in s_]
            mx_ = [jnp.max(s, axis=-1, keepdims=True) for s in s_]
            p_ = [jnp.exp(s - mx) for s, mx in zip(s_, mx_)]
            den_ = [jnp.sum(p, axis=-1, keepdims=True) for p in p_]
            o_ = [_dot(p, slab(v, hp)) / den for p, den, (hp, _) in zip(p_, den_, heads)]
            lse_ = [mx + jnp.log(den) for mx, den in zip(mx_, den_)]
            for hp in range(n_slab):
                sl = slice(hp * LANES, (hp + 1) * LANES)
                o_pair = jnp.where(in_head[1], o_[2 * hp + 1], o_[2 * hp])
                l_pair = jnp.where(in_head[1], lse_[2 * hp + 1], lse_[2 * hp])
                if dil == 1:
                    o_ref[0, row0:row0 + bq, sl] = o_pair.astype(o_ref.dtype)
                    l_ref[0, row0:row0 + bq, sl] = l_pair
                else:
                    scratch[0][hp, pl.ds(row0 * dil + r, bq, stride=dil), :] = o_pair
                    scratch[1][hp, pl.ds(row0 * dil + r, bq, stride=dil), :] = l_pair

        blocks.append(residue)

    if dil == 1:
        for block in blocks:
            block(0)
    else:
        def body(r, carry):
            for block in blocks:
                block(r)
            return carry
        lax.fori_loop(0, dil, body, 0, unroll=max(1, ATTN_MAX_BLOCKS // nq))
    if dil > 1:
        for hp in range(n_slab):
            sl = slice(hp * LANES, (hp + 1) * LANES)
            o_ref[0, :, sl] = scratch[0][hp].astype(o_ref.dtype)
            l_ref[0, :, sl] = scratch[1][hp]


def _attn(qkv, gi, dil):
    b, _, _, seq, _ = qkv.shape
    s = seq * dil
    bq = min(ATTN_BQ, seq)
    kb = min(bq + 2 * ATTN_HALF, seq)
    nq = 1
    while 2 * nq <= ATTN_MAX_BLOCKS and 2 * nq * bq * dil <= ATTN_MAX_ROWS and seq % (2 * nq * bq) == 0:
        nq *= 2
    rows = nq * bq * dil
    scratch = [pltpu.VMEM((GROUP_W // LANES, rows, LANES), F32)] * 2 if dil > 1 else []
    kv = lambda kind: pl.BlockSpec((1, 1, dil, seq, GROUP_W), lambda bi, m: (bi, kind, 0, 0, 0))
    return pl.pallas_call(
        functools.partial(_attn_kernel, seq=seq, bq=bq, nq=nq, kb=kb, dil=dil),
        grid=(b, seq // (nq * bq)),
        in_specs=[pl.BlockSpec((1, 1, dil, nq * bq, GROUP_W), lambda bi, m: (bi, 0, 0, m, 0)),
                  kv(1), kv(2)],
        out_specs=[pl.BlockSpec((1, rows, GROUP_W), lambda bi, m: (bi, m, 0))] * 2,
        out_shape=[jax.ShapeDtypeStruct((b, s, GROUP_W), BF16),
                   jax.ShapeDtypeStruct((b, s, GROUP_W), F32)],
        scratch_shapes=scratch,
        compiler_params=_cparams(),
        name=f"attn_g{gi}",
    )(qkv, qkv, qkv)


def _tri_inverse(a_list, masks_ref, lmask_ref, bases, uppers, eye):
    ts = [eye - a * masks_ref[b] for a, b in zip(a_list, bases)]
    a_list = [a.astype(BF16) for a in a_list]
    ps = [-(a * lmask_ref[b]) for a, b in zip(a_list, bases)]
    n = 2
    while n < DN_BASE:
        ps = [_dot(p, p).astype(BF16) for p in ps]
        ts = [t + _dot(t, p) for t, p in zip(ts, ps)]
        n *= 2
    n = DN_BASE
    lvl = 1
    while 2 * n < DN_CHUNK:
        left = [_dot(t, a * lmask_ref[b + lvl]) for t, a, b in zip(ts, a_list, bases)]
        ts = [t - _dot(l, t) for t, l in zip(ts, left)]
        n *= 2
        lvl += 1
    t11 = [t[0:n, 0:n] for t in ts]
    t22 = [t[n:, n:] for t in ts]
    zero = jnp.zeros((n, n), F32)
    left = [_dot(x11, a[0:n, n:]) if up else _dot(x22, a[n:, 0:n])
            for x11, x22, a, up in zip(t11, t22, a_list, uppers)]
    off = [-_dot(l, x22) if up else -_dot(l, x11) for l, x11, x22, up in zip(left, t11, t22, uppers)]
    return [jnp.concatenate([jnp.concatenate([x11, o if up else zero], axis=1),
                             jnp.concatenate([zero if up else o, x22], axis=1)], axis=0)
            for x11, x22, o, up in zip(t11, t22, off, uppers)]


def _dn_kernel(q_ref, k_ref, v_ref, z_ref, cwq_ref, cwk_ref, cwv_ref, gates_ref, gt_ref, ng_ref,
               o_ref, acc_ref, qkv_a_ref, qkv_b_ref, masks_ref, lmask_ref, u_ref, wq_ref, cs_ref, ns_ref, p_ref,
               gl_ref, *, seq):
    h = pl.program_id(1)
    c = DN_CHUNK
    nc = seq // c
    n_lvl = 1
    while DN_BASE << n_lvl < c:
        n_lvl += 1

    r = lax.broadcasted_iota(jnp.int32, (c, c), 0)
    cc = lax.broadcasted_iota(jnp.int32, (c, c), 1)
    as_f32 = lambda m: jnp.where(m, 1.0, 0.0).astype(F32)
    eye = as_f32(r == cc)
    per_dir = 2 + n_lvl
    base_shift = DN_BASE.bit_length() - 1

    @pl.when((pl.program_id(0) == 0) & (h == 0))
    def _():
        for d, (lo, hi) in enumerate(((cc, r), (r, cc))):
            masks_ref[d * per_dir + 0] = as_f32(lo <= hi)
            masks_ref[d * per_dir + 1] = as_f32(lo < hi)
            base_mask = as_f32((lo < hi) & ((lo >> base_shift) == (hi >> base_shift)))
            masks_ref[d * per_dir + 2] = base_mask
            lmask_ref[d * per_dir + 2] = base_mask.astype(BF16)
            for lvl in range(1, n_lvl):
                sh = base_shift + lvl - 1
                lo_blk, hi_blk = lo >> sh, hi >> sh
                lmask_ref[d * per_dir + 2 + lvl] = as_f32(
                    (hi_blk == lo_blk + 1) & ((lo_blk & 1) == 0)).astype(BF16)

    n_groups = nc // DN_GROUP

    def prep_group(g, dst_ref):
        halo = DN_HALO
        n = c + 2 * halo
        for j in range(DN_GROUP):
            ci = DN_GROUP * g + j
            r0 = pl.multiple_of(ci * c, c)
            lo = pl.multiple_of(jnp.maximum(r0 - halo, 0), halo)
            hi = pl.multiple_of(jnp.minimum(r0 + c, seq - halo), halo)
            keep_lo = jnp.where(ci > 0, 1.0, 0.0)
            keep_hi = jnp.where(ci < nc - 1, 1.0, 0.0)
            for idx, (x_ref, w_ref, kind) in enumerate(((q_ref, cwq_ref, "q"), (k_ref, cwk_ref, "k"),
                                                        (v_ref, cwv_ref, "v"))):
                x = jnp.concatenate([x_ref[0, pl.ds(lo, halo), :].astype(F32) * keep_lo,
                                     x_ref[0, pl.ds(r0, c), :].astype(F32),
                                     x_ref[0, pl.ds(hi, halo), :].astype(F32) * keep_hi], axis=0)
                w = w_ref[...]
                y = (pltpu.roll(x, 1, 0) * w[0:1, :] + x * w[1:2, :]
                     + pltpu.roll(x, n - 1, 0) * w[2:3, :])[halo:halo + c]
                y = y * _sigmoid(y)
                if kind != "v":
                    y = y * lax.rsqrt(jnp.sum(y * y, axis=-1, keepdims=True) + EPS)
                if kind == "q":
                    y = y * (DN_HEAD_DIM ** -0.5)
                dst_ref[idx, j * c:(j + 1) * c, :] = y

    prep_group(jnp.int32(0), qkv_a_ref)

    lane = lax.broadcasted_iota(jnp.int32, (c, LANES), 1)

    def solve_group(g, src_ref):
        sys_, a_list, bases = [], [], []
        for j in range(DN_GROUP):
            ci = DN_GROUP * g + j
            r0 = pl.multiple_of(ci * c, c)
            q = src_ref[0, j * c:(j + 1) * c, :]
            k = src_ref[1, j * c:(j + 1) * c, :]
            v = src_ref[2, j * c:(j + 1) * c, :]
            gch = gates_ref[0, pl.ds(r0, c), :]
            kk = _dot_nt(k, k)
            qk = _dot_nt(q, k)
            for d in range(2):
                col = lambda j: jnp.sum(jnp.where(lane == j, gch, 0.0), axis=1, keepdims=True)
                gc = col(d * DN_HEADS + h)
                beta = col((2 + d) * DN_HEADS + h)
                grow = gt_ref[0, ci, pl.ds(d * DN_HEADS + h, 1), :]
                g_end = gc[0:1, :] if d else gc[c - 1:c, :]
                dec = jnp.exp(jnp.minimum(gc - grow, 0.0))
                e_g = jnp.exp(gc)
                a_list.append(kk * (beta * dec) * masks_ref[d * per_dir + 1])
                bases.append(d * per_dir + 2)
                sys_.append(dict(
                    ci=ci, r0=r0, d=d,
                    rhs=jnp.concatenate([v * beta, k * (beta * e_g)], axis=1),
                    qg=(q * e_g).astype(BF16),
                    kd_t=(k * jnp.exp(g_end - gc)).T.astype(BF16),
                    p=(qk * dec * masks_ref[d * per_dir]).astype(BF16),
                    gl=jnp.broadcast_to(jnp.exp(g_end), (1, LANES))))
        t_list = _tri_inverse(a_list, masks_ref, lmask_ref, bases, [s["d"] == 1 for s in sys_], eye)
        uw_list = [_dot(t, s["rhs"]) for t, s in zip(t_list, sys_)]
        nc_list = [_dot(s["kd_t"], uw) for s, uw in zip(sys_, uw_list)]
        for s, uw, n_c in zip(sys_, uw_list, nc_list):
            d, ci, r0 = s["d"], s["ci"], s["r0"]
            u_ref[d, pl.ds(r0, c), :] = uw[:, 0:LANES]
            wq_ref[d, ci, 0:c, :] = uw[:, LANES:2 * LANES].astype(BF16)
            wq_ref[d, ci, c:2 * c, :] = s["qg"]
            ns_ref[d, ci] = n_c[:, 0:LANES]
            cs_ref[d, ci] = n_c[:, LANES:2 * LANES].astype(BF16)
            p_ref[d, ci] = s["p"]
            gl_ref[d, pl.ds(ci, 1), :] = s["gl"]

    def phase_a(m, carry):
        prep_group(2 * m + 1, qkv_b_ref)
        solve_group(2 * m, qkv_a_ref)
        prep_group(jnp.minimum(2 * m + 2, n_groups - 1), qkv_a_ref)
        solve_group(2 * m + 1, qkv_b_ref)
        return carry

    lax.fori_loop(0, n_groups // 2, phase_a, 0)

    def phase_b(i, carry, second_half):
        cis = (i, nc - 1 - i)
        r0s = [pl.multiple_of(ci * c, c) for ci in cis]
        s16 = [carry[d].astype(BF16) for d in range(2)]
        new = [carry[d] * gl_ref[d, pl.ds(cis[d], 1), :] + ns_ref[d, cis[d]]
               - jnp.dot(cs_ref[d, cis[d]], s16[d], preferred_element_type=F32) for d in range(2)]
        ws_qs = [jnp.dot(wq_ref[d, cis[d]], s16[d], preferred_element_type=F32) for d in range(2)]
        v_new = [(u_ref[d, pl.ds(r0s[d], c), :] - ws_qs[d][0:c]).astype(BF16) for d in range(2)]
        outs = [ws_qs[d][c:2 * c] + jnp.dot(p_ref[d, cis[d]], v_new[d], preferred_element_type=F32)
                for d in range(2)]
        for d in range(2):
            rows = pl.ds(r0s[d], c)
            if not second_half:
                acc_ref[rows, :] = outs[d]
            else:
                z = z_ref[0, rows, :].astype(F32)
                y = _rms(acc_ref[rows, :] + outs[d], ng_ref[...]) * (z * _sigmoid(z))
                o_ref[0, rows, :] = y.astype(o_ref.dtype)
        return tuple(new)

    zero = jnp.zeros((DN_HEAD_DIM, DN_HEAD_DIM), F32)
    mid = lax.fori_loop(0, nc // 2, functools.partial(phase_b, second_half=False), (zero, zero),
                        unroll=DN_B_UNROLL)
    lax.fori_loop(nc // 2, nc, functools.partial(phase_b, second_half=True), mid, unroll=DN_B_UNROLL)


def _deltanet(qkvd, zd, gates, gt, conv_w, norm_g):
    b, s, _ = qkvd.shape
    hd = DN_HEAD_DIM
    nc = s // DN_CHUNK
    n_lvl = 1
    while DN_BASE << n_lvl < DN_CHUNK:
        n_lvl += 1
    n_masks = 2 * (2 + n_lvl)
    col = lambda off: pl.BlockSpec((1, s, hd), lambda bi, h: (bi, 0, off + h))
    cw = lambda off: pl.BlockSpec((3, hd), lambda bi, h: (0, off + h))
    return pl.pallas_call(
        functools.partial(_dn_kernel, seq=s),
        grid=(b, DN_HEADS),
        in_specs=[col(0), col(DN_HEADS), col(2 * DN_HEADS), col(0),
                  cw(0), cw(DN_HEADS), cw(2 * DN_HEADS),
                  pl.BlockSpec((1, s, LANES), lambda bi, h: (bi, 0, 0)),
                  pl.BlockSpec((1, nc, 2 * DN_HEADS, DN_CHUNK), lambda bi, h: (bi, 0, 0, 0)),
                  pl.BlockSpec((1, hd), lambda bi, h: (0, 0))],
        out_specs=pl.BlockSpec((1, s, hd), lambda bi, h: (bi, 0, h)),
        out_shape=jax.ShapeDtypeStruct((b, s, DN_W), BF16),
        scratch_shapes=[pltpu.VMEM((s, hd), F32),
                        pltpu.VMEM((3, DN_GROUP * DN_CHUNK, hd), F32),
                        pltpu.VMEM((3, DN_GROUP * DN_CHUNK, hd), F32),
                        pltpu.VMEM((n_masks, DN_CHUNK, DN_CHUNK), F32),
                        pltpu.VMEM((n_masks, DN_CHUNK, DN_CHUNK), BF16),
                        pltpu.VMEM((2, s, hd), F32),
                        pltpu.VMEM((2, nc, 2 * DN_CHUNK, hd), BF16),
                        pltpu.VMEM((2, nc, hd, hd), BF16),
                        pltpu.VMEM((2, nc, hd, hd), F32),
                        pltpu.VMEM((2, nc, DN_CHUNK, DN_CHUNK), BF16),
                        pltpu.VMEM((2, max(nc, 8), hd), F32)],
        compiler_params=_cparams(dimension_semantics=("arbitrary", "arbitrary")),
        name="deltanet",
    )(qkvd, qkvd, qkvd, zd, conv_w, conv_w, conv_w, gates, gt, norm_g)


def _merge_kernel(o1_ref, o2_ref, o3_ref, l1_ref, l2_ref, l3_ref, yd_ref, h1_ref, x_ref, mod_ref,
                  wg_ref, wba_ref, wbd_ref, wo_ref, g2_ref, x1_ref, h2_ref):
    d = D_MODEL
    for s0 in range(0, x_ref.shape[1], MERGE_SUB):
        rows = slice(s0, s0 + MERGE_SUB)
        l1, l2, l3 = l1_ref[0, rows, :], l2_ref[0, rows, :], l3_ref[0, rows, :]
        mx = jnp.maximum(l1, jnp.maximum(l2, l3))
        e1, e2, e3 = jnp.exp(l1 - mx), jnp.exp(l2 - mx), jnp.exp(l3 - mx)
        ya = (e1 * o1_ref[0, rows, :] + e2 * o2_ref[0, rows, :] + e3 * o3_ref[0, rows, :]) / (e1 + e2 + e3)
        h1 = h1_ref[0, rows, :]
        gate_a = _sigmoid(jnp.dot(h1, wg_ref[:, 0:d], preferred_element_type=F32))
        gate_d = _sigmoid(jnp.dot(h1, wg_ref[:, d:2 * d], preferred_element_type=F32))
        merged = gate_a * _dot(ya, wba_ref[...]) + gate_d * _dot(yd_ref[0, rows, :], wbd_ref[...])
        x1 = x_ref[0, rows, :] + mod_ref[0, 2:3, :] * _dot(merged, wo_ref[...])
        x1_ref[0, rows, :] = x1
        h2 = _rms(x1, g2_ref[...]) * (1.0 + mod_ref[0, 4:5, :]) + mod_ref[0, 3:4, :]
        h2_ref[0, rows, :] = h2.astype(h2_ref.dtype)


def _merge_out(os_, ls_, yd, h1, x, mod, wg, wba, wbd, wo, g2):
    b, s, d = x.shape
    tm = 2 * MERGE_SUB
    row = lambda w: pl.BlockSpec((1, tm, w), lambda bi, i: (bi, i, 0))
    full = lambda a: pl.BlockSpec(a.shape, lambda bi, i: (0,) * a.ndim)
    return pl.pallas_call(
        _merge_kernel,
        grid=(b, s // tm),
        in_specs=[row(GROUP_W)] * 6 + [row(DN_W), row(d), row(d),
                                       pl.BlockSpec((1, 6, d), lambda bi, i: (bi, 0, 0)),
                                       full(wg), full(wba), full(wbd), full(wo), full(g2)],
        out_specs=[row(d), row(d)],
        out_shape=[jax.ShapeDtypeStruct((b, s, d), F32), jax.ShapeDtypeStruct((b, s, d), BF16)],
        compiler_params=_cparams(),
        name="merge_out",
    )(*os_, *ls_, yd, h1, x, mod, wg, wba, wbd, wo, g2)


FFN_HALO = 16
FFN_SUB = 512


def _ffn_up_kernel(hp_ref, hm_ref, hn_ref, wv_ref, wg_ref, cwv_ref, cwg_ref, bv_ref, bg_ref,
                   o_ref, *, nt):
    i = pl.program_id(1)
    tm = hm_ref.shape[1]
    lhs_all = jnp.concatenate([hp_ref[0], hm_ref[0], hn_ref[0]], axis=0)
    sub = FFN_SUB
    n = sub + 2 * FFN_HALO
    for s0 in range(0, tm, sub):
        lhs = lhs_all[s0:s0 + n]
        keep_prev = jnp.where(i > 0, 1.0, 0.0) if s0 == 0 else None
        keep_next = jnp.where(i < nt - 1, 1.0, 0.0) if s0 + sub == tm else None
        ups = [jnp.dot(lhs, w_ref[...], preferred_element_type=F32) for w_ref in (wv_ref, wg_ref)]

        def conv(up, cw_ref, b_ref):
            head, tail = up[0:FFN_HALO], up[FFN_HALO + sub:]
            if keep_prev is not None:
                head = head * keep_prev
            if keep_next is not None:
                tail = tail * keep_next
            up = jnp.concatenate([head, up[FFN_HALO:FFN_HALO + sub], tail], axis=0)
            cw = cw_ref[...]
            y = (pltpu.roll(up, 1, 0) * cw[0:1, :] + up * cw[1:2, :]
                 + pltpu.roll(up, n - 1, 0) * cw[2:3, :])
            return y[FFN_HALO:FFN_HALO + sub] + b_ref[...]

        val = conv(ups[0], cwv_ref, bv_ref)
        gate = conv(ups[1], cwg_ref, bg_ref)
        o_ref[0, s0:s0 + sub, :] = (gate * _sigmoid(gate) * val).astype(o_ref.dtype)


def _ffn_up(h2, w_up, conv_w, conv_b):
    b, s, d = h2.shape
    tm, tn = 2048, 256
    nt = s // tm
    nj = D_FF // tn
    hb = tm // FFN_HALO
    return pl.pallas_call(
        functools.partial(_ffn_up_kernel, nt=nt),
        grid=(b, nt, nj),
        in_specs=[pl.BlockSpec((1, FFN_HALO, d), lambda bi, i, j: (bi, jnp.maximum(i * hb - 1, 0), 0)),
                  pl.BlockSpec((1, tm, d), lambda bi, i, j: (bi, i, 0)),
                  pl.BlockSpec((1, FFN_HALO, d),
                               lambda bi, i, j: (bi, jnp.minimum((i + 1) * hb, s // FFN_HALO - 1), 0)),
                  pl.BlockSpec((d, tn), lambda bi, i, j: (0, j)),
                  pl.BlockSpec((d, tn), lambda bi, i, j: (0, j + nj)),
                  pl.BlockSpec((3, tn), lambda bi, i, j: (0, j)),
                  pl.BlockSpec((3, tn), lambda bi, i, j: (0, j + nj)),
                  pl.BlockSpec((1, tn), lambda bi, i, j: (0, j)),
                  pl.BlockSpec((1, tn), lambda bi, i, j: (0, j + nj))],
        out_specs=pl.BlockSpec((1, tm, tn), lambda bi, i, j: (bi, i, j)),
        out_shape=jax.ShapeDtypeStruct((b, s, D_FF), BF16),
        compiler_params=_cparams(),
        name="ffn_up",
    )(h2, h2, h2, w_up, w_up, conv_w, conv_w, conv_b, conv_b)


def _ffn_down_kernel(a_ref, w_ref, x_ref, mod_ref, g_ref, o_ref):
    for s0 in range(0, x_ref.shape[1], FFN_SUB):
        rows = slice(s0, s0 + FFN_SUB)
        x2 = x_ref[0, rows, :] + mod_ref[0, 5:6, :] * jnp.dot(a_ref[0, rows, :], w_ref[...],
                                                             preferred_element_type=F32)
        o_ref[0, rows, :] = _rms(x2, g_ref[...])


def _ffn_down(act, w_down, x1, mod, g):
    b, s, d = x1.shape
    tm = 1024
    return pl.pallas_call(
        _ffn_down_kernel,
        grid=(b, s // tm),
        in_specs=[pl.BlockSpec((1, tm, D_FF), lambda bi, i: (bi, i, 0)),
                  pl.BlockSpec((D_FF, d), lambda bi, i: (0, 0)),
                  pl.BlockSpec((1, tm, d), lambda bi, i: (bi, i, 0)),
                  pl.BlockSpec((1, 6, d), lambda bi, i: (bi, 0, 0)),
                  pl.BlockSpec((1, d), lambda bi, i: (0, 0))],
        out_specs=pl.BlockSpec((1, tm, d), lambda bi, i: (bi, i, 0)),
        out_shape=jax.ShapeDtypeStruct((b, s, d), F32),
        compiler_params=_cparams(),
        name="ffn_down",
    )(act, w_down, x1, mod, g)


def _qk_column_order():
    half = HEAD_DIM // 2
    return [(pair * 2 + j) * HEAD_DIM + f * half + i
            for pair in range(GROUP_W // LANES) for f in range(2) for j in range(2)
            for i in range(half)]


_QK_COLS = np.asarray(_qk_column_order(), np.int32)


def _trunk(x, mod, p, rope):
    b, s, _ = x.shape
    h1 = _norm_mod(x, mod, p["norm1_g"])
    qkvd = _mm(h1, p["w_qkvd"], BF16)
    zd = _mm(h1, p["w_zd"], BF16)
    gates, gt = _gates(h1, p["w_ab"], p["a_row"], p["dt_row"])
    os_, ls_ = [], []
    for gi, (_, dil) in enumerate(ATTN_GROUPS):
        o, l = _attn(_proj_attn(h1, p["w_attn"][gi], rope, dil, gi), gi, dil)
        os_.append(o)
        ls_.append(l)
    yd = _deltanet(qkvd, zd, gates, gt, p["conv_qkv_w"], p["dn_norm_g"])
    x1, h2 = _merge_out(os_, ls_, yd, h1, x, mod, p["w_gate"], p["w_br_attn"], p["w_br_dn"],
                        p["w_out"], p["norm2_g"])
    act = _ffn_up(h2, p["w_up"], p["ffn_conv_w"], p["ffn_conv_b"])
    return _ffn_down(act, p["w_down"], x1, mod, p["norm_f_g"])


def kernel(x_prompt, x_sample, c_prompt, c_sample, w_ada, b_ada, norm1_g, w_in, conv_qkv_w, a_log, dt_bias, dn_norm_g, w_br_attn, w_br_dn, w_out, norm2_g, w_up, ffn_conv_w, ffn_conv_b, w_down, norm_f_g):
    d = D_MODEL
    assert w_ada.shape[0] == 1, "single layer"
    w = w_in[0]
    o_qd = 3 * ATTN_W
    o_zd = o_qd + 3 * DN_W
    o_ab = o_zd + DN_W
    o_gate = o_ab + 4 * DN_HEADS
    pad16 = lambda v: jnp.pad(v.reshape(1, 2 * DN_HEADS).astype(F32), ((0, 0), (0, LANES - 2 * DN_HEADS)))
    p = {
        "norm1_g": norm1_g[0].reshape(1, d),
        "w_attn": [jnp.concatenate(
            [w[:, kind * ATTN_W + gi * GROUP_W:kind * ATTN_W + (gi + 1) * GROUP_W][:, cols]
             for kind, cols in ((0, _QK_COLS), (1, _QK_COLS), (2, slice(None)))], axis=1).astype(BF16)
            for gi in range(N_GROUPS)],
        "w_qkvd": w[:, o_qd:o_zd].astype(BF16),
        "w_zd": w[:, o_zd:o_ab].astype(BF16),
        "w_ab": jnp.pad(w[:, o_ab:o_gate], ((0, 0), (0, LANES - 4 * DN_HEADS))).astype(BF16),
        "w_gate": w[:, o_gate:].astype(BF16),
        "a_row": pad16(a_log[0]),
        "dt_row": pad16(dt_bias[0]),
        "conv_qkv_w": conv_qkv_w[0],
        "dn_norm_g": dn_norm_g[0].reshape(1, DN_HEAD_DIM),
        "w_br_attn": w_br_attn[0].astype(BF16),
        "w_br_dn": w_br_dn[0].astype(BF16),
        "w_out": w_out[0].astype(BF16),
        "norm2_g": norm2_g[0].reshape(1, d),
        "w_up": w_up[0].astype(BF16),
        "ffn_conv_w": ffn_conv_w[0],
        "ffn_conv_b": ffn_conv_b[0].reshape(1, 2 * D_FF),
        "w_down": w_down[0].astype(BF16),
        "norm_f_g": norm_f_g.reshape(1, d),
    }
    nb = x_prompt.shape[0]
    mod = _ada(jnp.concatenate([c_prompt, c_sample], axis=0), w_ada[0], b_ada[0].reshape(1, 6 * d))
    mod = mod.reshape(-1, 6, d)
    rope = _rope_tables(max(x_prompt.shape[1], x_sample.shape[1]))
    y_prompt = _trunk(x_prompt, mod[:nb], p, rope)
    y_sample = _trunk(x_sample, mod[nb:], p, rope)
    return (y_prompt, y_sample)
```

```python
import functools

import jax
import jax.numpy as jnp
import numpy as np
from jax import lax
from jax.experimental import pallas as pl
from jax.experimental.pallas import tpu as pltpu

F32 = jnp.float32
BF16 = jnp.bfloat16
HIGHEST = lax.Precision.HIGHEST

D_MODEL = 1024
ATTN_GROUPS = ((128, 1), (512, 4), (2048, 16))
N_GROUPS = 3
HEAD_DIM = 64
GROUP_W = 512
ATTN_W = N_GROUPS * GROUP_W
ATTN_HALF = 64
ATTN_BQ = 128
ATTN_MAX_BLOCKS = 4
ATTN_MAX_ROWS = 2048
ROPE_THETA = 10000.0
NEG = -1e30
DN_HEADS = 8
DN_HEAD_DIM = 128
DN_W = DN_HEADS * DN_HEAD_DIM
DN_CHUNK = 256
DN_BASE = 16
DN_GROUP = 2
DN_HALO = 16
DN_B_UNROLL = 8
MM_SUB = 256
MERGE_SUB = 256
D_FF = 2816
EPS = 1e-6
LANES = 128
VMEM_LIMIT = 56 * 1024 * 1024


def _cparams(**kw):
    return pltpu.CompilerParams(vmem_limit_bytes=VMEM_LIMIT, **kw)


def _sigmoid(x):
    return 1.0 / (1.0 + jnp.exp(-x))


def _dot(a, b):
    return jnp.dot(a.astype(BF16), b.astype(BF16), preferred_element_type=F32)


def _dot_nt(a, b):
    return lax.dot_general(a.astype(BF16), b.astype(BF16), (((1,), (1,)), ((), ())),
                           preferred_element_type=F32)


def _dot_tn(a, b):
    return lax.dot_general(a.astype(BF16), b.astype(BF16), (((0,), (0,)), ((), ())),
                           preferred_element_type=F32)


def _rms(x, g):
    return x * lax.rsqrt(jnp.mean(x * x, axis=-1, keepdims=True) + EPS) * g


def _ada_kernel(c_ref, w_ref, b_ref, o_ref):
    c = c_ref[...]
    s = c * _sigmoid(c)
    o_ref[...] = jnp.dot(s, w_ref[...], preferred_element_type=F32, precision=HIGHEST) + b_ref[...]


def _ada(c, w, b):
    bt, d = c.shape
    n = w.shape[1]
    tn = 1024
    return pl.pallas_call(
        _ada_kernel,
        grid=(n // tn,),
        in_specs=[pl.BlockSpec((bt, d), lambda j: (0, 0)),
                  pl.BlockSpec((d, tn), lambda j: (0, j)),
                  pl.BlockSpec((1, tn), lambda j: (0, j))],
        out_specs=pl.BlockSpec((bt, tn), lambda j: (0, j)),
        out_shape=jax.ShapeDtypeStruct((bt, n), F32),
        name="ada",
    )(c, w, b)


def _rope_table_kernel(inv_ref, cos_ref, sin_ref):
    tm = cos_ref.shape[0]
    pos = (pl.program_id(0) * tm + lax.broadcasted_iota(jnp.int32, (tm, LANES), 0)).astype(F32)
    ang = pos * inv_ref[...]
    lane = lax.broadcasted_iota(jnp.int32, (tm, LANES), 1)
    cos_ref[...] = jnp.cos(ang)
    sin_ref[...] = jnp.sin(ang) * jnp.where(lane < LANES // 2, -1.0, 1.0)


def _rope_tables(s):
    half = HEAD_DIM // 2
    inv = ROPE_THETA ** (-(jnp.arange(LANES) % half).astype(F32) / half)
    tm = 512
    return pl.pallas_call(
        _rope_table_kernel,
        grid=(s // tm,),
        in_specs=[pl.BlockSpec((1, LANES), lambda i: (0, 0))],
        out_specs=[pl.BlockSpec((tm, LANES), lambda i: (i, 0))] * 2,
        out_shape=[jax.ShapeDtypeStruct((s, LANES), F32)] * 2,
        name="rope_tables",
    )(inv.reshape(1, LANES))


def _norm_mod_kernel(x_ref, mod_ref, g_ref, o_ref):
    h = _rms(x_ref[0], g_ref[...]) * (1.0 + mod_ref[0, 1:2, :]) + mod_ref[0, 0:1, :]
    o_ref[0] = h.astype(o_ref.dtype)


def _norm_mod(x, mod, g):
    b, s, d = x.shape
    tm = 1024
    return pl.pallas_call(
        _norm_mod_kernel,
        grid=(b, s // tm),
        in_specs=[pl.BlockSpec((1, tm, d), lambda bi, i: (bi, i, 0)),
                  pl.BlockSpec((1, 6, d), lambda bi, i: (bi, 0, 0)),
                  pl.BlockSpec((1, d), lambda bi, i: (0, 0))],
        out_specs=pl.BlockSpec((1, tm, d), lambda bi, i: (bi, i, 0)),
        out_shape=jax.ShapeDtypeStruct((b, s, d), BF16),
        name="norm_mod",
    )(x, mod, g)


def _mm_kernel(h_ref, w_ref, o_ref):
    o_ref[0] = jnp.dot(h_ref[0], w_ref[...], preferred_element_type=F32).astype(o_ref.dtype)


def _mm(h, w, out_dtype):
    b, s, k = h.shape
    n = w.shape[1]
    tm, tn = 1024, 1024
    return pl.pallas_call(
        _mm_kernel,
        grid=(b, s // tm, n // tn),
        in_specs=[pl.BlockSpec((1, tm, k), lambda bi, i, j: (bi, i, 0)),
                  pl.BlockSpec((k, tn), lambda bi, i, j: (0, j))],
        out_specs=pl.BlockSpec((1, tm, tn), lambda bi, i, j: (bi, i, j)),
        out_shape=jax.ShapeDtypeStruct((b, s, n), out_dtype),
        compiler_params=_cparams(),
        name="proj",
    )(h, w)


def _proj_attn_kernel(h_ref, w_ref, cos_ref, sin_ref, o_ref, *scratch, dil):
    is_qk = pl.program_id(2) < 2
    tm = h_ref.shape[1]
    n_slab = w_ref.shape[1] // LANES
    for s0 in range(0, tm, MM_SUB):
        rows = slice(s0, s0 + MM_SUB)
        acc = jnp.dot(h_ref[0, rows, :], w_ref[...], preferred_element_type=F32)
        cos = jnp.where(is_qk, cos_ref[rows, :], 1.0)
        sin = jnp.where(is_qk, sin_ref[rows, :], 0.0)
        for c in range(n_slab):
            t = acc[:, c * LANES:(c + 1) * LANES]
            val = t * cos + pltpu.roll(t, LANES // 2, 1) * sin
            if dil == 1:
                o_ref[0, 0, 0, rows, c * LANES:(c + 1) * LANES] = val.astype(o_ref.dtype)
            else:
                scratch[0][c, rows, :] = val
        if dil > 1:
            n = MM_SUB // dil
            m0 = s0 // dil
            for r in range(dil):
                for c in range(n_slab):
                    o_ref[0, 0, r, m0:m0 + n, c * LANES:(c + 1) * LANES] = (
                        scratch[0][c, pl.ds(s0 + r, n, stride=dil), :].astype(o_ref.dtype))


def _proj_attn(h, w, rope, dil, gi):
    b, s, k = h.shape
    tm, tn = 2048, GROUP_W
    scratch = [pltpu.VMEM((tn // LANES, tm, LANES), F32)] if dil > 1 else []
    return pl.pallas_call(
        functools.partial(_proj_attn_kernel, dil=dil),
        grid=(b, s // tm, 3),
        in_specs=[pl.BlockSpec((1, tm, k), lambda bi, i, j: (bi, i, 0)),
                  pl.BlockSpec((k, tn), lambda bi, i, j: (0, j)),
                  pl.BlockSpec((tm, LANES), lambda bi, i, j: (i, 0)),
                  pl.BlockSpec((tm, LANES), lambda bi, i, j: (i, 0))],
        out_specs=pl.BlockSpec((1, 1, dil, tm // dil, tn), lambda bi, i, j: (bi, j, 0, i, 0)),
        out_shape=jax.ShapeDtypeStruct((b, 3, dil, s // dil, tn), BF16),
        scratch_shapes=scratch,
        compiler_params=_cparams(),
        name=f"proj_attn_g{gi}",
    )(h, w, *rope)


def _gates_kernel(h_ref, w_ref, a_ref, dt_ref, o_ref, gt_ref):
    c = DN_CHUNK
    r = lax.broadcasted_iota(jnp.int32, (c, c), 0)
    cc = lax.broadcasted_iota(jnp.int32, (c, c), 1)
    lower = jnp.where(cc <= r, 1.0, 0.0).astype(F32)
    upper = jnp.where(cc >= r, 1.0, 0.0).astype(F32)
    lane = lax.broadcasted_iota(jnp.int32, (c, LANES), 1)
    for j in range(h_ref.shape[1] // c):
        acc = jnp.dot(h_ref[0, j * c:(j + 1) * c, :], w_ref[...], preferred_element_type=F32)
        x = acc + dt_ref[...]
        softplus = jnp.maximum(x, 0.0) + jnp.log1p(jnp.exp(-jnp.abs(x)))
        g = -jnp.exp(a_ref[...]) * softplus
        beta = _sigmoid(acc)
        pre = jnp.dot(lower, g, preferred_element_type=F32, precision=HIGHEST)
        suf = jnp.dot(upper, g, preferred_element_type=F32, precision=HIGHEST)
        out = jnp.where(lane < DN_HEADS, pre,
                        jnp.where(lane < 2 * DN_HEADS, suf,
                                  jnp.where(lane < 4 * DN_HEADS, beta, 0.0)))
        o_ref[0, j * c:(j + 1) * c, :] = out
        gt_ref[0, j] = out.T[0:2 * DN_HEADS, :]


def _gates(h, w, a_row, dt_row):
    b, s, k = h.shape
    tm = 4 * DN_CHUNK
    return pl.pallas_call(
        _gates_kernel,
        grid=(b, s // tm),
        in_specs=[pl.BlockSpec((1, tm, k), lambda bi, i: (bi, i, 0)),
                  pl.BlockSpec((k, LANES), lambda bi, i: (0, 0)),
                  pl.BlockSpec((1, LANES), lambda bi, i: (0, 0)),
                  pl.BlockSpec((1, LANES), lambda bi, i: (0, 0))],
        out_specs=[pl.BlockSpec((1, tm, LANES), lambda bi, i: (bi, i, 0)),
                   pl.BlockSpec((1, tm // DN_CHUNK, 2 * DN_HEADS, DN_CHUNK),
                                lambda bi, i: (bi, i, 0, 0))],
        out_shape=[jax.ShapeDtypeStruct((b, s, LANES), F32),
                   jax.ShapeDtypeStruct((b, s // DN_CHUNK, 2 * DN_HEADS, DN_CHUNK), F32)],
        name="dn_gates",
    )(h, w, a_row, dt_row)


def _attn_kernel(q_ref, k_ref, v_ref, o_ref, l_ref, *scratch, seq, bq, nq, kb, dil):
    n_slab = GROUP_W // LANES
    lane = lax.broadcasted_iota(jnp.int32, (bq, LANES), 1)
    blocks = []
    for qb in range(nq):
        row0 = qb * bq
        q0 = (pl.program_id(1) * nq + qb) * bq
        ks = pl.multiple_of(jnp.clip(q0 - ATTN_HALF, 0, seq - kb), ATTN_HALF)
        qpos = q0 + lax.broadcasted_iota(jnp.int32, (bq, kb), 0)
        kpos = ks + lax.broadcasted_iota(jnp.int32, (bq, kb), 1)
        valid = jnp.abs(qpos - kpos) <= ATTN_HALF

        def residue(r, row0=row0, ks=ks, valid=valid):
            q = q_ref[0, 0, r, row0:row0 + bq, :].astype(F32) * (HEAD_DIM ** -0.5)
            k = k_ref[0, 0, r, pl.ds(ks, kb), :]
            v = v_ref[0, 0, r, pl.ds(ks, kb), :]
            heads = [(hp, sub) for hp in range(n_slab) for sub in range(2)]
            in_head = [lane < HEAD_DIM, lane >= HEAD_DIM]
            qk_lanes = [(lane & (HEAD_DIM // 2)) == 0, (lane & (HEAD_DIM // 2)) != 0]
            slab = lambda x, hp: x[:, hp * LANES:(hp + 1) * LANES]
            s_ = [_dot_nt(jnp.where(qk_lanes[sub], slab(q, hp), 0.0), slab(k, hp)) for hp, sub in heads]
            s_ = [jnp.where(valid, s, NEG) for s in s_]
            mx_ = [jnp.max(s, axis=-1, keepdims=True) for s in s_]
            p_ = [jnp.exp(s - mx) for s, mx in zip(s_, mx_)]
            den_ = [jnp.sum(p, axis=-1, keepdims=True) for p in p_]
            o_ = [_dot(p, slab(v, hp)) / den for p, den, (hp, _) in zip(p_, den_, heads)]
            lse_ = [mx + jnp.log(den) for mx, den in zip(mx_, den_)]
            for hp in range(n_slab):
                sl = slice(hp * LANES, (hp + 1) * LANES)
                o_pair = jnp.where(in_head[1], o_[2 * hp + 1], o_[2 * hp])
                l_pair = jnp.where(in_head[1], lse_[2 * hp + 1], lse_[2 * hp])
                if dil == 1:
                    o_ref[0, row0:row0 + bq, sl] = o_pair.astype(o_ref.dtype)
                    l_ref[0, row0:row0 + bq, sl] = l_pair
                else:
                    scratch[0][hp, pl.ds(row0 * dil + r, bq, stride=dil), :] = o_pair
                    scratch[1][hp, pl.ds(row0 * dil + r, bq, stride=dil), :] = l_pair

        blocks.append(residue)

    if dil == 1:
        for block in blocks:
            block(0)
    else:
        def body(r, carry):
            for block in blocks:
                block(r)
            return carry
        lax.fori_loop(0, dil, body, 0, unroll=max(1, ATTN_MAX_BLOCKS // nq))
    if dil > 1:
        for hp in range(n_slab):
            sl = slice(hp * LANES, (hp + 1) * LANES)
            o_ref[0, :, sl] = scratch[0][hp].astype(o_ref.dtype)
            l_ref[0, :, sl] = scratch[1][hp]


def _attn(qkv, gi, dil):
    b, _, _, seq, _ = qkv.shape
    s = seq * dil
    bq = min(ATTN_BQ, seq)
    kb = min(bq + 2 * ATTN_HALF, seq)
    nq = 1
    while 2 * nq <= ATTN_MAX_BLOCKS and 2 * nq * bq * dil <= ATTN_MAX_ROWS and seq % (2 * nq * bq) == 0:
        nq *= 2
    rows = nq * bq * dil
    scratch = [pltpu.VMEM((GROUP_W // LANES, rows, LANES), F32)] * 2 if dil > 1 else []
    kv = lambda kind: pl.BlockSpec((1, 1, dil, seq, GROUP_W), lambda bi, m: (bi, kind, 0, 0, 0))
    return pl.pallas_call(
        functools.partial(_attn_kernel, seq=seq, bq=bq, nq=nq, kb=kb, dil=dil),
        grid=(b, seq // (nq * bq)),
        in_specs=[pl.BlockSpec((1, 1, dil, nq * bq, GROUP_W), lambda bi, m: (bi, 0, 0, m, 0)),
                  kv(1), kv(2)],
        out_specs=[pl.BlockSpec((1, rows, GROUP_W), lambda bi, m: (bi, m, 0))] * 2,
        out_shape=[jax.ShapeDtypeStruct((b, s, GROUP_W), BF16),
                   jax.ShapeDtypeStruct((b, s, GROUP_W), F32)],
        scratch_shapes=scratch,
        compiler_params=_cparams(),
        name=f"attn_g{gi}",
    )(qkv, qkv, qkv)


def _tri_inverse(a_list, masks_ref, lmask_ref, bases, uppers, eye):
    ts = [eye - a * masks_ref[b] for a, b in zip(a_list, bases)]
    a_list = [a.astype(BF16) for a in a_list]
    ps = [-(a * lmask_ref[b]) for a, b in zip(a_list, bases)]
    n = 2
    while n < DN_BASE:
        ps = [_dot(p, p).astype(BF16) for p in ps]
        ts = [t + _dot(t, p) for t, p in zip(ts, ps)]
        n *= 2
    n = DN_BASE
    lvl = 1
    while 2 * n < DN_CHUNK:
        left = [_dot(t, a * lmask_ref[b + lvl]) for t, a, b in zip(ts, a_list, bases)]
        ts = [t - _dot(l, t) for t, l in zip(ts, left)]
        n *= 2
        lvl += 1
    t11 = [t[0:n, 0:n] for t in ts]
    t22 = [t[n:, n:] for t in ts]
    zero = jnp.zeros((n, n), F32)
    left = [_dot(x11, a[0:n, n:]) if up else _dot(x22, a[n:, 0:n])
            for x11, x22, a, up in zip(t11, t22, a_list, uppers)]
    off = [-_dot(l, x22) if up else -_dot(l, x11) for l, x11, x22, up in zip(left, t11, t22, uppers)]
    return [jnp.concatenate([jnp.concatenate([x11, o if up else zero], axis=1),
                             jnp.concatenate([zero if up else o, x22], axis=1)], axis=0)
            for x11, x22, o, up in zip(t11, t22, off, uppers)]


def _dn_kernel(q_ref, k_ref, v_ref, z_ref, cwq_ref, cwk_ref, cwv_ref, gates_ref, gt_ref, ng_ref,
               o_ref, acc_ref, qkv_a_ref, qkv_b_ref, masks_ref, lmask_ref, u_ref, wq_ref, cs_ref, ns_ref, p_ref,
               gl_ref, *, seq):
    h = pl.program_id(1)
    c = DN_CHUNK
    nc = seq // c
    n_lvl = 1
    while DN_BASE << n_lvl < c:
        n_lvl += 1

    r = lax.broadcasted_iota(jnp.int32, (c, c), 0)
    cc = lax.broadcasted_iota(jnp.int32, (c, c), 1)
    as_f32 = lambda m: jnp.where(m, 1.0, 0.0).astype(F32)
    eye = as_f32(r == cc)
    per_dir = 2 + n_lvl
    base_shift = DN_BASE.bit_length() - 1

    @pl.when((pl.program_id(0) == 0) & (h == 0))
    def _():
        for d, (lo, hi) in enumerate(((cc, r), (r, cc))):
            masks_ref[d * per_dir + 0] = as_f32(lo <= hi)
            masks_ref[d * per_dir + 1] = as_f32(lo < hi)
            base_mask = as_f32((lo < hi) & ((lo >> base_shift) == (hi >> base_shift)))
            masks_ref[d * per_dir + 2] = base_mask
            lmask_ref[d * per_dir + 2] = base_mask.astype(BF16)
            for lvl in range(1, n_lvl):
                sh = base_shift + lvl - 1
                lo_blk, hi_blk = lo >> sh, hi >> sh
                lmask_ref[d * per_dir + 2 + lvl] = as_f32(
                    (hi_blk == lo_blk + 1) & ((lo_blk & 1) == 0)).astype(BF16)

    n_groups = nc // DN_GROUP

    def prep_group(g, dst_ref):
        halo = DN_HALO
        n = c + 2 * halo
        for j in range(DN_GROUP):
            ci = DN_GROUP * g + j
            r0 = pl.multiple_of(ci * c, c)
            lo = pl.multiple_of(jnp.maximum(r0 - halo, 0), halo)
            hi = pl.multiple_of(jnp.minimum(r0 + c, seq - halo), halo)
            keep_lo = jnp.where(ci > 0, 1.0, 0.0)
            keep_hi = jnp.where(ci < nc - 1, 1.0, 0.0)
            for idx, (x_ref, w_ref, kind) in enumerate(((q_ref, cwq_ref, "q"), (k_ref, cwk_ref, "k"),
                                                        (v_ref, cwv_ref, "v"))):
                x = jnp.concatenate([x_ref[0, pl.ds(lo, halo), :].astype(F32) * keep_lo,
                                     x_ref[0, pl.ds(r0, c), :].astype(F32),
                                     x_ref[0, pl.ds(hi, halo), :].astype(F32) * keep_hi], axis=0)
                w = w_ref[...]
                y = (pltpu.roll(x, 1, 0) * w[0:1, :] + x * w[1:2, :]
                     + pltpu.roll(x, n - 1, 0) * w[2:3, :])[halo:halo + c]
                y = y * _sigmoid(y)
                if kind != "v":
                    y = y * lax.rsqrt(jnp.sum(y * y, axis=-1, keepdims=True) + EPS)
                if kind == "q":
                    y = y * (DN_HEAD_DIM ** -0.5)
                dst_ref[idx, j * c:(j + 1) * c, :] = y

    prep_group(jnp.int32(0), qkv_a_ref)

    lane = lax.broadcasted_iota(jnp.int32, (c, LANES), 1)

    def solve_group(g, src_ref):
        sys_, a_list, bases = [], [], []
        for j in range(DN_GROUP):
            ci = DN_GROUP * g + j
            r0 = pl.multiple_of(ci * c, c)
            q = src_ref[0, j * c:(j + 1) * c, :]
            k = src_ref[1, j * c:(j + 1) * c, :]
            v = src_ref[2, j * c:(j + 1) * c, :]
            gch = gates_ref[0, pl.ds(r0, c), :]
            kk = _dot_nt(k, k)
            qk = _dot_nt(q, k)
            for d in range(2):
                col = lambda j: jnp.sum(jnp.where(lane == j, gch, 0.0), axis=1, keepdims=True)
                gc = col(d * DN_HEADS + h)
                beta = col((2 + d) * DN_HEADS + h)
                grow = gt_ref[0, ci, pl.ds(d * DN_HEADS + h, 1), :]
                g_end = gc[0:1, :] if d else gc[c - 1:c, :]
                dec = jnp.exp(jnp.minimum(gc - grow, 0.0))
                e_g = jnp.exp(gc)
                a_list.append(kk * (beta * dec) * masks_ref[d * per_dir + 1])
                bases.append(d * per_dir + 2)
                sys_.append(dict(
                    ci=ci, r0=r0, d=d,
                    rhs=jnp.concatenate([v * beta, k * (beta * e_g)], axis=1),
                    qg=(q * e_g).astype(BF16),
                    kd_t=(k * jnp.exp(g_end - gc)).T.astype(BF16),
                    p=(qk * dec * masks_ref[d * per_dir]).astype(BF16),
                    gl=jnp.broadcast_to(jnp.exp(g_end), (1, LANES))))
        t_list = _tri_inverse(a_list, masks_ref, lmask_ref, bases, [s["d"] == 1 for s in sys_], eye)
        uw_list = [_dot(t, s["rhs"]) for t, s in zip(t_list, sys_)]
        nc_list = [_dot(s["kd_t"], uw) for s, uw in zip(sys_, uw_list)]
        for s, uw, n_c in zip(sys_, uw_list, nc_list):
            d, ci, r0 = s["d"], s["ci"], s["r0"]
            u_ref[d, pl.ds(r0, c), :] = uw[:, 0:LANES]
            wq_ref[d, ci, 0:c, :] = uw[:, LANES:2 * LANES].astype(BF16)
            wq_ref[d, ci, c:2 * c, :] = s["qg"]
            ns_ref[d, ci] = n_c[:, 0:LANES]
            cs_ref[d, ci] = n_c[:, LANES:2 * LANES].astype(BF16)
            p_ref[d, ci] = s["p"]
            gl_ref[d, pl.ds(ci, 1), :] = s["gl"]

    def phase_a(m, carry):
        prep_group(2 * m + 1, qkv_b_ref)
        solve_group(2 * m, qkv_a_ref)
        prep_group(jnp.minimum(2 * m + 2, n_groups - 1), qkv_a_ref)
        solve_group(2 * m + 1, qkv_b_ref)
        return carry

    lax.fori_loop(0, n_groups // 2, phase_a, 0)

    def phase_b(i, carry, second_half):
        cis = (i, nc - 1 - i)
        r0s = [pl.multiple_of(ci * c, c) for ci in cis]
        s16 = [carry[d].astype(BF16) for d in range(2)]
        new = [carry[d] * gl_ref[d, pl.ds(cis[d], 1), :] + ns_ref[d, cis[d]]
               - jnp.dot(cs_ref[d, cis[d]], s16[d], preferred_element_type=F32) for d in range(2)]
        ws_qs = [jnp.dot(wq_ref[d, cis[d]], s16[d], preferred_element_type=F32) for d in range(2)]
        v_new = [(u_ref[d, pl.ds(r0s[d], c), :] - ws_qs[d][0:c]).astype(BF16) for d in range(2)]
        outs = [ws_qs[d][c:2 * c] + jnp.dot(p_ref[d, cis[d]], v_new[d], preferred_element_type=F32)
                for d in range(2)]
        for d in range(2):
            rows = pl.ds(r0s[d], c)
            if not second_half:
                acc_ref[rows, :] = outs[d]
            else:
                z = z_ref[0, rows, :].astype(F32)
                y = _rms(acc_ref[rows, :] + outs[d], ng_ref[...]) * (z * _sigmoid(z))
                o_ref[0, rows, :] = y.astype(o_ref.dtype)
        return tuple(new)

    zero = jnp.zeros((DN_HEAD_DIM, DN_HEAD_DIM), F32)
    mid = lax.fori_loop(0, nc // 2, functools.partial(phase_b, second_half=False), (zero, zero),
                        unroll=DN_B_UNROLL)
    lax.fori_loop(nc // 2, nc, functools.partial(phase_b, second_half=True), mid, unroll=DN_B_UNROLL)


def _deltanet(qkvd, zd, gates, gt, conv_w, norm_g):
    b, s, _ = qkvd.shape
    hd = DN_HEAD_DIM
    nc = s // DN_CHUNK
    n_lvl = 1
    while DN_BASE << n_lvl < DN_CHUNK:
        n_lvl += 1
    n_masks = 2 * (2 + n_lvl)
    col = lambda off: pl.BlockSpec((1, s, hd), lambda bi, h: (bi, 0, off + h))
    cw = lambda off: pl.BlockSpec((3, hd), lambda bi, h: (0, off + h))
    return pl.pallas_call(
        functools.partial(_dn_kernel, seq=s),
        grid=(b, DN_HEADS),
        in_specs=[col(0), col(DN_HEADS), col(2 * DN_HEADS), col(0),
                  cw(0), cw(DN_HEADS), cw(2 * DN_HEADS),
                  pl.BlockSpec((1, s, LANES), lambda bi, h: (bi, 0, 0)),
                  pl.BlockSpec((1, nc, 2 * DN_HEADS, DN_CHUNK), lambda bi, h: (bi, 0, 0, 0)),
                  pl.BlockSpec((1, hd), lambda bi, h: (0, 0))],
        out_specs=pl.BlockSpec((1, s, hd), lambda bi, h: (bi, 0, h)),
        out_shape=jax.ShapeDtypeStruct((b, s, DN_W), BF16),
        scratch_shapes=[pltpu.VMEM((s, hd), F32),
                        pltpu.VMEM((3, DN_GROUP * DN_CHUNK, hd), F32),
                        pltpu.VMEM((3, DN_GROUP * DN_CHUNK, hd), F32),
                        pltpu.VMEM((n_masks, DN_CHUNK, DN_CHUNK), F32),
                        pltpu.VMEM((n_masks, DN_CHUNK, DN_CHUNK), BF16),
                        pltpu.VMEM((2, s, hd), F32),
                        pltpu.VMEM((2, nc, 2 * DN_CHUNK, hd), BF16),
                        pltpu.VMEM((2, nc, hd, hd), BF16),
                        pltpu.VMEM((2, nc, hd, hd), F32),
                        pltpu.VMEM((2, nc, DN_CHUNK, DN_CHUNK), BF16),
                        pltpu.VMEM((2, max(nc, 8), hd), F32)],
        compiler_params=_cparams(dimension_semantics=("arbitrary", "arbitrary")),
        name="deltanet",
    )(qkvd, qkvd, qkvd, zd, conv_w, conv_w, conv_w, gates, gt, norm_g)


def _merge_kernel(o1_ref, o2_ref, o3_ref, l1_ref, l2_ref, l3_ref, yd_ref, h1_ref, x_ref, mod_ref,
                  wg_ref, wba_ref, wbd_ref, wo_ref, g2_ref, x1_ref, h2_ref):
    d = D_MODEL
    for s0 in range(0, x_ref.shape[1], MERGE_SUB):
        rows = slice(s0, s0 + MERGE_SUB)
        l1, l2, l3 = l1_ref[0, rows, :], l2_ref[0, rows, :], l3_ref[0, rows, :]
        mx = jnp.maximum(l1, jnp.maximum(l2, l3))
        e1, e2, e3 = jnp.exp(l1 - mx), jnp.exp(l2 - mx), jnp.exp(l3 - mx)
        ya = (e1 * o1_ref[0, rows, :] + e2 * o2_ref[0, rows, :] + e3 * o3_ref[0, rows, :]) / (e1 + e2 + e3)
        h1 = h1_ref[0, rows, :]
        gate_a = _sigmoid(jnp.dot(h1, wg_ref[:, 0:d], preferred_element_type=F32))
        gate_d = _sigmoid(jnp.dot(h1, wg_ref[:, d:2 * d], preferred_element_type=F32))
        merged = gate_a * _dot(ya, wba_ref[...]) + gate_d * _dot(yd_ref[0, rows, :], wbd_ref[...])
        x1 = x_ref[0, rows, :] + mod_ref[0, 2:3, :] * _dot(merged, wo_ref[...])
        x1_ref[0, rows, :] = x1
        h2 = _rms(x1, g2_ref[...]) * (1.0 + mod_ref[0, 4:5, :]) + mod_ref[0, 3:4, :]
        h2_ref[0, rows, :] = h2.astype(h2_ref.dtype)


def _merge_out(os_, ls_, yd, h1, x, mod, wg, wba, wbd, wo, g2):
    b, s, d = x.shape
    tm = 2 * MERGE_SUB
    row = lambda w: pl.BlockSpec((1, tm, w), lambda bi, i: (bi, i, 0))
    full = lambda a: pl.BlockSpec(a.shape, lambda bi, i: (0,) * a.ndim)
    return pl.pallas_call(
        _merge_kernel,
        grid=(b, s // tm),
        in_specs=[row(GROUP_W)] * 6 + [row(DN_W), row(d), row(d),
                                       pl.BlockSpec((1, 6, d), lambda bi, i: (bi, 0, 0)),
                                       full(wg), full(wba), full(wbd), full(wo), full(g2)],
        out_specs=[row(d), row(d)],
        out_shape=[jax.ShapeDtypeStruct((b, s, d), F32), jax.ShapeDtypeStruct((b, s, d), BF16)],
        compiler_params=_cparams(),
        name="merge_out",
    )(*os_, *ls_, yd, h1, x, mod, wg, wba, wbd, wo, g2)


FFN_HALO = 16
FFN_SUB = 512


def _ffn_up_kernel(hp_ref, hm_ref, hn_ref, wv_ref, wg_ref, cwv_ref, cwg_ref, bv_ref, bg_ref,
                   o_ref, *, nt):
    i = pl.program_id(1)
    tm = hm_ref.shape[1]
    lhs_all = jnp.concatenate([hp_ref[0], hm_ref[0], hn_ref[0]], axis=0)
    sub = FFN_SUB
    n = sub + 2 * FFN_HALO
    for s0 in range(0, tm, sub):
        lhs = lhs_all[s0:s0 + n]
        keep_prev = jnp.where(i > 0, 1.0, 0.0) if s0 == 0 else None
        keep_next = jnp.where(i < nt - 1, 1.0, 0.0) if s0 + sub == tm else None
        ups = [jnp.dot(lhs, w_ref[...], preferred_element_type=F32) for w_ref in (wv_ref, wg_ref)]

        def conv(up, cw_ref, b_ref):
            head, tail = up[0:FFN_HALO], up[FFN_HALO + sub:]
            if keep_prev is not None:
                head = head * keep_prev
            if keep_next is not None:
                tail = tail * keep_next
            up = jnp.concatenate([head, up[FFN_HALO:FFN_HALO + sub], tail], axis=0)
            cw = cw_ref[...]
            y = (pltpu.roll(up, 1, 0) * cw[0:1, :] + up * cw[1:2, :]
                 + pltpu.roll(up, n - 1, 0) * cw[2:3, :])
            return y[FFN_HALO:FFN_HALO + sub] + b_ref[...]

        val = conv(ups[0], cwv_ref, bv_ref)
        gate = conv(ups[1], cwg_ref, bg_ref)
        o_ref[0, s0:s0 + sub, :] = (gate * _sigmoid(gate) * val).astype(o_ref.dtype)


def _ffn_up(h2, w_up, conv_w, conv_b):
    b, s, d = h2.shape
    tm, tn = 2048, 256
    nt = s // tm
    nj = D_FF // tn
    hb = tm // FFN_HALO
    return pl.pallas_call(
        functools.partial(_ffn_up_kernel, nt=nt),
        grid=(b, nt, nj),
        in_specs=[pl.BlockSpec((1, FFN_HALO, d), lambda bi, i, j: (bi, jnp.maximum(i * hb - 1, 0), 0)),
                  pl.BlockSpec((1, tm, d), lambda bi, i, j: (bi, i, 0)),
                  pl.BlockSpec((1, FFN_HALO, d),
                               lambda bi, i, j: (bi, jnp.minimum((i + 1) * hb, s // FFN_HALO - 1), 0)),
                  pl.BlockSpec((d, tn), lambda bi, i, j: (0, j)),
                  pl.BlockSpec((d, tn), lambda bi, i, j: (0, j + nj)),
                  pl.BlockSpec((3, tn), lambda bi, i, j: (0, j)),
                  pl.BlockSpec((3, tn), lambda bi, i, j: (0, j + nj)),
                  pl.BlockSpec((1, tn), lambda bi, i, j: (0, j)),
                  pl.BlockSpec((1, tn), lambda bi, i, j: (0, j + nj))],
        out_specs=pl.BlockSpec((1, tm, tn), lambda bi, i, j: (bi, i, j)),
        out_shape=jax.ShapeDtypeStruct((b, s, D_FF), BF16),
        compiler_params=_cparams(),
        name="ffn_up",
    )(h2, h2, h2, w_up, w_up, conv_w, conv_w, conv_b, conv_b)


def _ffn_down_kernel(a_ref, w_ref, x_ref, mod_ref, g_ref, o_ref):
    for s0 in range(0, x_ref.shape[1], FFN_SUB):
        rows = slice(s0, s0 + FFN_SUB)
        x2 = x_ref[0, rows, :] + mod_ref[0, 5:6, :] * jnp.dot(a_ref[0, rows, :], w_ref[...],
                                                             preferred_element_type=F32)
        o_ref[0, rows, :] = _rms(x2, g_ref[...])


def _ffn_down(act, w_down, x1, mod, g):
    b, s, d = x1.shape
    tm = 1024
    return pl.pallas_call(
        _ffn_down_kernel,
        grid=(b, s // tm),
        in_specs=[pl.BlockSpec((1, tm, D_FF), lambda bi, i: (bi, i, 0)),
                  pl.BlockSpec((D_FF, d), lambda bi, i: (0, 0)),
                  pl.BlockSpec((1, tm, d), lambda bi, i: (bi, i, 0)),
                  pl.BlockSpec((1, 6, d), lambda bi, i: (bi, 0, 0)),
                  pl.BlockSpec((1, d), lambda bi, i: (0, 0))],
        out_specs=pl.BlockSpec((1, tm, d), lambda bi, i: (bi, i, 0)),
        out_shape=jax.ShapeDtypeStruct((b, s, d), F32),
        compiler_params=_cparams(),
        name="ffn_down",
    )(act, w_down, x1, mod, g)


def _qk_column_order():
    half = HEAD_DIM // 2
    return [(pair * 2 + j) * HEAD_DIM + f * half + i
            for pair in range(GROUP_W // LANES) for f in range(2) for j in range(2)
            for i in range(half)]


_QK_COLS = np.asarray(_qk_column_order(), np.int32)


def _trunk(x, mod, p, rope):
    b, s, _ = x.shape
    h1 = _norm_mod(x, mod, p["norm1_g"])
    qkvd = _mm(h1, p["w_qkvd"], BF16)
    zd = _mm(h1, p["w_zd"], BF16)
    gates, gt = _gates(h1, p["w_ab"], p["a_row"], p["dt_row"])
    os_, ls_ = [], []
    for gi, (_, dil) in enumerate(ATTN_GROUPS):
        o, l = _attn(_proj_attn(h1, p["w_attn"][gi], rope, dil, gi), gi, dil)
        os_.append(o)
        ls_.append(l)
    yd = _deltanet(qkvd, zd, gates, gt, p["conv_qkv_w"], p["dn_norm_g"])
    x1, h2 = _merge_out(os_, ls_, yd, h1, x, mod, p["w_gate"], p["w_br_attn"], p["w_br_dn"],
                        p["w_out"], p["norm2_g"])
    act = _ffn_up(h2, p["w_up"], p["ffn_conv_w"], p["ffn_conv_b"])
    return _ffn_down(act, p["w_down"], x1, mod, p["norm_f_g"])


def kernel(x_prompt, x_sample, c_prompt, c_sample, w_ada, b_ada, norm1_g, w_in, conv_qkv_w, a_log, dt_bias, dn_norm_g, w_br_attn, w_br_dn, w_out, norm2_g, w_up, ffn_conv_w, ffn_conv_b, w_down, norm_f_g):
    d = D_MODEL
    assert w_ada.shape[0] == 1, "single layer"
    w = w_in[0]
    o_qd = 3 * ATTN_W
    o_zd = o_qd + 3 * DN_W
    o_ab = o_zd + DN_W
    o_gate = o_ab + 4 * DN_HEADS
    pad16 = lambda v: jnp.pad(v.reshape(1, 2 * DN_HEADS).astype(F32), ((0, 0), (0, LANES - 2 * DN_HEADS)))
    p = {
        "norm1_g": norm1_g[0].reshape(1, d),
        "w_attn": [jnp.concatenate(
            [w[:, kind * ATTN_W + gi * GROUP_W:kind * ATTN_W + (gi + 1) * GROUP_W][:, cols]
             for kind, cols in ((0, _QK_COLS), (1, _QK_COLS), (2, slice(None)))], axis=1).astype(BF16)
            for gi in range(N_GROUPS)],
        "w_qkvd": w[:, o_qd:o_zd].astype(BF16),
        "w_zd": w[:, o_zd:o_ab].astype(BF16),
        "w_ab": jnp.pad(w[:, o_ab:o_gate], ((0, 0), (0, LANES - 4 * DN_HEADS))).astype(BF16),
        "w_gate": w[:, o_gate:].astype(BF16),
        "a_row": pad16(a_log[0]),
        "dt_row": pad16(dt_bias[0]),
        "conv_qkv_w": conv_qkv_w[0],
        "dn_norm_g": dn_norm_g[0].reshape(1, DN_HEAD_DIM),
        "w_br_attn": w_br_attn[0].astype(BF16),
        "w_br_dn": w_br_dn[0].astype(BF16),
        "w_out": w_out[0].astype(BF16),
        "norm2_g": norm2_g[0].reshape(1, d),
        "w_up": w_up[0].astype(BF16),
        "ffn_conv_w": ffn_conv_w[0],
        "ffn_conv_b": ffn_conv_b[0].reshape(1, 2 * D_FF),
        "w_down": w_down[0].astype(BF16),
        "norm_f_g": norm_f_g.reshape(1, d),
    }
    nb = x_prompt.shape[0]
    mod = _ada(jnp.concatenate([c_prompt, c_sample], axis=0), w_ada[0], b_ada[0].reshape(1, 6 * d))
    mod = mod.reshape(-1, 6, d)
    rope = _rope_tables(max(x_prompt.shape[1], x_sample.shape[1]))
    y_prompt = _trunk(x_prompt, mod[:nb], p, rope)
    y_sample = _trunk(x_sample, mod[nb:], p, rope)
    return (y_prompt, y_sample)
```

```python
import functools

import jax
import jax.numpy as jnp
import numpy as np
from jax import lax
from jax.experimental import pallas as pl
from jax.experimental.pallas import tpu as pltpu

F32 = jnp.float32
BF16 = jnp.bfloat16
HIGHEST = lax.Precision.HIGHEST

D_MODEL = 1024
ATTN_GROUPS = ((128, 1), (512, 4), (2048, 16))
N_GROUPS = 3
HEAD_DIM = 64
GROUP_W = 512
ATTN_W = N_GROUPS * GROUP_W
ATTN_HALF = 64
ATTN_BQ = 128
ATTN_MAX_BLOCKS = 4
ATTN_MAX_ROWS = 2048
ROPE_THETA = 10000.0
NEG = -1e30
DN_HEADS = 8
DN_HEAD_DIM = 128
DN_W = DN_HEADS * DN_HEAD_DIM
DN_CHUNK = 256
DN_BASE = 16
DN_GROUP = 2
DN_HALO = 16
DN_B_UNROLL = 8
MM_SUB = 256
MERGE_SUB = 256
D_FF = 2816
EPS = 1e-6
LANES = 128
VMEM_LIMIT = 56 * 1024 * 1024


def _cparams(**kw):
    return pltpu.CompilerParams(vmem_limit_bytes=VMEM_LIMIT, **kw)


def _sigmoid(x):
    return 1.0 / (1.0 + jnp.exp(-x))


def _dot(a, b):
    return jnp.dot(a.astype(BF16), b.astype(BF16), preferred_element_type=F32)


def _dot_nt(a, b):
    return lax.dot_general(a.astype(BF16), b.astype(BF16), (((1,), (1,)), ((), ())),
                           preferred_element_type=F32)


def _dot_tn(a, b):
    return lax.dot_general(a.astype(BF16), b.astype(BF16), (((0,), (0,)), ((), ())),
                           preferred_element_type=F32)


def _rms(x, g):
    return x * lax.rsqrt(jnp.mean(x * x, axis=-1, keepdims=True) + EPS) * g


def _ada_kernel(c_ref, w_ref, b_ref, o_ref):
    c = c_ref[...]
    s = c * _sigmoid(c)
    o_ref[...] = jnp.dot(s, w_ref[...], preferred_element_type=F32, precision=HIGHEST) + b_ref[...]


def _ada(c, w, b):
    bt, d = c.shape
    n = w.shape[1]
    tn = 1024
    return pl.pallas_call(
        _ada_kernel,
        grid=(n // tn,),
        in_specs=[pl.BlockSpec((bt, d), lambda j: (0, 0)),
                  pl.BlockSpec((d, tn), lambda j: (0, j)),
                  pl.BlockSpec((1, tn), lambda j: (0, j))],
        out_specs=pl.BlockSpec((bt, tn), lambda j: (0, j)),
        out_shape=jax.ShapeDtypeStruct((bt, n), F32),
        name="ada",
    )(c, w, b)


def _rope_table_kernel(inv_ref, cos_ref, sin_ref):
    tm = cos_ref.shape[0]
    pos = (pl.program_id(0) * tm + lax.broadcasted_iota(jnp.int32, (tm, LANES), 0)).astype(F32)
    ang = pos * inv_ref[...]
    lane = lax.broadcasted_iota(jnp.int32, (tm, LANES), 1)
    cos_ref[...] = jnp.cos(ang)
    sin_ref[...] = jnp.sin(ang) * jnp.where(lane < LANES // 2, -1.0, 1.0)


def _rope_tables(s):
    half = HEAD_DIM // 2
    inv = ROPE_THETA ** (-(jnp.arange(LANES) % half).astype(F32) / half)
    tm = 512
    return pl.pallas_call(
        _rope_table_kernel,
        grid=(s // tm,),
        in_specs=[pl.BlockSpec((1, LANES), lambda i: (0, 0))],
        out_specs=[pl.BlockSpec((tm, LANES), lambda i: (i, 0))] * 2,
        out_shape=[jax.ShapeDtypeStruct((s, LANES), F32)] * 2,
        name="rope_tables",
    )(inv.reshape(1, LANES))


def _norm_mod_kernel(x_ref, mod_ref, g_ref, o_ref):
    h = _rms(x_ref[0], g_ref[...]) * (1.0 + mod_ref[0, 1:2, :]) + mod_ref[0, 0:1, :]
    o_ref[0] = h.astype(o_ref.dtype)


def _norm_mod(x, mod, g):
    b, s, d = x.shape
    tm = 1024
    return pl.pallas_call(
        _norm_mod_kernel,
        grid=(b, s // tm),
        in_specs=[pl.BlockSpec((1, tm, d), lambda bi, i: (bi, i, 0)),
                  pl.BlockSpec((1, 6, d), lambda bi, i: (bi, 0, 0)),
                  pl.BlockSpec((1, d), lambda bi, i: (0, 0))],
        out_specs=pl.BlockSpec((1, tm, d), lambda bi, i: (bi, i, 0)),
        out_shape=jax.ShapeDtypeStruct((b, s, d), BF16),
        name="norm_mod",
    )(x, mod, g)


def _mm_kernel(h_ref, w_ref, o_ref):
    o_ref[0] = jnp.dot(h_ref[0], w_ref[...], preferred_element_type=F32).astype(o_ref.dtype)


def _mm(h, w, out_dtype):
    b, s, k = h.shape
    n = w.shape[1]
    tm, tn = 1024, 1024
    return pl.pallas_call(
        _mm_kernel,
        grid=(b, s // tm, n // tn),
        in_specs=[pl.BlockSpec((1, tm, k), lambda bi, i, j: (bi, i, 0)),
                  pl.BlockSpec((k, tn), lambda bi, i, j: (0, j))],
        out_specs=pl.BlockSpec((1, tm, tn), lambda bi, i, j: (bi, i, j)),
        out_shape=jax.ShapeDtypeStruct((b, s, n), out_dtype),
        compiler_params=_cparams(),
        name="proj",
    )(h, w)


def _proj_attn_kernel(h_ref, w_ref, cos_ref, sin_ref, o_ref, *scratch, dil):
    is_qk = pl.program_id(2) < 2
    tm = h_ref.shape[1]
    n_slab = w_ref.shape[1] // LANES
    for s0 in range(0, tm, MM_SUB):
        rows = slice(s0, s0 + MM_SUB)
        acc = jnp.dot(h_ref[0, rows, :], w_ref[...], preferred_element_type=F32)
        cos = jnp.where(is_qk, cos_ref[rows, :], 1.0)
        sin = jnp.where(is_qk, sin_ref[rows, :], 0.0)
        for c in range(n_slab):
            t = acc[:, c * LANES:(c + 1) * LANES]
            val = t * cos + pltpu.roll(t, LANES // 2, 1) * sin
            if dil == 1:
                o_ref[0, 0, 0, rows, c * LANES:(c + 1) * LANES] = val.astype(o_ref.dtype)
            else:
                scratch[0][c, rows, :] = val
        if dil > 1:
            n = MM_SUB // dil
            m0 = s0 // dil
            for r in range(dil):
                for c in range(n_slab):
                    o_ref[0, 0, r, m0:m0 + n, c * LANES:(c + 1) * LANES] = (
                        scratch[0][c, pl.ds(s0 + r, n, stride=dil), :].astype(o_ref.dtype))


def _proj_attn(h, w, rope, dil, gi):
    b, s, k = h.shape
    tm, tn = 2048, GROUP_W
    scratch = [pltpu.VMEM((tn // LANES, tm, LANES), F32)] if dil > 1 else []
    return pl.pallas_call(
        functools.partial(_proj_attn_kernel, dil=dil),
        grid=(b, s // tm, 3),
        in_specs=[pl.BlockSpec((1, tm, k), lambda bi, i, j: (bi, i, 0)),
                  pl.BlockSpec((k, tn), lambda bi, i, j: (0, j)),
                  pl.BlockSpec((tm, LANES), lambda bi, i, j: (i, 0)),
                  pl.BlockSpec((tm, LANES), lambda bi, i, j: (i, 0))],
        out_specs=pl.BlockSpec((1, 1, dil, tm // dil, tn), lambda bi, i, j: (bi, j, 0, i, 0)),
        out_shape=jax.ShapeDtypeStruct((b, 3, dil, s // dil, tn), BF16),
        scratch_shapes=scratch,
        compiler_params=_cparams(),
        name=f"proj_attn_g{gi}",
    )(h, w, *rope)


def _gates_kernel(h_ref, w_ref, a_ref, dt_ref, o_ref, gt_ref):
    c = DN_CHUNK
    r = lax.broadcasted_iota(jnp.int32, (c, c), 0)
    cc = lax.broadcasted_iota(jnp.int32, (c, c), 1)
    lower = jnp.where(cc <= r, 1.0, 0.0).astype(F32)
    upper = jnp.where(cc >= r, 1.0, 0.0).astype(F32)
    lane = lax.broadcasted_iota(jnp.int32, (c, LANES), 1)
    for j in range(h_ref.shape[1] // c):
        acc = jnp.dot(h_ref[0, j * c:(j + 1) * c, :], w_ref[...], preferred_element_type=F32)
        x = acc + dt_ref[...]
        softplus = jnp.maximum(x, 0.0) + jnp.log1p(jnp.exp(-jnp.abs(x)))
        g = -jnp.exp(a_ref[...]) * softplus
        beta = _sigmoid(acc)
        pre = jnp.dot(lower, g, preferred_element_type=F32, precision=HIGHEST)
        suf = jnp.dot(upper, g, preferred_element_type=F32, precision=HIGHEST)
        out = jnp.where(lane < DN_HEADS, pre,
                        jnp.where(lane < 2 * DN_HEADS, suf,
                                  jnp.where(lane < 4 * DN_HEADS, beta, 0.0)))
        o_ref[0, j * c:(j + 1) * c, :] = out
        gt_ref[0, j] = out.T[0:2 * DN_HEADS, :]


def _gates(h, w, a_row, dt_row):
    b, s, k = h.shape
    tm = 4 * DN_CHUNK
    return pl.pallas_call(
        _gates_kernel,
        grid=(b, s // tm),
        in_specs=[pl.BlockSpec((1, tm, k), lambda bi, i: (bi, i, 0)),
                  pl.BlockSpec((k, LANES), lambda bi, i: (0, 0)),
                  pl.BlockSpec((1, LANES), lambda bi, i: (0, 0)),
                  pl.BlockSpec((1, LANES), lambda bi, i: (0, 0))],
        out_specs=[pl.BlockSpec((1, tm, LANES), lambda bi, i: (bi, i, 0)),
                   pl.BlockSpec((1, tm // DN_CHUNK, 2 * DN_HEADS, DN_CHUNK),
                                lambda bi, i: (bi, i, 0, 0))],
        out_shape=[jax.ShapeDtypeStruct((b, s, LANES), F32),
                   jax.ShapeDtypeStruct((b, s // DN_CHUNK, 2 * DN_HEADS, DN_CHUNK), F32)],
        name="dn_gates",
    )(h, w, a_row, dt_row)


def _attn_kernel(q_ref, k_ref, v_ref, o_ref, l_ref, *scratch, seq, bq, nq, kb, dil):
    n_slab = GROUP_W // LANES
    lane = lax.broadcasted_iota(jnp.int32, (bq, LANES), 1)
    blocks = []
    for qb in range(nq):
        row0 = qb * bq
        q0 = (pl.program_id(1) * nq + qb) * bq
        ks = pl.multiple_of(jnp.clip(q0 - ATTN_HALF, 0, seq - kb), ATTN_HALF)
        qpos = q0 + lax.broadcasted_iota(jnp.int32, (bq, kb), 0)
        kpos = ks + lax.broadcasted_iota(jnp.int32, (bq, kb), 1)
        valid = jnp.abs(qpos - kpos) <= ATTN_HALF

        def residue(r, row0=row0, ks=ks, valid=valid):
            q = q_ref[0, 0, r, row0:row0 + bq, :].astype(F32) * (HEAD_DIM ** -0.5)
            k = k_ref[0, 0, r, pl.ds(ks, kb), :]
            v = v_ref[0, 0, r, pl.ds(ks, kb), :]
            heads = [(hp, sub) for hp in range(n_slab) for sub in range(2)]
            in_head = [lane < HEAD_DIM, lane >= HEAD_DIM]
            qk_lanes = [(lane & (HEAD_DIM // 2)) == 0, (lane & (HEAD_DIM // 2)) != 0]
            slab = lambda x, hp: x[:, hp * LANES:(hp + 1) * LANES]
            s_ = [_dot_nt(jnp.where(qk_lanes[sub], slab(q, hp), 0.0), slab(k, hp)) for hp, sub in heads]
            s_ = [jnp.where(valid, s, NEG) for s in s_]
            mx_ = [jnp.max(s, axis=-1, keepdims=True) for s in s_]
            p_ = [jnp.exp(s - mx) for s, mx in zip(s_, mx_)]
            den_ = [jnp.sum(p, axis=-1, keepdims=True) for p in p_]
            o_ = [_dot(p, slab(v, hp)) / den for p, den, (hp, _) in zip(p_, den_, heads)]
            lse_ = [mx + jnp.log(den) for mx, den in zip(mx_, den_)]
            for hp in range(n_slab):
                sl = slice(hp * LANES, (hp + 1) * LANES)
                o_pair = jnp.where(in_head[1], o_[2 * hp + 1], o_[2 * hp])
                l_pair = jnp.where(in_head[1], lse_[2 * hp + 1], lse_[2 * hp])
                if dil == 1:
                    o_ref[0, row0:row0 + bq, sl] = o_pair.astype(o_ref.dtype)
                    l_ref[0, row0:row0 + bq, sl] = l_pair
                else:
                    scratch[0][hp, pl.ds(row0 * dil + r, bq, stride=dil), :] = o_pair
                    scratch[1][hp, pl.ds(row0 * dil + r, bq, stride=dil), :] = l_pair

        blocks.append(residue)

    if dil == 1:
        for block in blocks:
            block(0)
    else:
        def body(r, carry):
            for block in blocks:
                block(r)
            return carry
        lax.fori_loop(0, dil, body, 0, unroll=max(1, ATTN_MAX_BLOCKS // nq))
    if dil > 1:
        for hp in range(n_slab):
            sl = slice(hp * LANES, (hp + 1) * LANES)
            o_ref[0, :, sl] = scratch[0][hp].astype(o_ref.dtype)
            l_ref[0, :, sl] = scratch[1][hp]


def _attn(qkv, gi, dil):
    b, _, _, seq, _ = qkv.shape
    s = seq * dil
    bq = min(ATTN_BQ, seq)
    kb = min(bq + 2 * ATTN_HALF, seq)
    nq = 1
    while 2 * nq <= ATTN_MAX_BLOCKS and 2 * nq * bq * dil <= ATTN_MAX_ROWS and seq % (2 * nq * bq) == 0:
        nq *= 2
    rows = nq * bq * dil
    scratch = [pltpu.VMEM((GROUP_W // LANES, rows, LANES), F32)] * 2 if dil > 1 else []
    kv = lambda kind: pl.BlockSpec((1, 1, dil, seq, GROUP_W), lambda bi, m: (bi, kind, 0, 0, 0))
    return pl.pallas_call(
        functools.partial(_attn_kernel, seq=seq, bq=bq, nq=nq, kb=kb, dil=dil),
        grid=(b, seq // (nq * bq)),
        in_specs=[pl.BlockSpec((1, 1, dil, nq * bq, GROUP_W), lambda bi, m: (bi, 0, 0, m, 0)),
                  kv(1), kv(2)],
        out_specs=[pl.BlockSpec((1, rows, GROUP_W), lambda bi, m: (bi, m, 0))] * 2,
        out_shape=[jax.ShapeDtypeStruct((b, s, GROUP_W), BF16),
                   jax.ShapeDtypeStruct((b, s, GROUP_W), F32)],
        scratch_shapes=scratch,
        compiler_params=_cparams(),
        name=f"attn_g{gi}",
    )(qkv, qkv, qkv)


def _tri_inverse(a_list, masks_ref, lmask_ref, bases, uppers, eye):
    ts = [eye - a * masks_ref[b] for a, b in zip(a_list, bases)]
    a_list = [a.astype(BF16) for a in a_list]
    ps = [-(a * lmask_ref[b]) for a, b in zip(a_list, bases)]
    n = 2
    while n < DN_BASE:
        ps = [_dot(p, p).astype(BF16) for p in ps]
        ts = [t + _dot(t, p) for t, p in zip(ts, ps)]
        n *= 2
    n = DN_BASE
    lvl = 1
    while 2 * n < DN_CHUNK:
        left = [_dot(t, a * lmask_ref[b + lvl]) for t, a, b in zip(ts, a_list, bases)]
        ts = [t - _dot(l, t) for t, l in zip(ts, left)]
        n *= 2
        lvl += 1
    t11 = [t[0:n, 0:n] for t in ts]
    t22 = [t[n:, n:] for t in ts]
    zero = jnp.zeros((n, n), F32)
    left = [_dot(x11, a[0:n, n:]) if up else _dot(x22, a[n:, 0:n])
            for x11, x22, a, up in zip(t11, t22, a_list, uppers)]
    off = [-_dot(l, x22) if up else -_dot(l, x11) for l, x11, x22, up in zip(left, t11, t22, uppers)]
    return [jnp.concatenate([jnp.concatenate([x11, o if up else zero], axis=1),
                             jnp.concatenate([zero if up else o, x22], axis=1)], axis=0)
            for x11, x22, o, up in zip(t11, t22, off, uppers)]


def _dn_kernel(q_ref, k_ref, v_ref, z_ref, cwq_ref, cwk_ref, cwv_ref, gates_ref, gt_ref, ng_ref,
               o_ref, acc_ref, qkv_a_ref, qkv_b_ref, masks_ref, lmask_ref, u_ref, wq_ref, cs_ref, ns_ref, p_ref,
               gl_ref, *, seq):
    h = pl.program_id(1)
    c = DN_CHUNK
    nc = seq // c
    n_lvl = 1
    while DN_BASE << n_lvl < c:
        n_lvl += 1

    r = lax.broadcasted_iota(jnp.int32, (c, c), 0)
    cc = lax.broadcasted_iota(jnp.int32, (c, c), 1)
    as_f32 = lambda m: jnp.where(m, 1.0, 0.0).astype(F32)
    eye = as_f32(r == cc)
    per_dir = 2 + n_lvl
    base_shift = DN_BASE.bit_length() - 1

    @pl.when((pl.program_id(0) == 0) & (h == 0))
    def _():
        for d, (lo, hi) in enumerate(((cc, r), (r, cc))):
            masks_ref[d * per_dir + 0] = as_f32(lo <= hi)
            masks_ref[d * per_dir + 1] = as_f32(lo < hi)
            base_mask = as_f32((lo < hi) & ((lo >> base_shift) == (hi >> base_shift)))
            masks_ref[d * per_dir + 2] = base_mask
            lmask_ref[d * per_dir + 2] = base_mask.astype(BF16)
            for lvl in range(1, n_lvl):
                sh = base_shift + lvl - 1
                lo_blk, hi_blk = lo >> sh, hi >> sh
                lmask_ref[d * per_dir + 2 + lvl] = as_f32(
                    (hi_blk == lo_blk + 1) & ((lo_blk & 1) == 0)).astype(BF16)

    n_groups = nc // DN_GROUP

    def prep_group(g, dst_ref):
        halo = DN_HALO
        n = c + 2 * halo
        for j in range(DN_GROUP):
            ci = DN_GROUP * g + j
            r0 = pl.multiple_of(ci * c, c)
            lo = pl.multiple_of(jnp.maximum(r0 - halo, 0), halo)
            hi = pl.multiple_of(jnp.minimum(r0 + c, seq - halo), halo)
            keep_lo = jnp.where(ci > 0, 1.0, 0.0)
            keep_hi = jnp.where(ci < nc - 1, 1.0, 0.0)
            for idx, (x_ref, w_ref, kind) in enumerate(((q_ref, cwq_ref, "q"), (k_ref, cwk_ref, "k"),
                                                        (v_ref, cwv_ref, "v"))):
                x = jnp.concatenate([x_ref[0, pl.ds(lo, halo), :].astype(F32) * keep_lo,
                                     x_ref[0, pl.ds(r0, c), :].astype(F32),
                                     x_ref[0, pl.ds(hi, halo), :].astype(F32) * keep_hi], axis=0)
                w = w_ref[...]
                y = (pltpu.roll(x, 1, 0) * w[0:1, :] + x * w[1:2, :]
                     + pltpu.roll(x, n - 1, 0) * w[2:3, :])[halo:halo + c]
                y = y * _sigmoid(y)
                if kind != "v":
                    y = y * lax.rsqrt(jnp.sum(y * y, axis=-1, keepdims=True) + EPS)
                if kind == "q":
                    y = y * (DN_HEAD_DIM ** -0.5)
                dst_ref[idx, j * c:(j + 1) * c, :] = y

    prep_group(jnp.int32(0), qkv_a_ref)

    lane = lax.broadcasted_iota(jnp.int32, (c, LANES), 1)

    def solve_group(g, src_ref):
        sys_, a_list, bases = [], [], []
        for j in range(DN_GROUP):
            ci = DN_GROUP * g + j
            r0 = pl.multiple_of(ci * c, c)
            q = src_ref[0, j * c:(j + 1) * c, :]
            k = src_ref[1, j * c:(j + 1) * c, :]
            v = src_ref[2, j * c:(j + 1) * c, :]
            gch = gates_ref[0, pl.ds(r0, c), :]
            kk = _dot_nt(k, k)
            qk = _dot_nt(q, k)
            for d in range(2):
                col = lambda j: jnp.sum(jnp.where(lane == j, gch, 0.0), axis=1, keepdims=True)
                gc = col(d * DN_HEADS + h)
                beta = col((2 + d) * DN_HEADS + h)
                grow = gt_ref[0, ci, pl.ds(d * DN_HEADS + h, 1), :]
                g_end = gc[0:1, :] if d else gc[c - 1:c, :]
                dec = jnp.exp(jnp.minimum(gc - grow, 0.0))
                e_g = jnp.exp(gc)
                a_list.append(kk * (beta * dec) * masks_ref[d * per_dir + 1])
                bases.append(d * per_dir + 2)
                sys_.append(dict(
                    ci=ci, r0=r0, d=d,
                    rhs=jnp.concatenate([v * beta, k * (beta * e_g)], axis=1),
                    qg=(q * e_g).astype(BF16),
                    kd_t=(k * jnp.exp(g_end - gc)).T.astype(BF16),
                    p=(qk * dec * masks_ref[d * per_dir]).astype(BF16),
                    gl=jnp.broadcast_to(jnp.exp(g_end), (1, LANES))))
        t_list = _tri_inverse(a_list, masks_ref, lmask_ref, bases, [s["d"] == 1 for s in sys_], eye)
        uw_list = [_dot(t, s["rhs"]) for t, s in zip(t_list, sys_)]
        nc_list = [_dot(s["kd_t"], uw) for s, uw in zip(sys_, uw_list)]
        for s, uw, n_c in zip(sys_, uw_list, nc_list):
            d, ci, r0 = s["d"], s["ci"], s["r0"]
            u_ref[d, pl.ds(r0, c), :] = uw[:, 0:LANES]
            wq_ref[d, ci, 0:c, :] = uw[:, LANES:2 * LANES].astype(BF16)
            wq_ref[d, ci, c:2 * c, :] = s["qg"]
            ns_ref[d, ci] = n_c[:, 0:LANES]
            cs_ref[d, ci] = n_c[:, LANES:2 * LANES].astype(BF16)
            p_ref[d, ci] = s["p"]
            gl_ref[d, pl.ds(ci, 1), :] = s["gl"]

    def phase_a(m, carry):
        prep_group(2 * m + 1, qkv_b_ref)
        solve_group(2 * m, qkv_a_ref)
        prep_group(jnp.minimum(2 * m + 2, n_groups - 1), qkv_a_ref)
        solve_group(2 * m + 1, qkv_b_ref)
        return carry

    lax.fori_loop(0, n_groups // 2, phase_a, 0)

    def phase_b(i, carry, second_half):
        cis = (i, nc - 1 - i)
        r0s = [pl.multiple_of(ci * c, c) for ci in cis]
        s16 = [carry[d].astype(BF16) for d in range(2)]
        new = [carry[d] * gl_ref[d, pl.ds(cis[d], 1), :] + ns_ref[d, cis[d]]
               - jnp.dot(cs_ref[d, cis[d]], s16[d], preferred_element_type=F32) for d in range(2)]
        ws_qs = [jnp.dot(wq_ref[d, cis[d]], s16[d], preferred_element_type=F32) for d in range(2)]
        v_new = [(u_ref[d, pl.ds(r0s[d], c), :] - ws_qs[d][0:c]).astype(BF16) for d in range(2)]
        outs = [ws_qs[d][c:2 * c] + jnp.dot(p_ref[d, cis[d]], v_new[d], preferred_element_type=F32)
                for d in range(2)]
        for d in range(2):
            rows = pl.ds(r0s[d], c)
            if not second_half:
                acc_ref[rows, :] = outs[d]
            else:
                z = z_ref[0, rows, :].astype(F32)
                y = _rms(acc_ref[rows, :] + outs[d], ng_ref[...]) * (z * _sigmoid(z))
                o_ref[0, rows, :] = y.astype(o_ref.dtype)
        return tuple(new)

    zero = jnp.zeros((DN_HEAD_DIM, DN_HEAD_DIM), F32)
    mid = lax.fori_loop(0, nc // 2, functools.partial(phase_b, second_half=False), (zero, zero),
                        unroll=DN_B_UNROLL)
    lax.fori_loop(nc // 2, nc, functools.partial(phase_b, second_half=True), mid, unroll=DN_B_UNROLL)


def _deltanet(qkvd, gates, gt, conv_w, norm_g):
    b, s, _ = qkvd.shape
    hd = DN_HEAD_DIM
    nc = s // DN_CHUNK
    n_lvl = 1
    while DN_BASE << n_lvl < DN_CHUNK:
        n_lvl += 1
    n_masks = 2 * (2 + n_lvl)
    col = lambda off: pl.BlockSpec((1, s, hd), lambda bi, h: (bi, 0, off + h))
    cw = lambda off: pl.BlockSpec((3, hd), lambda bi, h: (0, off + h))
    return pl.pallas_call(
        functools.partial(_dn_kernel, seq=s),
        grid=(b, DN_HEADS),
        in_specs=[col(0), col(DN_HEADS), col(2 * DN_HEADS), col(3 * DN_HEADS),
                  cw(0), cw(DN_HEADS), cw(2 * DN_HEADS),
                  pl.BlockSpec((1, s, LANES), lambda bi, h: (bi, 0, 0)),
                  pl.BlockSpec((1, nc, 2 * DN_HEADS, DN_CHUNK), lambda bi, h: (bi, 0, 0, 0)),
                  pl.BlockSpec((1, hd), lambda bi, h: (0, 0))],
        out_specs=pl.BlockSpec((1, s, hd), lambda bi, h: (bi, 0, h)),
        out_shape=jax.ShapeDtypeStruct((b, s, DN_W), BF16),
        scratch_shapes=[pltpu.VMEM((s, hd), F32),
                        pltpu.VMEM((3, DN_GROUP * DN_CHUNK, hd), F32),
                        pltpu.VMEM((3, DN_GROUP * DN_CHUNK, hd), F32),
                        pltpu.VMEM((n_masks, DN_CHUNK, DN_CHUNK), F32),
                        pltpu.VMEM((n_masks, DN_CHUNK, DN_CHUNK), BF16),
                        pltpu.VMEM((2, s, hd), F32),
                        pltpu.VMEM((2, nc, 2 * DN_CHUNK, hd), BF16),
                        pltpu.VMEM((2, nc, hd, hd), BF16),
                        pltpu.VMEM((2, nc, hd, hd), F32),
                        pltpu.VMEM((2, nc, DN_CHUNK, DN_CHUNK), BF16),
                        pltpu.VMEM((2, max(nc, 8), hd), F32)],
        compiler_params=_cparams(dimension_semantics=("arbitrary", "arbitrary")),
        name="deltanet",
    )(qkvd, qkvd, qkvd, qkvd, conv_w, conv_w, conv_w, gates, gt, norm_g)


def _merge_kernel(o1_ref, o2_ref, o3_ref, l1_ref, l2_ref, l3_ref, yd_ref, h1_ref, x_ref, mod_ref,
                  wg_ref, wba_ref, wbd_ref, wo_ref, g2_ref, x1_ref, h2_ref):
    d = D_MODEL
    for s0 in range(0, x_ref.shape[1], MERGE_SUB):
        rows = slice(s0, s0 + MERGE_SUB)
        l1, l2, l3 = l1_ref[0, rows, :], l2_ref[0, rows, :], l3_ref[0, rows, :]
        mx = jnp.maximum(l1, jnp.maximum(l2, l3))
        e1, e2, e3 = jnp.exp(l1 - mx), jnp.exp(l2 - mx), jnp.exp(l3 - mx)
        ya = (e1 * o1_ref[0, rows, :] + e2 * o2_ref[0, rows, :] + e3 * o3_ref[0, rows, :]) / (e1 + e2 + e3)
        h1 = h1_ref[0, rows, :]
        gate_a = _sigmoid(jnp.dot(h1, wg_ref[:, 0:d], preferred_element_type=F32))
        gate_d = _sigmoid(jnp.dot(h1, wg_ref[:, d:2 * d], preferred_element_type=F32))
        merged = gate_a * _dot(ya, wba_ref[...]) + gate_d * _dot(yd_ref[0, rows, :], wbd_ref[...])
        x1 = x_ref[0, rows, :] + mod_ref[0, 2:3, :] * _dot(merged, wo_ref[...])
        x1_ref[0, rows, :] = x1
        h2 = _rms(x1, g2_ref[...]) * (1.0 + mod_ref[0, 4:5, :]) + mod_ref[0, 3:4, :]
        h2_ref[0, rows, :] = h2.astype(h2_ref.dtype)


def _merge_out(os_, ls_, yd, h1, x, mod, wg, wba, wbd, wo, g2):
    b, s, d = x.shape
    tm = 2 * MERGE_SUB
    row = lambda w: pl.BlockSpec((1, tm, w), lambda bi, i: (bi, i, 0))
    full = lambda a: pl.BlockSpec(a.shape, lambda bi, i: (0,) * a.ndim)
    return pl.pallas_call(
        _merge_kernel,
        grid=(b, s // tm),
        in_specs=[row(GROUP_W)] * 6 + [row(DN_W), row(d), row(d),
                                       pl.BlockSpec((1, 6, d), lambda bi, i: (bi, 0, 0)),
                                       full(wg), full(wba), full(wbd), full(wo), full(g2)],
        out_specs=[row(d), row(d)],
        out_shape=[jax.ShapeDtypeStruct((b, s, d), F32), jax.ShapeDtypeStruct((b, s, d), BF16)],
        compiler_params=_cparams(),
        name="merge_out",
    )(*os_, *ls_, yd, h1, x, mod, wg, wba, wbd, wo, g2)


FFN_HALO = 16
FFN_SUB = 512


def _ffn_up_kernel(hp_ref, hm_ref, hn_ref, wv_ref, wg_ref, cwv_ref, cwg_ref, bv_ref, bg_ref,
                   o_ref, *, nt):
    i = pl.program_id(1)
    tm = hm_ref.shape[1]
    lhs_all = jnp.concatenate([hp_ref[0], hm_ref[0], hn_ref[0]], axis=0)
    sub = FFN_SUB
    n = sub + 2 * FFN_HALO
    for s0 in range(0, tm, sub):
        lhs = lhs_all[s0:s0 + n]
        keep_prev = jnp.where(i > 0, 1.0, 0.0) if s0 == 0 else None
        keep_next = jnp.where(i < nt - 1, 1.0, 0.0) if s0 + sub == tm else None
        ups = [jnp.dot(lhs, w_ref[...], preferred_element_type=F32) for w_ref in (wv_ref, wg_ref)]

        def conv(up, cw_ref, b_ref):
            head, tail = up[0:FFN_HALO], up[FFN_HALO + sub:]
            if keep_prev is not None:
                head = head * keep_prev
            if keep_next is not None:
                tail = tail * keep_next
            up = jnp.concatenate([head, up[FFN_HALO:FFN_HALO + sub], tail], axis=0)
            cw = cw_ref[...]
            y = (pltpu.roll(up, 1, 0) * cw[0:1, :] + up * cw[1:2, :]
                 + pltpu.roll(up, n - 1, 0) * cw[2:3, :])
            return y[FFN_HALO:FFN_HALO + sub] + b_ref[...]

        val = conv(ups[0], cwv_ref, bv_ref)
        gate = conv(ups[1], cwg_ref, bg_ref)
        o_ref[0, s0:s0 + sub, :] = (gate * _sigmoid(gate) * val).astype(o_ref.dtype)


def _ffn_up(h2, w_up, conv_w, conv_b):
    b, s, d = h2.shape
    tm, tn = 2048, 256
    nt = s // tm
    nj = D_FF // tn
    hb = tm // FFN_HALO
    return pl.pallas_call(
        functools.partial(_ffn_up_kernel, nt=nt),
        grid=(b, nt, nj),
        in_specs=[pl.BlockSpec((1, FFN_HALO, d), lambda bi, i, j: (bi, jnp.maximum(i * hb - 1, 0), 0)),
                  pl.BlockSpec((1, tm, d), lambda bi, i, j: (bi, i, 0)),
                  pl.BlockSpec((1, FFN_HALO, d),
                               lambda bi, i, j: (bi, jnp.minimum((i + 1) * hb, s // FFN_HALO - 1), 0)),
                  pl.BlockSpec((d, tn), lambda bi, i, j: (0, j)),
                  pl.BlockSpec((d, tn), lambda bi, i, j: (0, j + nj)),
                  pl.BlockSpec((3, tn), lambda bi, i, j: (0, j)),
                  pl.BlockSpec((3, tn), lambda bi, i, j: (0, j + nj)),
                  pl.BlockSpec((1, tn), lambda bi, i, j: (0, j)),
                  pl.BlockSpec((1, tn), lambda bi, i, j: (0, j + nj))],
        out_specs=pl.BlockSpec((1, tm, tn), lambda bi, i, j: (bi, i, j)),
        out_shape=jax.ShapeDtypeStruct((b, s, D_FF), BF16),
        compiler_params=_cparams(),
        name="ffn_up",
    )(h2, h2, h2, w_up, w_up, conv_w, conv_w, conv_b, conv_b)


def _ffn_down_kernel(a_ref, w_ref, x_ref, mod_ref, g_ref, o_ref):
    for s0 in range(0, x_ref.shape[1], FFN_SUB):
        rows = slice(s0, s0 + FFN_SUB)
        x2 = x_ref[0, rows, :] + mod_ref[0, 5:6, :] * jnp.dot(a_ref[0, rows, :], w_ref[...],
                                                             preferred_element_type=F32)
        o_ref[0, rows, :] = _rms(x2, g_ref[...])


def _ffn_down(act, w_down, x1, mod, g):
    b, s, d = x1.shape
    tm = 1024
    return pl.pallas_call(
        _ffn_down_kernel,
        grid=(b, s // tm),
        in_specs=[pl.BlockSpec((1, tm, D_FF), lambda bi, i: (bi, i, 0)),
                  pl.BlockSpec((D_FF, d), lambda bi, i: (0, 0)),
                  pl.BlockSpec((1, tm, d), lambda bi, i: (bi, i, 0)),
                  pl.BlockSpec((1, 6, d), lambda bi, i: (bi, 0, 0)),
                  pl.BlockSpec((1, d), lambda bi, i: (0, 0))],
        out_specs=pl.BlockSpec((1, tm, d), lambda bi, i: (bi, i, 0)),
        out_shape=jax.ShapeDtypeStruct((b, s, d), F32),
        compiler_params=_cparams(),
        name="ffn_down",
    )(act, w_down, x1, mod, g)


def _qk_column_order():
    half = HEAD_DIM // 2
    return [(pair * 2 + j) * HEAD_DIM + f * half + i
            for pair in range(GROUP_W // LANES) for f in range(2) for j in range(2)
            for i in range(half)]


_QK_COLS = np.asarray(_qk_column_order(), np.int32)


def _trunk(x, mod, p, rope):
    b, s, _ = x.shape
    h1 = _norm_mod(x, mod, p["norm1_g"])
    qkvd = _mm(h1, p["w_qkvzd"], BF16)
    gates, gt = _gates(h1, p["w_ab"], p["a_row"], p["dt_row"])
    os_, ls_ = [], []
    for gi, (_, dil) in enumerate(ATTN_GROUPS):
        o, l = _attn(_proj_attn(h1, p["w_attn"][gi], rope, dil, gi), gi, dil)
        os_.append(o)
        ls_.append(l)
    yd = _deltanet(qkvd, gates, gt, p["conv_qkv_w"], p["dn_norm_g"])
    x1, h2 = _merge_out(os_, ls_, yd, h1, x, mod, p["w_gate"], p["w_br_attn"], p["w_br_dn"],
                        p["w_out"], p["norm2_g"])
    act = _ffn_up(h2, p["w_up"], p["ffn_conv_w"], p["ffn_conv_b"])
    return _ffn_down(act, p["w_down"], x1, mod, p["norm_f_g"])


def kernel(x_prompt, x_sample, c_prompt, c_sample, w_ada, b_ada, norm1_g, w_in, conv_qkv_w, a_log, dt_bias, dn_norm_g, w_br_attn, w_br_dn, w_out, norm2_g, w_up, ffn_conv_w, ffn_conv_b, w_down, norm_f_g):
    d = D_MODEL
    assert w_ada.shape[0] == 1, "single layer"
    w = w_in[0]
    o_qd = 3 * ATTN_W
    o_zd = o_qd + 3 * DN_W
    o_ab = o_zd + DN_W
    o_gate = o_ab + 4 * DN_HEADS
    pad16 = lambda v: jnp.pad(v.reshape(1, 2 * DN_HEADS).astype(F32), ((0, 0), (0, LANES - 2 * DN_HEADS)))
    p = {
        "norm1_g": norm1_g[0].reshape(1, d),
        "w_attn": [jnp.concatenate(
            [w[:, kind * ATTN_W + gi * GROUP_W:kind * ATTN_W + (gi + 1) * GROUP_W][:, cols]
             for kind, cols in ((0, _QK_COLS), (1, _QK_COLS), (2, slice(None)))], axis=1).astype(BF16)
            for gi in range(N_GROUPS)],
        "w_qkvzd": w[:, o_qd:o_ab].astype(BF16),
        "w_ab": jnp.pad(w[:, o_ab:o_gate], ((0, 0), (0, LANES - 4 * DN_HEADS))).astype(BF16),
        "w_gate": w[:, o_gate:].astype(BF16),
        "a_row": pad16(a_log[0]),
        "dt_row": pad16(dt_bias[0]),
        "conv_qkv_w": conv_qkv_w[0],
        "dn_norm_g": dn_norm_g[0].reshape(1, DN_HEAD_DIM),
        "w_br_attn": w_br_attn[0].astype(BF16),
        "w_br_dn": w_br_dn[0].astype(BF16),
        "w_out": w_out[0].astype(BF16),
        "norm2_g": norm2_g[0].reshape(1, d),
        "w_up": w_up[0].astype(BF16),
        "ffn_conv_w": ffn_conv_w[0],
        "ffn_conv_b": ffn_conv_b[0].reshape(1, 2 * D_FF),
        "w_down": w_down[0].astype(BF16),
        "norm_f_g": norm_f_g.reshape(1, d),
    }
    nb = x_prompt.shape[0]
    mod = _ada(jnp.concatenate([c_prompt, c_sample], axis=0), w_ada[0], b_ada[0].reshape(1, 6 * d))
    mod = mod.reshape(-1, 6, d)
    rope = _rope_tables(max(x_prompt.shape[1], x_sample.shape[1]))
    y_prompt = _trunk(x_prompt, mod[:nb], p, rope)
    y_sample = _trunk(x_sample, mod[nb:], p, rope)
    return (y_prompt, y_sample)
```
